```python
import math
import jax
import jax.numpy as jnp
from jax import lax
import numpy as np

D_MODEL = 1024
BATCH = 2
SEQ = 8192
DEPTH = 4

ATTN_HEADS = 8
HEAD_DIM = 64
D_ATTN = ATTN_HEADS * HEAD_DIM
RNN_BLOCKS = 8
D_RNN = 512
RNN_BLOCK_W = D_RNN // RNN_BLOCKS
D_MIX = D_ATTN + D_RNN
D_IN = 3 * D_ATTN + 2 * D_RNN
DILATED_BRANCHES = ((128, 1), (512, 4), (2048, 16))
Q_BLOCK = 128
REL_BUCKETS = 32
REL_MAX_DIST = 2048
CONV_WIDTH = 4
RG_C = 8.0
N_GROUPS = 4
EXPERTS_PER_GROUP = 8
N_EXPERTS = N_GROUPS * EXPERTS_PER_GROUP
TOP_K = 2
D_EXPERT = 512
MOE_BLOCK = 128
EPS = 1e-6
NEG_INF = -1e30

kernel_name = 'hymba_dilated_rglru_hmoe'


def rms_norm(x, gain):
    xf = x.astype(jnp.float32)
    y = xf * lax.rsqrt(jnp.mean(xf * xf, axis=-1, keepdims=True) + EPS)
    return (y * gain.astype(jnp.float32)).astype(x.dtype)


def t5_bucket(dist):
    exact = REL_BUCKETS // 2
    d = jnp.maximum(dist, 1).astype(jnp.float32)
    log_b = exact + (jnp.log(d / exact) / math.log(REL_MAX_DIST / exact) * (REL_BUCKETS - exact)).astype(jnp.int32)
    return jnp.where(dist < exact, dist, jnp.minimum(log_b, REL_BUCKETS - 1))


def dilated_branch(q, k, v, rel_table, window, dil):
    b, s, h, e = q.shape
    span = dil * Q_BLOCK
    s_pad = -(-s // span) * span
    n_blk = s_pad // span
    sub_len = s_pad // dil
    pad = ((0, 0), (0, s_pad - s), (0, 0), (0, 0))
    qb = jnp.pad(q, pad).reshape(b, n_blk, Q_BLOCK, dil, h, e)

    def key_windows(t):
        t = jnp.pad(t, pad).reshape(b, sub_len, dil, h, e)
        t = jnp.pad(t, ((0, 0), (Q_BLOCK, 0), (0, 0), (0, 0), (0, 0)))
        t = t.reshape(b, n_blk + 1, Q_BLOCK, dil, h, e)
        return jnp.concatenate([t[:, :-1], t[:, 1:]], axis=2)

    kw = key_windows(k)
    vw = key_windows(v)
    j_idx = jnp.arange(2 * Q_BLOCK)
    steps = jnp.arange(Q_BLOCK)[:, None] + Q_BLOCK - j_idx[None, :]
    band = (steps >= 0) & (steps <= window // dil)
    first = (jnp.arange(n_blk)[:, None, None] == 0) & (j_idx[None, None, :] < Q_BLOCK)
    valid = band[None] & ~first
    bias = rel_table[t5_bucket(jnp.maximum(steps, 0) * dil)]
    bias = jnp.transpose(bias, (2, 0, 1)).astype(jnp.float32)
    logits = jnp.einsum('bnirhe,bnjrhe->bnrhij', qb, kw).astype(jnp.float32) + bias
    logits = jnp.where(valid[None, :, None, None], logits, NEG_INF)
    m = jnp.max(logits, axis=-1, keepdims=True)
    p = jnp.exp(logits - m)
    denom = jnp.sum(p, axis=-1, keepdims=True)
    o = jnp.einsum('bnrhij,bnjrhe->bnirhe', (p / denom).astype(v.dtype), vw)
    lse = (m + jnp.log(denom))[..., 0]
    o = o.reshape(b, s_pad, h, e)[:, :s]
    lse = jnp.transpose(lse, (0, 1, 4, 2, 3)).reshape(b, s_pad, h)[:, :s]
    return o, lse


def dilated_attention(q, k, v, rel_table):
    outs = []
    lses = []
    for window, dil in DILATED_BRANCHES:
        o, lse = dilated_branch(q, k, v, rel_table, window, dil)
        outs.append(o)
        lses.append(lse)
    w = jax.nn.softmax(jnp.stack(lses, axis=-1), axis=-1)
    o = sum(outs[i].astype(jnp.float32) * w[..., i, None] for i in range(len(outs)))
    return o.astype(q.dtype)


def lru_combine(left, right):
    a_l, b_l = left
    a_r, b_r = right
    return a_l * a_r, a_r * b_l + b_r


def rglru_branch(xr, gr, conv_w, conv_b, w_a, b_a, w_x, b_x, lam):
    b, s, c = xr.shape
    u = lax.conv_general_dilated(xr, conv_w[:, None, :], window_strides=(1,),
                                 padding=((CONV_WIDTH - 1, 0),),
                                 dimension_numbers=('NWC', 'WIO', 'NWC'),
                                 feature_group_count=c) + conv_b
    ub = u.reshape(b, s, RNN_BLOCKS, RNN_BLOCK_W)
    r = jax.nn.sigmoid(jnp.einsum('bsgi,gij->bsgj', ub, w_a) + b_a).reshape(b, s, c)
    i = jax.nn.sigmoid(jnp.einsum('bsgi,gij->bsgj', ub, w_x) + b_x).reshape(b, s, c)
    log_a = -RG_C * r.astype(jnp.float32) * jax.nn.softplus(-lam.astype(jnp.float32))
    a = jnp.exp(log_a)
    drive = jnp.sqrt(-jnp.expm1(2.0 * log_a)) * (i * u).astype(jnp.float32)
    _, hseq = lax.associative_scan(lru_combine, (a, drive), axis=1)
    return hseq.astype(xr.dtype) * jax.nn.gelu(gr)


def expert_mlp(xb, wg, wu, wd):
    return (jax.nn.silu(xb @ wg) * (xb @ wu)) @ wd


def hier_moe(h, wr_g, br_g, wr_e, br_e, w_gate, w_up, w_down):
    b, s, d = h.shape
    m = b * s
    hf = h.reshape(m, d)
    g_logits = (hf @ wr_g + br_g).astype(jnp.float32)
    _, g_sel = lax.top_k(g_logits, 1)
    g_w = jnp.take_along_axis(jax.nn.softmax(g_logits, axis=-1), g_sel, axis=-1)
    e_logits = (hf @ wr_e + br_e).astype(jnp.float32).reshape(m, N_GROUPS, EXPERTS_PER_GROUP)
    e_logits = jnp.take_along_axis(e_logits, g_sel[:, :, None], axis=1)[:, 0]
    e_top, e_idx = lax.top_k(e_logits, TOP_K)
    gate = (g_w * jax.nn.softmax(e_top, axis=-1)).reshape(-1)
    expert = (g_sel * EXPERTS_PER_GROUP + e_idx).reshape(-1)
    token = jnp.repeat(jnp.arange(m), TOP_K)
    n_assign = m * TOP_K
    order = jnp.argsort(expert)
    e_sorted = expert[order]
    tok_sorted = token[order]
    counts = jnp.zeros((N_EXPERTS,), jnp.int32).at[expert].add(1)
    padded = (counts + MOE_BLOCK - 1) // MOE_BLOCK * MOE_BLOCK
    starts = jnp.cumsum(counts) - counts
    p_ends = jnp.cumsum(padded)
    p_starts = p_ends - padded
    dest = p_starts[e_sorted] + jnp.arange(n_assign) - starts[e_sorted]
    n_rows = n_assign + N_EXPERTS * MOE_BLOCK
    n_blocks = n_rows // MOE_BLOCK
    rows = jnp.zeros((n_rows, d), h.dtype).at[dest].set(hf[tok_sorted])
    block_expert = jnp.minimum(
        jnp.searchsorted(p_ends, jnp.arange(n_blocks) * MOE_BLOCK, side='right'), N_EXPERTS - 1)

    def run_block(args):
        xb, e = args
        return expert_mlp(xb, w_gate[e], w_up[e], w_down[e])

    out_rows = lax.map(run_block, (rows.reshape(n_blocks, MOE_BLOCK, d), block_expert)).reshape(n_rows, d)
    contrib = (out_rows[dest].astype(jnp.float32) * gate[order][:, None]).astype(h.dtype)
    return jnp.zeros((m, d), h.dtype).at[tok_sorted].add(contrib).reshape(b, s, d)


def setup_inputs(seed: int = 0) -> dict:
    key = jax.random.key(seed)
    ks = jax.random.split(key, 24)
    f32 = jnp.float32

    def nrm(k, shape, scale):
        return jax.random.normal(k, shape, f32) * scale

    def gain(k, shape):
        return 1.0 + 0.01 * jax.random.normal(k, shape, f32)

    a0 = jax.random.uniform(ks[12], (DEPTH, D_RNN), f32, minval=0.9, maxval=0.999)
    return {
        'x': nrm(ks[0], (BATCH, SEQ, D_MODEL), 1.0),
        'rel_bias_table': nrm(ks[1], (REL_BUCKETS, ATTN_HEADS), 0.2),
        'norm_mix': gain(ks[2], (DEPTH, D_MODEL)),
        'w_in': nrm(ks[3], (DEPTH, D_MODEL, D_IN), D_MODEL ** -0.5),
        'q_norm': gain(ks[4], (DEPTH, HEAD_DIM)),
        'k_norm': gain(ks[5], (DEPTH, HEAD_DIM)),
        'conv_w': nrm(ks[6], (DEPTH, CONV_WIDTH, D_RNN), CONV_WIDTH ** -0.5),
        'conv_b': nrm(ks[7], (DEPTH, D_RNN), 0.01),
        'rg_w_a': nrm(ks[8], (DEPTH, RNN_BLOCKS, RNN_BLOCK_W, RNN_BLOCK_W), RNN_BLOCK_W ** -0.5),
        'rg_b_a': nrm(ks[9], (DEPTH, RNN_BLOCKS, RNN_BLOCK_W), 0.01),
        'rg_w_x': nrm(ks[10], (DEPTH, RNN_BLOCKS, RNN_BLOCK_W, RNN_BLOCK_W), RNN_BLOCK_W ** -0.5),
        'rg_b_x': nrm(ks[11], (DEPTH, RNN_BLOCKS, RNN_BLOCK_W), 0.01),
        'rg_lambda': jnp.log(a0) - jnp.log1p(-a0),
        'norm_attn_out': gain(ks[13], (DEPTH, D_ATTN)),
        'norm_rnn_out': gain(ks[14], (DEPTH, D_RNN)),
        'w_out': nrm(ks[15], (DEPTH, D_MIX, D_MODEL), D_MIX ** -0.5),
        'norm_ffn': gain(ks[16], (DEPTH, D_MODEL)),
        'router_group_w': nrm(ks[17], (DEPTH, D_MODEL, N_GROUPS), D_MODEL ** -0.5),
        'router_group_b': nrm(ks[18], (DEPTH, N_GROUPS), 0.01),
        'router_expert_w': nrm(ks[19], (DEPTH, D_MODEL, N_EXPERTS), D_MODEL ** -0.5),
        'router_expert_b': nrm(ks[20], (DEPTH, N_EXPERTS), 0.01),
        'expert_w_gate': nrm(ks[21], (DEPTH, N_EXPERTS, D_MODEL, D_EXPERT), D_MODEL ** -0.5),
        'expert_w_up': nrm(ks[22], (DEPTH, N_EXPERTS, D_MODEL, D_EXPERT), D_MODEL ** -0.5),
        'expert_w_down': nrm(ks[23], (DEPTH, N_EXPERTS, D_EXPERT, D_MODEL), D_EXPERT ** -0.5),
    }


def reference(x, rel_bias_table, norm_mix, w_in, q_norm, k_norm, conv_w, conv_b,
              rg_w_a, rg_b_a, rg_w_x, rg_b_x, rg_lambda, norm_attn_out, norm_rnn_out,
              w_out, norm_ffn, router_group_w, router_group_b, router_expert_w,
              router_expert_b, expert_w_gate, expert_w_up, expert_w_down):
    b, s, _ = x.shape
    scale = HEAD_DIM ** -0.5
    splits = [D_ATTN, 2 * D_ATTN, 3 * D_ATTN, 3 * D_ATTN + D_RNN]
    for l in range(DEPTH):
        hn = rms_norm(x, norm_mix[l])
        z = hn @ w_in[l]
        q, k, v, xr, gr = jnp.split(z, splits, axis=-1)
        q = rms_norm(q.reshape(b, s, ATTN_HEADS, HEAD_DIM), q_norm[l]) * scale
        k = rms_norm(k.reshape(b, s, ATTN_HEADS, HEAD_DIM), k_norm[l])
        v = v.reshape(b, s, ATTN_HEADS, HEAD_DIM)
        attn = dilated_attention(q, k, v, rel_bias_table).reshape(b, s, D_ATTN)
        rnn = rglru_branch(xr, gr, conv_w[l], conv_b[l], rg_w_a[l], rg_b_a[l],
                           rg_w_x[l], rg_b_x[l], rg_lambda[l])
        mix = jnp.concatenate([rms_norm(attn, norm_attn_out[l]), rms_norm(rnn, norm_rnn_out[l])], axis=-1)
        x = x + mix @ w_out[l]
        x = x + hier_moe(rms_norm(x, norm_ffn[l]), router_group_w[l], router_group_b[l],
                         router_expert_w[l], router_expert_b[l], expert_w_gate[l],
                         expert_w_up[l], expert_w_down[l])
    return x
```

```python
import functools
import math

import numpy as np
import jax
import jax.numpy as jnp
from jax import lax
from jax.experimental import pallas as pl
from jax.experimental.pallas import tpu as pltpu

F32 = jnp.float32
BF16 = jnp.bfloat16
I32 = jnp.int32

D_MODEL = 1024
ATTN_HEADS = 8
HEAD_DIM = 64
D_ATTN = ATTN_HEADS * HEAD_DIM
RNN_BLOCKS = 8
D_RNN = 512
D_MIX = D_ATTN + D_RNN
D_IN = 3 * D_ATTN + 2 * D_RNN
DILATED_BRANCHES = ((128, 1), (512, 4), (2048, 16))
Q_BLOCK = 128
REL_BUCKETS = 32
REL_MAX_DIST = 2048
CONV_WIDTH = 4
RG_C = 8.0
N_GROUPS = 4
EXPERTS_PER_GROUP = 8
N_EXPERTS = N_GROUPS * EXPERTS_PER_GROUP
TOP_K = 2
D_EXPERT = 512
EPS = 1e-6
NEG_INF = -1e30

LANES = 128
ROUTER_LANE0 = N_GROUPS
TOKEN_TILE = 512
ATTN_TILE = 512
RNN_TILE = 512
RNN_CHUNK = 64
EXPERT_BLOCK = 256
VMEM_LIMIT = 48 * 1024 * 1024


def _cparams(*sem):
    return pltpu.CompilerParams(dimension_semantics=sem, vmem_limit_bytes=VMEM_LIMIT)


def _rms(x, gain):
    return x * lax.rsqrt(jnp.mean(x * x, axis=-1, keepdims=True) + EPS) * gain


def _proj_kernel(x_ref, g_ref, w_ref, qg_ref, kg_ref, gm_ref, q_ref, k_ref, v_ref, xr_ref, gr_ref):
    hb = _rms(x_ref[...], g_ref[...]).astype(BF16)

    def sec(n):
        return jnp.dot(hb, w_ref[:, n * D_ATTN:(n + 1) * D_ATTN], preferred_element_type=F32)

    def head_norm(z, gain):
        ms = jnp.dot((z * z).astype(BF16), gm_ref[...], preferred_element_type=F32)
        return z * lax.rsqrt(ms + EPS) * gain

    q_ref[...] = head_norm(sec(0), qg_ref[...]).astype(BF16)
    k_ref[...] = head_norm(sec(1), kg_ref[...]).astype(BF16)
    v_ref[...] = sec(2).astype(BF16)
    xr_ref[...] = sec(3)
    gr_ref[...] = sec(4)


def _proj(x2, gain, w_in_bf, qg, kg, gm):
    m = x2.shape[0]
    tm = TOKEN_TILE
    row = lambda i: (i, 0)
    fix = lambda i: (0, 0)
    return pl.pallas_call(
        _proj_kernel,
        grid=(m // tm,),
        in_specs=[
            pl.BlockSpec((tm, D_MODEL), row),
            pl.BlockSpec((1, D_MODEL), fix),
            pl.BlockSpec((D_MODEL, D_IN), fix),
            pl.BlockSpec((1, D_ATTN), fix),
            pl.BlockSpec((1, D_ATTN), fix),
            pl.BlockSpec((D_ATTN, D_ATTN), fix),
        ],
        out_specs=[pl.BlockSpec((tm, D_ATTN), row)] * 5,
        out_shape=[jax.ShapeDtypeStruct((m, D_ATTN), BF16)] * 3 + [jax.ShapeDtypeStruct((m, D_RNN), F32)] * 2,
        compiler_params=_cparams("parallel"),
    )(x2, gain, w_in_bf, qg, kg, gm)


def _attn_kernel(q_ref, kc_ref, kp_ref, vc_ref, vp_ref, bias_ref, o_ref, lse_ref, kbuf, vbuf, *, tq):
    i = pl.program_id(2)
    kbuf[0:Q_BLOCK, :] = kp_ref[...]
    kbuf[Q_BLOCK:, :] = kc_ref[...]
    vbuf[0:Q_BLOCK, :] = vp_ref[...]
    vbuf[Q_BLOCK:, :] = vc_ref[...]
    lane = lax.broadcasted_iota(I32, (Q_BLOCK, LANES), 1)
    low = lane < HEAD_DIM

    def block(j, carry):
        r0 = pl.multiple_of(j * Q_BLOCK, Q_BLOCK)
        sel = jnp.where(jnp.logical_and(i == 0, j == 0), 1, 0)
        lse_t = jnp.zeros((Q_BLOCK, LANES), F32)
        for hp in range(ATTN_HEADS // 2):
            ls = slice(hp * LANES, (hp + 1) * LANES)
            q = q_ref[pl.ds(r0, Q_BLOCK), ls]
            kk = kbuf[pl.ds(r0, 2 * Q_BLOCK), ls]
            vv = vbuf[pl.ds(r0, 2 * Q_BLOCK), ls]
            outs = []
            for hh in range(2):
                h = 2 * hp + hh
                qm = jnp.where(low if hh == 0 else jnp.logical_not(low), q, jnp.zeros_like(q))
                s = lax.dot_general(qm, kk, (((1,), (1,)), ((), ())), preferred_element_type=F32)
                s = s + bias_ref[sel, h]
                mx = jnp.max(s, axis=-1, keepdims=True)
                p = jnp.exp(s - mx)
                l = jnp.sum(p, axis=-1, keepdims=True)
                pv = jnp.dot(p.astype(BF16), vv, preferred_element_type=F32)
                outs.append(pv * (1.0 / l))
                lse_t = jnp.where(lane == h, mx + jnp.log(l), lse_t)
            o_ref[pl.ds(r0, Q_BLOCK), ls] = jnp.where(low, outs[0], outs[1])
        lse_ref[pl.ds(r0, Q_BLOCK), :] = lse_t
        return carry

    lax.fori_loop(0, tq // Q_BLOCK, block, 0)


def _attention_branch(q, k, v, bias, b, s, dil):
    sub = s // dil
    tq = min(ATTN_TILE, sub)
    view = lambda t: t.reshape(b, sub, dil * D_ATTN)
    qv, kv, vv = view(q), view(k), view(v)
    per = tq // Q_BLOCK
    cur = lambda bb, r, i: (bb, i, r)
    prev = lambda bb, r, i: (bb, jnp.maximum(i * per - 1, 0), r)
    o, lse = pl.pallas_call(
        functools.partial(_attn_kernel, tq=tq),
        grid=(b, dil, sub // tq),
        in_specs=[
            pl.BlockSpec((None, tq, D_ATTN), cur),
            pl.BlockSpec((None, tq, D_ATTN), cur),
            pl.BlockSpec((None, Q_BLOCK, D_ATTN), prev),
            pl.BlockSpec((None, tq, D_ATTN), cur),
            pl.BlockSpec((None, Q_BLOCK, D_ATTN), prev),
            pl.BlockSpec((2, ATTN_HEADS, Q_BLOCK, 2 * Q_BLOCK), lambda bb, r, i: (0, 0, 0, 0)),
        ],
        out_specs=[
            pl.BlockSpec((None, tq, D_ATTN), cur),
            pl.BlockSpec((None, tq, LANES), cur),
        ],
        out_shape=[
            jax.ShapeDtypeStruct((b, sub, dil * D_ATTN), F32),
            jax.ShapeDtypeStruct((b, sub, dil * LANES), F32),
        ],
        scratch_shapes=[pltpu.VMEM((tq + Q_BLOCK, D_ATTN), BF16), pltpu.VMEM((tq + Q_BLOCK, D_ATTN), BF16)],
        compiler_params=_cparams("parallel", "parallel", "parallel"),
    )(qv, kv, kv, vv, vv, bias)
    return o.reshape(b * s, D_ATTN), lse.reshape(b * s, LANES)


def _bias_index_tables():
    exact = REL_BUCKETS // 2
    i = np.arange(Q_BLOCK)[:, None]
    j = np.arange(2 * Q_BLOCK)[None, :]
    steps = i + Q_BLOCK - j
    idx, band = [], []
    for window, dil in DILATED_BRANCHES:
        dist = (np.maximum(steps, 0) * dil).astype(np.int32)
        d = np.maximum(dist, 1).astype(np.float32)
        log_b = exact + (np.log(d / np.float32(exact)) / np.float32(math.log(REL_MAX_DIST / exact))
                         * np.float32(REL_BUCKETS - exact)).astype(np.int32)
        idx.append(np.where(dist < exact, dist, np.minimum(log_b, REL_BUCKETS - 1)).astype(np.int32))
        band.append((steps >= 0) & (steps <= window // dil))
    first = np.broadcast_to(j >= Q_BLOCK, (Q_BLOCK, 2 * Q_BLOCK))
    return np.stack(idx), np.stack(band), first


def _bias_tables(rel_table):
    idx, band, first = _bias_index_tables()
    bias = jnp.transpose(rel_table.astype(F32)[idx], (0, 3, 1, 2))
    regular = jnp.where(band[:, None], bias, NEG_INF)
    start = jnp.where((band & first[None])[:, None], bias, NEG_INF)
    return jnp.stack([regular, start], axis=1)


def _rglru_kernel(xr_ref, gr_ref, cw_ref, cb_ref, wa_ref, ba_ref, wx_ref, bx_ref, lam_ref, out_ref,
                  xbuf, a_scr, b_scr, h_scr, *, tt):
    t = pl.program_id(1)

    @pl.when(t == 0)
    def _():
        xbuf[0:8, :] = jnp.zeros((8, D_RNN), F32)
        h_scr[...] = jnp.zeros_like(h_scr)

    xbuf[8:8 + tt, :] = xr_ref[...]
    u = cb_ref[...] + cw_ref[3:4, :] * xbuf[8:8 + tt, :]
    for back in range(1, CONV_WIDTH):
        u = u + cw_ref[3 - back:4 - back, :] * xbuf[8 - back:8 - back + tt, :]
    xbuf[0:8, :] = xbuf[tt:tt + 8, :]

    ub = u.astype(BF16)
    r = jax.nn.sigmoid(jnp.dot(ub, wa_ref[...], preferred_element_type=F32) + ba_ref[...])
    gi = jax.nn.sigmoid(jnp.dot(ub, wx_ref[...], preferred_element_type=F32) + bx_ref[...])
    nl = -lam_ref[...]
    softplus = jnp.maximum(nl, 0.0) + jnp.log1p(jnp.exp(-jnp.abs(nl)))
    log_a = (-RG_C) * r * softplus
    a = jnp.exp(log_a)
    a_scr[...] = a
    b_scr[...] = jnp.sqrt(-jnp.tanh(log_a) * (a * a + 1.0)) * (gi * u)

    rc = RNN_CHUNK
    row = lax.broadcasted_iota(I32, (rc, LANES), 0)

    def chunk(c, carry):
        r0 = pl.multiple_of(c * rc, rc)
        for g in range(D_RNN // LANES):
            ls = slice(g * LANES, (g + 1) * LANES)
            aa = a_scr[pl.ds(r0, rc), ls]
            bb = b_scr[pl.ds(r0, rc), ls]
            k = 1
            while k < rc:
                keep = row >= k
                a_sh = pltpu.roll(aa, k, 0)
                b_sh = pltpu.roll(bb, k, 0)
                bb = jnp.where(keep, aa * b_sh + bb, bb)
                aa = jnp.where(keep, aa * a_sh, aa)
                k *= 2
            h = aa * h_scr[0:1, ls] + bb
            h_scr[0:1, ls] = h[rc - 1:rc, :]
            out_ref[pl.ds(r0, rc), ls] = h * jax.nn.gelu(gr_ref[pl.ds(r0, rc), ls], approximate=True)
        return carry

    lax.fori_loop(0, tt // rc, chunk, 0)


def _rglru(xr, gr, cw, cb, wa, ba, wx, bx, lam, b, s):
    tt = RNN_TILE
    nt = s // tt
    row = lambda bb, t: (bb * nt + t, 0)
    fix = lambda bb, t: (0, 0)
    vec = pl.BlockSpec((1, D_RNN), fix)
    mat = pl.BlockSpec((D_RNN, D_RNN), fix)
    return pl.pallas_call(
        functools.partial(_rglru_kernel, tt=tt),
        grid=(b, nt),
        in_specs=[pl.BlockSpec((tt, D_RNN), row), pl.BlockSpec((tt, D_RNN), row),
                  pl.BlockSpec((CONV_WIDTH, D_RNN), fix), vec, mat, vec, mat, vec, vec],
        out_specs=pl.BlockSpec((tt, D_RNN), row),
        out_shape=jax.ShapeDtypeStruct((b * s, D_RNN), F32),
        scratch_shapes=[pltpu.VMEM((tt + 8, D_RNN), F32), pltpu.VMEM((tt, D_RNN), F32),
                        pltpu.VMEM((tt, D_RNN), F32), pltpu.VMEM((8, D_RNN), F32)],
        compiler_params=_cparams("parallel", "arbitrary"),
    )(xr, gr, cw, cb, wa, ba, wx, bx, lam)


def _block_diag(w):
    nb, n, _ = w.shape
    eye = jnp.eye(nb, dtype=w.dtype)
    return (eye[:, None, :, None] * w[:, :, None, :]).reshape(nb * n, nb * n)


def _split_bf16(x):
    hi = x.astype(BF16)
    return hi, (x - hi.astype(F32)).astype(BF16)


def _mix_kernel(x_ref, o1_ref, o2_ref, o3_ref, l1_ref, l2_ref, l3_ref, rnn_ref, ga_ref, gn_ref, wo_ref, gf_ref,
                wrh_ref, wrl_ref, br_ref, ex_ref, tri_ref,
                x1_ref, h2_ref, route_ref, gate_ref, cnt_ref, carry):
    step = pl.program_id(0)

    @pl.when(step == 0)
    def _():
        carry[...] = jnp.zeros_like(carry)

    l1, l2, l3 = l1_ref[...], l2_ref[...], l3_ref[...]
    mx = jnp.maximum(jnp.maximum(l1, l2), l3)
    e1, e2, e3 = jnp.exp(l1 - mx), jnp.exp(l2 - mx), jnp.exp(l3 - mx)
    inv = 1.0 / (e1 + e2 + e3)

    def widen(w):
        hi, lo = _split_bf16(w)
        return (jnp.dot(hi, ex_ref[...], preferred_element_type=F32)
                + jnp.dot(lo, ex_ref[...], preferred_element_type=F32))

    attn = widen(e1 * inv) * o1_ref[...] + widen(e2 * inv) * o2_ref[...] + widen(e3 * inv) * o3_ref[...]
    na = _rms(attn, ga_ref[...]).astype(BF16)
    nr = _rms(rnn_ref[...], gn_ref[...]).astype(BF16)
    x1 = (x_ref[...] + jnp.dot(na, wo_ref[0:D_ATTN, :], preferred_element_type=F32)
          + jnp.dot(nr, wo_ref[D_ATTN:, :], preferred_element_type=F32))
    x1_ref[...] = x1
    h2 = _rms(x1, gf_ref[...])
    h2_ref[...] = h2

    hh, hl = _split_bf16(h2)
    logits = (jnp.dot(hh, wrh_ref[...], preferred_element_type=F32)
              + jnp.dot(hl, wrh_ref[...], preferred_element_type=F32)
              + jnp.dot(hh, wrl_ref[...], preferred_element_type=F32)) + br_ref[...]
    tm = logits.shape[0]
    lane = lax.broadcasted_iota(I32, (tm, LANES), 1)
    lanef = lane.astype(F32)
    big = float(LANES)

    def top(vals):
        m = jnp.max(vals, axis=-1, keepdims=True)
        return m, jnp.min(jnp.where(vals == m, lanef, big), axis=-1, keepdims=True)

    is_group = lane < N_GROUPS
    gmax, gsel = top(jnp.where(is_group, logits, NEG_INF))
    g_w = 1.0 / jnp.sum(jnp.where(is_group, jnp.exp(logits - gmax), 0.0), axis=-1, keepdims=True)
    lo_lane = ROUTER_LANE0 + EXPERTS_PER_GROUP * gsel
    in_group = jnp.logical_and(lanef >= lo_lane, lanef < lo_lane + EXPERTS_PER_GROUP)
    el = jnp.where(in_group, logits, NEG_INF)
    v1, i1 = top(el)
    v2, i2 = top(jnp.where(lanef == i1, NEG_INF, el))
    t = jnp.exp(v2 - v1)
    p1 = 1.0 / (1.0 + t)
    gate1 = g_w * p1
    gate2 = g_w * (t * p1)

    oh1 = (lanef == i1).astype(F32)
    oh2 = (lanef == i2).astype(F32)
    cnt = oh1 + oh2
    base = carry[0:1, :] + jnp.dot(tri_ref[...], cnt.astype(BF16), preferred_element_type=F32)
    rank1 = jnp.sum(oh1 * base, axis=-1, keepdims=True)
    rank2 = jnp.sum(oh2 * base, axis=-1, keepdims=True)
    total = carry[0:1, :] + jnp.sum(cnt, axis=0, keepdims=True)
    carry[...] = jnp.broadcast_to(total, carry.shape)
    cnt_ref[...] = jnp.broadcast_to(total, cnt_ref.shape).astype(I32)

    route = jnp.where(lane == 0, i1 - ROUTER_LANE0,
                      jnp.where(lane == 1, i2 - ROUTER_LANE0,
                                jnp.where(lane == 2, rank1, jnp.where(lane == 3, rank2, 0.0))))
    route_ref[...] = route.astype(I32)
    gate_ref[...] = jnp.where(lane == 0, gate1, jnp.where(lane == 1, gate2, 0.0))


def _mix(x2, o, lse, rnn, ga, gn, wo_bf, gf, wr_hi, wr_lo, br, ex, tri):
    m = x2.shape[0]
    tm = TOKEN_TILE
    row = lambda i: (i, 0)
    fix = lambda i: (0, 0)
    t512 = pl.BlockSpec((tm, D_ATTN), row)
    t128 = pl.BlockSpec((tm, LANES), row)
    t1024 = pl.BlockSpec((tm, D_MODEL), row)
    return pl.pallas_call(
        _mix_kernel,
        grid=(m // tm,),
        in_specs=[t1024, t512, t512, t512, t128, t128, t128, t512,
                  pl.BlockSpec((1, D_ATTN), fix), pl.BlockSpec((1, D_RNN), fix),
                  pl.BlockSpec((D_MIX, D_MODEL), fix), pl.BlockSpec((1, D_MODEL), fix),
                  pl.BlockSpec((D_MODEL, LANES), fix), pl.BlockSpec((D_MODEL, LANES), fix),
                  pl.BlockSpec((1, LANES), fix), pl.BlockSpec((LANES, D_ATTN), fix),
                  pl.BlockSpec((tm, tm), fix)],
        out_specs=[t1024, t1024, t128, t128, pl.BlockSpec((8, LANES), fix)],
        out_shape=[jax.ShapeDtypeStruct((m, D_MODEL), F32), jax.ShapeDtypeStruct((m, D_MODEL), F32),
                   jax.ShapeDtypeStruct((m, LANES), I32), jax.ShapeDtypeStruct((m, LANES), F32),
                   jax.ShapeDtypeStruct((8, LANES), I32)],
        scratch_shapes=[pltpu.VMEM((8, LANES), F32)],
        compiler_params=_cparams("arbitrary"),
    )(x2, o[0], o[1], o[2], lse[0], lse[1], lse[2], rnn, ga, gn, wo_bf, gf, wr_hi, wr_lo, br, ex, tri)


def _row_copy(src, src_row, dst, dst_row, sem):
    return pltpu.make_async_copy(src.at[pl.ds(src_row, 1)], dst.at[pl.ds(dst_row, 1)], sem)


def _dispatch_kernel(pstart_ref, ids_ref, rank_ref, h_ref, rows_in_ref, rows_ref, sem, *, tm):
    del rows_in_ref

    def issue(j, carry):
        for slot in range(TOP_K):
            a = TOP_K * j + slot
            dest = pstart_ref[ids_ref[a]] + rank_ref[a]
            _row_copy(h_ref, j, rows_ref, dest, sem).start()
        return carry

    lax.fori_loop(0, tm, issue, 0)

    def drain(j, carry):
        _row_copy(h_ref, 0, rows_ref, 0, sem).wait()
        return carry

    lax.fori_loop(0, TOP_K * tm, drain, 0)


def _dispatch(pstart, ids, ranks, h2, rows0):
    m = h2.shape[0]
    tm = TOKEN_TILE
    flat = pl.BlockSpec((TOP_K * tm,), lambda i, ps: (i,), memory_space=pltpu.SMEM)
    return pl.pallas_call(
        functools.partial(_dispatch_kernel, tm=tm),
        grid_spec=pltpu.PrefetchScalarGridSpec(
            num_scalar_prefetch=1,
            grid=(m // tm,),
            in_specs=[flat, flat, pl.BlockSpec((tm, D_MODEL), lambda i, ps: (i, 0)),
                      pl.BlockSpec(memory_space=pl.ANY)],
            out_specs=pl.BlockSpec(memory_space=pl.ANY),
            scratch_shapes=[pltpu.SemaphoreType.DMA(())],
        ),
        out_shape=jax.ShapeDtypeStruct(rows0.shape, rows0.dtype),
        input_output_aliases={4: 0},
        compiler_params=_cparams("arbitrary"),
    )(pstart, ids, ranks, h2, rows0)


def _expert_kernel(be_ref, nused_ref, rows_ref, wg_ref, wu_ref, wd_ref, out_ref, wgb, wub, wdb):
    i = pl.program_id(0)
    changed = jnp.logical_or(i == 0, be_ref[i] != be_ref[jnp.maximum(i - 1, 0)])

    @pl.when(changed)
    def _():
        wgb[...] = wg_ref[...].astype(BF16)
        wub[...] = wu_ref[...].astype(BF16)
        wdb[...] = wd_ref[...].astype(BF16)

    @pl.when(i < nused_ref[0])
    def _():
        xb = rows_ref[...].astype(BF16)
        g = jnp.dot(xb, wgb[...], preferred_element_type=F32)
        u = jnp.dot(xb, wub[...], preferred_element_type=F32)
        act = (g * jax.nn.sigmoid(g) * u).astype(BF16)
        out_ref[...] = jnp.dot(act, wdb[...], preferred_element_type=F32)

    @pl.when(i >= nused_ref[0])
    def _():
        out_ref[...] = jnp.zeros_like(out_ref)


def _experts(block_expert, n_used, rows, w_gate, w_up, w_down, layer):
    n_rows = rows.shape[0]
    blk = EXPERT_BLOCK
    wmap = lambda i, be, nu: (layer, be[i], 0, 0)
    return pl.pallas_call(
        _expert_kernel,
        grid_spec=pltpu.PrefetchScalarGridSpec(
            num_scalar_prefetch=2,
            grid=(n_rows // blk,),
            in_specs=[pl.BlockSpec((blk, D_MODEL), lambda i, be, nu: (i, 0)),
                      pl.BlockSpec((None, None, D_MODEL, D_EXPERT), wmap),
                      pl.BlockSpec((None, None, D_MODEL, D_EXPERT), wmap),
                      pl.BlockSpec((None, None, D_EXPERT, D_MODEL), wmap)],
            out_specs=pl.BlockSpec((blk, D_MODEL), lambda i, be, nu: (i, 0)),
            scratch_shapes=[pltpu.VMEM((D_MODEL, D_EXPERT), BF16), pltpu.VMEM((D_MODEL, D_EXPERT), BF16),
                            pltpu.VMEM((D_EXPERT, D_MODEL), BF16)],
        ),
        out_shape=jax.ShapeDtypeStruct((n_rows, D_MODEL), F32),
        compiler_params=_cparams("arbitrary"),
    )(block_expert, n_used, rows, w_gate, w_up, w_down)


def _combine_kernel(pstart_ref, ids_ref, rank_ref, x1_ref, gate_ref, rows_ref, out_ref, buf0, buf1, sem, *, tm):
    bufs = (buf0, buf1)

    def issue(j, carry):
        for slot in range(TOP_K):
            a = TOP_K * j + slot
            src = pstart_ref[ids_ref[a]] + rank_ref[a]
            _row_copy(rows_ref, src, bufs[slot], j, sem).start()
        return carry

    lax.fori_loop(0, tm, issue, 0)

    def drain(j, carry):
        _row_copy(rows_ref, 0, buf0, 0, sem).wait()
        return carry

    lax.fori_loop(0, TOP_K * tm, drain, 0)
    gate = gate_ref[...]
    out_ref[...] = x1_ref[...] + (gate[:, 0:1] * buf0[...] + gate[:, 1:2] * buf1[...])


def _combine(pstart, ids, ranks, x1, gates, out_rows):
    m = x1.shape[0]
    tm = TOKEN_TILE
    flat = pl.BlockSpec((TOP_K * tm,), lambda i, ps: (i,), memory_space=pltpu.SMEM)
    return pl.pallas_call(
        functools.partial(_combine_kernel, tm=tm),
        grid_spec=pltpu.PrefetchScalarGridSpec(
            num_scalar_prefetch=1,
            grid=(m // tm,),
            in_specs=[flat, flat, pl.BlockSpec((tm, D_MODEL), lambda i, ps: (i, 0)),
                      pl.BlockSpec((tm, LANES), lambda i, ps: (i, 0)),
                      pl.BlockSpec(memory_space=pl.ANY)],
            out_specs=pl.BlockSpec((tm, D_MODEL), lambda i, ps: (i, 0)),
            scratch_shapes=[pltpu.VMEM((tm, D_MODEL), F32), pltpu.VMEM((tm, D_MODEL), F32),
                            pltpu.SemaphoreType.DMA(())],
        ),
        out_shape=jax.ShapeDtypeStruct((m, D_MODEL), F32),
        compiler_params=_cparams("arbitrary"),
    )(pstart, ids, ranks, x1, gates, out_rows)


def _constants():
    gm = np.kron(np.eye(ATTN_HEADS), np.full((HEAD_DIM, HEAD_DIM), 1.0 / HEAD_DIM))
    ex = np.zeros((LANES, D_ATTN), np.float32)
    for h in range(ATTN_HEADS):
        ex[h, h * HEAD_DIM:(h + 1) * HEAD_DIM] = 1.0
    tri = np.tril(np.ones((TOKEN_TILE, TOKEN_TILE), np.float32), -1)
    return jnp.asarray(gm, BF16), jnp.asarray(ex, BF16), jnp.asarray(tri, BF16)


def kernel(x, rel_bias_table, norm_mix, w_in, q_norm, k_norm, conv_w, conv_b, rg_w_a, rg_b_a, rg_w_x, rg_b_x,
           rg_lambda, norm_attn_out, norm_rnn_out, w_out, norm_ffn, router_group_w, router_group_b,
           router_expert_w, router_expert_b, expert_w_gate, expert_w_up, expert_w_down):
    b, s, d = x.shape
    depth = w_in.shape[0]
    m = b * s
    span = DILATED_BRANCHES[-1][1] * Q_BLOCK
    assert d == D_MODEL and s % span == 0 and s % RNN_TILE == 0 and m % TOKEN_TILE == 0

    gm, ex, tri = _constants()
    bias = _bias_tables(rel_bias_table)
    scale = HEAD_DIM ** -0.5
    n_assign = m * TOP_K
    n_rows = n_assign + N_EXPERTS * EXPERT_BLOCK
    n_blocks = n_rows // EXPERT_BLOCK
    rows0 = jnp.zeros((n_rows, D_MODEL), F32)
    row1 = lambda v: v.reshape(1, -1).astype(F32)

    x2 = x.reshape(m, d).astype(F32)
    for l in range(depth):
        q, k, v, xr, gr = _proj(x2, row1(norm_mix[l]), w_in[l].astype(BF16),
                                row1(jnp.tile(q_norm[l], ATTN_HEADS) * scale),
                                row1(jnp.tile(k_norm[l], ATTN_HEADS)), gm)
        outs, lses = [], []
        for n, (_, dil) in enumerate(DILATED_BRANCHES):
            o, lse = _attention_branch(q, k, v, bias[n], b, s, dil)
            outs.append(o)
            lses.append(lse)
        rnn = _rglru(xr, gr, conv_w[l].astype(F32), row1(conv_b[l]),
                     _block_diag(rg_w_a[l]).astype(BF16), row1(rg_b_a[l]),
                     _block_diag(rg_w_x[l]).astype(BF16), row1(rg_b_x[l]), row1(rg_lambda[l]), b, s)

        wr = jnp.zeros((D_MODEL, LANES), F32)
        wr = wr.at[:, :N_GROUPS].set(router_group_w[l]).at[:, N_GROUPS:N_GROUPS + N_EXPERTS].set(router_expert_w[l])
        wr_hi, wr_lo = _split_bf16(wr)
        br = jnp.zeros((1, LANES), F32)
        br = br.at[0, :N_GROUPS].set(router_group_b[l]).at[0, N_GROUPS:N_GROUPS + N_EXPERTS].set(router_expert_b[l])
        x1, h2, route, gates, cnt = _mix(x2, outs, lses, rnn, row1(norm_attn_out[l]), row1(norm_rnn_out[l]),
                                         w_out[l].astype(BF16), row1(norm_ffn[l]), wr_hi, wr_lo, br, ex, tri)

        counts = cnt[0, ROUTER_LANE0:ROUTER_LANE0 + N_EXPERTS]
        padded = (counts + EXPERT_BLOCK - 1) // EXPERT_BLOCK * EXPERT_BLOCK
        p_ends = jnp.cumsum(padded)
        p_starts = (p_ends - padded).astype(I32)
        block_expert = jnp.minimum(
            jnp.searchsorted(p_ends, jnp.arange(n_blocks) * EXPERT_BLOCK, side='right'), N_EXPERTS - 1).astype(I32)
        n_used = (p_ends[-1:] // EXPERT_BLOCK).astype(I32)
        ids = route[:, 0:TOP_K].reshape(n_assign)
        ranks = route[:, TOP_K:2 * TOP_K].reshape(n_assign)

        rows = _dispatch(p_starts, ids, ranks, h2, rows0)
        out_rows = _experts(block_expert, n_used, rows, expert_w_gate, expert_w_up, expert_w_down, l)
        x2 = _combine(p_starts, ids, ranks, x1, gates, out_rows)
    return x2.reshape(b, s, d).astype(x.dtype)
```

```python
import functools
import math

import numpy as np
import jax
import jax.numpy as jnp
from jax import lax
from jax.experimental import pallas as pl
from jax.experimental.pallas import tpu as pltpu

F32 = jnp.float32
BF16 = jnp.bfloat16
I32 = jnp.int32

D_MODEL = 1024
ATTN_HEADS = 8
HEAD_DIM = 64
D_ATTN = ATTN_HEADS * HEAD_DIM
RNN_BLOCKS = 8
D_RNN = 512
D_MIX = D_ATTN + D_RNN
D_IN = 3 * D_ATTN + 2 * D_RNN
DILATED_BRANCHES = ((128, 1), (512, 4), (2048, 16))
Q_BLOCK = 128
REL_BUCKETS = 32
REL_MAX_DIST = 2048
CONV_WIDTH = 4
RG_C = 8.0
N_GROUPS = 4
EXPERTS_PER_GROUP = 8
N_EXPERTS = N_GROUPS * EXPERTS_PER_GROUP
TOP_K = 2
D_EXPERT = 512
EPS = 1e-6
NEG_INF = -1e30

LANES = 128
ROUTER_LANE0 = N_GROUPS
TOKEN_TILE = 512
ATTN_TILE = 512
RNN_TILE = 512
RNN_CHUNK = 64
EXPERT_BLOCK = 256
ROW_CHUNK = 8
TILE_ROWS = -(-(TOKEN_TILE * TOP_K + N_EXPERTS * (ROW_CHUNK - 1)) // 256) * 256
ROW_WIDTH = D_MODEL + LANES
VMEM_LIMIT = 48 * 1024 * 1024


def _cparams(*sem):
    return pltpu.CompilerParams(dimension_semantics=sem, vmem_limit_bytes=VMEM_LIMIT)


def _rms(x, gain):
    return x * lax.rsqrt(jnp.mean(x * x, axis=-1, keepdims=True) + EPS) * gain


def _proj_kernel(x_ref, g_ref, w_ref, qg_ref, kg_ref, gm_ref, q_ref, k_ref, v_ref, xr_ref, gr_ref):
    hb = _rms(x_ref[...], g_ref[...]).astype(BF16)

    def sec(n):
        return jnp.dot(hb, w_ref[:, n * D_ATTN:(n + 1) * D_ATTN], preferred_element_type=F32)

    def head_norm(z, gain):
        ms = jnp.dot((z * z).astype(BF16), gm_ref[...], preferred_element_type=F32)
        return z * lax.rsqrt(ms + EPS) * gain

    q_ref[...] = head_norm(sec(0), qg_ref[...]).astype(BF16)
    k_ref[...] = head_norm(sec(1), kg_ref[...]).astype(BF16)
    v_ref[...] = sec(2).astype(BF16)
    xr_ref[...] = sec(3)
    gr_ref[...] = sec(4)


def _proj(x2, gain, w_in_bf, qg, kg, gm):
    m = x2.shape[0]
    tm = TOKEN_TILE
    row = lambda i: (i, 0)
    fix = lambda i: (0, 0)
    return pl.pallas_call(
        _proj_kernel,
        grid=(m // tm,),
        in_specs=[
            pl.BlockSpec((tm, D_MODEL), row),
            pl.BlockSpec((1, D_MODEL), fix),
            pl.BlockSpec((D_MODEL, D_IN), fix),
            pl.BlockSpec((1, D_ATTN), fix),
            pl.BlockSpec((1, D_ATTN), fix),
            pl.BlockSpec((D_ATTN, D_ATTN), fix),
        ],
        out_specs=[pl.BlockSpec((tm, D_ATTN), row)] * 5,
        out_shape=[jax.ShapeDtypeStruct((m, D_ATTN), BF16)] * 3 + [jax.ShapeDtypeStruct((m, D_RNN), F32)] * 2,
        compiler_params=_cparams("parallel"),
    )(x2, gain, w_in_bf, qg, kg, gm)


def _attn_kernel(q_ref, kc_ref, kp_ref, vc_ref, vp_ref, bias_ref, o_ref, lse_ref, kbuf, vbuf, *, tq):
    i = pl.program_id(2)
    kbuf[0:Q_BLOCK, :] = kp_ref[...]
    kbuf[Q_BLOCK:, :] = kc_ref[...]
    vbuf[0:Q_BLOCK, :] = vp_ref[...]
    vbuf[Q_BLOCK:, :] = vc_ref[...]
    lane = lax.broadcasted_iota(I32, (Q_BLOCK, LANES), 1)
    low = lane < HEAD_DIM

    def block(j, carry):
        r0 = pl.multiple_of(j * Q_BLOCK, Q_BLOCK)
        sel = jnp.where(jnp.logical_and(i == 0, j == 0), 1, 0)
        lse_t = jnp.zeros((Q_BLOCK, LANES), F32)
        for hp in range(ATTN_HEADS // 2):
            ls = slice(hp * LANES, (hp + 1) * LANES)
            q = q_ref[pl.ds(r0, Q_BLOCK), ls]
            kk = kbuf[pl.ds(r0, 2 * Q_BLOCK), ls]
            vv = vbuf[pl.ds(r0, 2 * Q_BLOCK), ls]
            outs = []
            for hh in range(2):
                h = 2 * hp + hh
                qm = jnp.where(low if hh == 0 else jnp.logical_not(low), q, jnp.zeros_like(q))
                s = lax.dot_general(qm, kk, (((1,), (1,)), ((), ())), preferred_element_type=F32)
                s = s + bias_ref[sel, h]
                mx = jnp.max(s, axis=-1, keepdims=True)
                p = jnp.exp(s - mx)
                l = jnp.sum(p, axis=-1, keepdims=True)
                pv = jnp.dot(p.astype(BF16), vv, preferred_element_type=F32)
                outs.append(pv * (1.0 / l))
                lse_t = jnp.where(lane == h, mx + jnp.log(l), lse_t)
            o_ref[pl.ds(r0, Q_BLOCK), ls] = jnp.where(low, outs[0], outs[1])
        lse_ref[pl.ds(r0, Q_BLOCK), :] = lse_t
        return carry

    lax.fori_loop(0, tq // Q_BLOCK, block, 0)


def _attention_branch(q, k, v, bias, b, s, dil):
    sub = s // dil
    tq = min(ATTN_TILE, sub)
    view = lambda t: t.reshape(b, sub, dil * D_ATTN)
    qv, kv, vv = view(q), view(k), view(v)
    per = tq // Q_BLOCK
    cur = lambda bb, r, i: (bb, i, r)
    prev = lambda bb, r, i: (bb, jnp.maximum(i * per - 1, 0), r)
    o, lse = pl.pallas_call(
        functools.partial(_attn_kernel, tq=tq),
        grid=(b, dil, sub // tq),
        in_specs=[
            pl.BlockSpec((None, tq, D_ATTN), cur),
            pl.BlockSpec((None, tq, D_ATTN), cur),
            pl.BlockSpec((None, Q_BLOCK, D_ATTN), prev),
            pl.BlockSpec((None, tq, D_ATTN), cur),
            pl.BlockSpec((None, Q_BLOCK, D_ATTN), prev),
            pl.BlockSpec((2, ATTN_HEADS, Q_BLOCK, 2 * Q_BLOCK), lambda bb, r, i: (0, 0, 0, 0)),
        ],
        out_specs=[
            pl.BlockSpec((None, tq, D_ATTN), cur),
            pl.BlockSpec((None, tq, LANES), cur),
        ],
        out_shape=[
            jax.ShapeDtypeStruct((b, sub, dil * D_ATTN), F32),
            jax.ShapeDtypeStruct((b, sub, dil * LANES), F32),
        ],
        scratch_shapes=[pltpu.VMEM((tq + Q_BLOCK, D_ATTN), BF16), pltpu.VMEM((tq + Q_BLOCK, D_ATTN), BF16)],
        compiler_params=_cparams("parallel", "parallel", "parallel"),
    )(qv, kv, kv, vv, vv, bias)
    return o.reshape(b * s, D_ATTN), lse.reshape(b * s, LANES)


def _bias_step_tables():
    exact = REL_BUCKETS // 2
    i = np.arange(Q_BLOCK)[:, None]
    j = np.arange(2 * Q_BLOCK)[None, :]
    steps = i + Q_BLOCK - j
    onehot, band = [], []
    for window, dil in DILATED_BRANCHES:
        dist = (np.arange(2 * Q_BLOCK) * dil).astype(np.int32)
        d = np.maximum(dist, 1).astype(np.float32)
        log_b = exact + (np.log(d / np.float32(exact)) / np.float32(math.log(REL_MAX_DIST / exact))
                         * np.float32(REL_BUCKETS - exact)).astype(np.int32)
        bucket = np.where(dist < exact, dist, np.minimum(log_b, REL_BUCKETS - 1))
        onehot.append(np.eye(REL_BUCKETS, dtype=np.float32)[bucket])
        band.append((steps >= 0) & (steps <= window // dil))
    first = np.broadcast_to(j >= Q_BLOCK, (Q_BLOCK, 2 * Q_BLOCK))
    return np.stack(onehot), np.stack(band), first


def _bias_tables(rel_table):
    onehot, band, first = _bias_step_tables()
    nb = len(DILATED_BRANCHES)
    vec = jnp.einsum('nsb,bh->nhs', onehot, rel_table.astype(F32), precision=lax.Precision.HIGHEST)
    width = 3 * Q_BLOCK
    ext = jnp.concatenate([jnp.zeros((nb, ATTN_HEADS, Q_BLOCK), F32), vec[..., ::-1]], axis=-1)
    hank = jnp.tile(ext, (1, 1, Q_BLOCK + 1))[..., :Q_BLOCK * (width + 1)].reshape(nb, ATTN_HEADS, Q_BLOCK, width + 1)
    bias = hank[:, :, ::-1, Q_BLOCK:width]
    regular = jnp.where(band[:, None], bias, NEG_INF)
    start = jnp.where((band & first[None])[:, None], bias, NEG_INF)
    return jnp.stack([regular, start], axis=1)


def _rglru_kernel(xr_ref, gr_ref, cw_ref, cb_ref, wa_ref, ba_ref, wx_ref, bx_ref, lam_ref, out_ref,
                  xbuf, a_scr, b_scr, h_scr, *, tt):
    t = pl.program_id(1)

    @pl.when(t == 0)
    def _():
        xbuf[0:8, :] = jnp.zeros((8, D_RNN), F32)
        h_scr[...] = jnp.zeros_like(h_scr)

    xbuf[8:8 + tt, :] = xr_ref[...]
    u = cb_ref[...] + cw_ref[3:4, :] * xbuf[8:8 + tt, :]
    for back in range(1, CONV_WIDTH):
        u = u + cw_ref[3 - back:4 - back, :] * xbuf[8 - back:8 - back + tt, :]
    xbuf[0:8, :] = xbuf[tt:tt + 8, :]

    ub = u.astype(BF16)
    r = jax.nn.sigmoid(jnp.dot(ub, wa_ref[...], preferred_element_type=F32) + ba_ref[...])
    gi = jax.nn.sigmoid(jnp.dot(ub, wx_ref[...], preferred_element_type=F32) + bx_ref[...])
    nl = -lam_ref[...]
    softplus = jnp.maximum(nl, 0.0) + jnp.log1p(jnp.exp(-jnp.abs(nl)))
    log_a = (-RG_C) * r * softplus
    a = jnp.exp(log_a)
    a_scr[...] = a
    b_scr[...] = jnp.sqrt(-jnp.tanh(log_a) * (a * a + 1.0)) * (gi * u)

    rc = RNN_CHUNK
    row = lax.broadcasted_iota(I32, (rc, LANES), 0)

    def chunk(c, carry):
        r0 = pl.multiple_of(c * rc, rc)
        for g in range(D_RNN // LANES):
            ls = slice(g * LANES, (g + 1) * LANES)
            aa = a_scr[pl.ds(r0, rc), ls]
            bb = b_scr[pl.ds(r0, rc), ls]
            k = 1
            while k < rc:
                keep = row >= k
                a_sh = pltpu.roll(aa, k, 0)
                b_sh = pltpu.roll(bb, k, 0)
                bb = jnp.where(keep, aa * b_sh + bb, bb)
                aa = jnp.where(keep, aa * a_sh, aa)
                k *= 2
            h = aa * h_scr[0:1, ls] + bb
            h_scr[0:1, ls] = h[rc - 1:rc, :]
            out_ref[pl.ds(r0, rc), ls] = h * jax.nn.gelu(gr_ref[pl.ds(r0, rc), ls], approximate=True)
        return carry

    lax.fori_loop(0, tt // rc, chunk, 0)


def _rglru(xr, gr, cw, cb, wa, ba, wx, bx, lam, b, s):
    tt = RNN_TILE
    nt = s // tt
    row = lambda bb, t: (bb * nt + t, 0)
    fix = lambda bb, t: (0, 0)
    vec = pl.BlockSpec((1, D_RNN), fix)
    mat = pl.BlockSpec((D_RNN, D_RNN), fix)
    return pl.pallas_call(
        functools.partial(_rglru_kernel, tt=tt),
        grid=(b, nt),
        in_specs=[pl.BlockSpec((tt, D_RNN), row), pl.BlockSpec((tt, D_RNN), row),
                  pl.BlockSpec((CONV_WIDTH, D_RNN), fix), vec, mat, vec, mat, vec, vec],
        out_specs=pl.BlockSpec((tt, D_RNN), row),
        out_shape=jax.ShapeDtypeStruct((b * s, D_RNN), F32),
        scratch_shapes=[pltpu.VMEM((tt + 8, D_RNN), F32), pltpu.VMEM((tt, D_RNN), F32),
                        pltpu.VMEM((tt, D_RNN), F32), pltpu.VMEM((8, D_RNN), F32)],
        compiler_params=_cparams("parallel", "arbitrary"),
    )(xr, gr, cw, cb, wa, ba, wx, bx, lam)


def _block_diag(w):
    nb, n, _ = w.shape
    eye = jnp.eye(nb, dtype=w.dtype)
    return (eye[:, None, :, None] * w[:, :, None, :]).reshape(nb * n, nb * n)


def _split_bf16(x):
    hi = x.astype(BF16)
    return hi, (x - hi.astype(F32)).astype(BF16)


def _mix_kernel(x_ref, o1_ref, o2_ref, o3_ref, l1_ref, l2_ref, l3_ref, rnn_ref, ga_ref, gn_ref, wo_ref, gf_ref,
                wrh_ref, wrl_ref, br_ref, ex_ref, tri_ref, upper_ref,
                x1_ref, h2_ref, pos_ref, gate_ref, cnt_ref):
    l1, l2, l3 = l1_ref[...], l2_ref[...], l3_ref[...]
    mx = jnp.maximum(jnp.maximum(l1, l2), l3)
    e1, e2, e3 = jnp.exp(l1 - mx), jnp.exp(l2 - mx), jnp.exp(l3 - mx)
    inv = 1.0 / (e1 + e2 + e3)

    def widen(w):
        hi, lo = _split_bf16(w)
        return (jnp.dot(hi, ex_ref[...], preferred_element_type=F32)
                + jnp.dot(lo, ex_ref[...], preferred_element_type=F32))

    attn = widen(e1 * inv) * o1_ref[...] + widen(e2 * inv) * o2_ref[...] + widen(e3 * inv) * o3_ref[...]
    na = _rms(attn, ga_ref[...]).astype(BF16)
    nr = _rms(rnn_ref[...], gn_ref[...]).astype(BF16)
    x1 = (x_ref[...] + jnp.dot(na, wo_ref[0:D_ATTN, :], preferred_element_type=F32)
          + jnp.dot(nr, wo_ref[D_ATTN:, :], preferred_element_type=F32))
    x1_ref[...] = x1
    h2 = _rms(x1, gf_ref[...])
    h2_ref[...] = h2.astype(BF16)

    hh, hl = _split_bf16(h2)
    logits = (jnp.dot(hh, wrh_ref[...], preferred_element_type=F32)
              + jnp.dot(hl, wrh_ref[...], preferred_element_type=F32)
              + jnp.dot(hh, wrl_ref[...], preferred_element_type=F32)) + br_ref[...]
    tm = logits.shape[0]
    lane = lax.broadcasted_iota(I32, (tm, LANES), 1)
    lanef = lane.astype(F32)
    big = float(LANES)

    def top(vals):
        m = jnp.max(vals, axis=-1, keepdims=True)
        return m, jnp.min(jnp.where(vals == m, lanef, big), axis=-1, keepdims=True)

    is_group = lane < N_GROUPS
    gmax, gsel = top(jnp.where(is_group, logits, NEG_INF))
    g_w = 1.0 / jnp.sum(jnp.where(is_group, jnp.exp(logits - gmax), 0.0), axis=-1, keepdims=True)
    lo_lane = ROUTER_LANE0 + EXPERTS_PER_GROUP * gsel
    in_group = jnp.logical_and(lanef >= lo_lane, lanef < lo_lane + EXPERTS_PER_GROUP)
    el = jnp.where(in_group, logits, NEG_INF)
    v1, i1 = top(el)
    v2, i2 = top(jnp.where(lanef == i1, NEG_INF, el))
    t = jnp.exp(v2 - v1)
    p1 = 1.0 / (1.0 + t)
    gate1 = g_w * p1
    gate2 = g_w * (t * p1)

    oh1 = (lanef == i1).astype(F32)
    oh2 = (lanef == i2).astype(F32)
    cnt = oh1 + oh2
    prefix = jnp.dot(tri_ref[...], cnt.astype(BF16), preferred_element_type=F32)
    total = jnp.sum(cnt, axis=0, keepdims=True)
    chunks = jnp.floor((total + (ROW_CHUNK - 1)) * (1.0 / ROW_CHUNK))
    seg_start = ROW_CHUNK * jnp.dot(jnp.broadcast_to(chunks, (8, LANES)).astype(BF16), upper_ref[...],
                                    preferred_element_type=F32)[0:1, :]
    base = seg_start + prefix
    pos1 = jnp.sum(oh1 * base, axis=-1, keepdims=True)
    pos2 = jnp.sum(oh2 * base, axis=-1, keepdims=True)
    cnt_ref[...] = jnp.broadcast_to(total, cnt_ref.shape).astype(I32)
    pos_ref[...] = jnp.where(lane == 0, pos1, jnp.where(lane == 1, pos2, 0.0))
    gate_ref[...] = jnp.where(lane == 0, gate1, jnp.where(lane == 1, gate2, 0.0))


def _mix(x2, o, lse, rnn, ga, gn, wo_bf, gf, wr_hi, wr_lo, br, ex, tri, upper):
    m = x2.shape[0]
    tm = TOKEN_TILE
    row = lambda i: (i, 0)
    fix = lambda i: (0, 0)
    t512 = pl.BlockSpec((tm, D_ATTN), row)
    t128 = pl.BlockSpec((tm, LANES), row)
    t1024 = pl.BlockSpec((tm, D_MODEL), row)
    return pl.pallas_call(
        _mix_kernel,
        grid=(m // tm,),
        in_specs=[t1024, t512, t512, t512, t128, t128, t128, t512,
                  pl.BlockSpec((1, D_ATTN), fix), pl.BlockSpec((1, D_RNN), fix),
                  pl.BlockSpec((D_MIX, D_MODEL), fix), pl.BlockSpec((1, D_MODEL), fix),
                  pl.BlockSpec((D_MODEL, LANES), fix), pl.BlockSpec((D_MODEL, LANES), fix),
                  pl.BlockSpec((1, LANES), fix), pl.BlockSpec((LANES, D_ATTN), fix),
                  pl.BlockSpec((tm, tm), fix), pl.BlockSpec((LANES, LANES), fix)],
        out_specs=[t1024, t1024, t128, t128, pl.BlockSpec((8, LANES), row)],
        out_shape=[jax.ShapeDtypeStruct((m, D_MODEL), F32), jax.ShapeDtypeStruct((m, D_MODEL), BF16),
                   jax.ShapeDtypeStruct((m, LANES), F32), jax.ShapeDtypeStruct((m, LANES), F32),
                   jax.ShapeDtypeStruct((m // tm * 8, LANES), I32)],
        compiler_params=_cparams("parallel"),
    )(x2, o[0], o[1], o[2], lse[0], lse[1], lse[2], rnn, ga, gn, wo_bf, gf, wr_hi, wr_lo, br, ex, tri, upper)


def _chunk_copy(src, src_row, dst, dst_row, sem):
    return pltpu.make_async_copy(src.at[pl.ds(src_row, ROW_CHUNK)], dst.at[pl.ds(dst_row, ROW_CHUNK)], sem)


def _for_each_chunk(i, nch_ref, lstart_ref, segdst_ref, copy):
    def per_expert(e, total):
        n = nch_ref[i * N_EXPERTS + e]
        loc0 = lstart_ref[i * N_EXPERTS + e]
        dst0 = segdst_ref[i * N_EXPERTS + e]

        def per_chunk(c, carry):
            off = c * ROW_CHUNK
            copy(pl.multiple_of(loc0 + off, ROW_CHUNK), pl.multiple_of(dst0 + off, ROW_CHUNK))
            return carry

        lax.fori_loop(0, n, per_chunk, 0)
        return total + n

    return lax.fori_loop(0, N_EXPERTS, per_expert, jnp.int32(0))


def _dispatch_kernel(nch_ref, lstart_ref, segdst_ref, tailn_ref, taildst_ref,
                     h_ref, pos_ref, gate_ref, rows_ref, sorted_buf, zero_buf, sem, *, tm):
    i = pl.program_id(0)
    pos_t = pos_ref[...].T
    row = lax.broadcasted_iota(I32, (TILE_ROWS, tm), 0).astype(F32)
    hit1 = row == pos_t[0:1, :]
    hit2 = row == pos_t[1:2, :]
    onehot = lambda hit: jnp.where(hit, 1.0, 0.0).astype(BF16)
    sorted_buf[:, 0:D_MODEL] = jnp.dot(onehot(jnp.logical_or(hit1, hit2)), h_ref[...], preferred_element_type=F32)
    lane = lax.broadcasted_iota(I32, (tm, LANES), 1)
    gates = gate_ref[...]

    def pieces(g):
        hi = g.astype(BF16).astype(F32)
        mid = (g - hi).astype(BF16).astype(F32)
        lo = g - hi - mid
        return jnp.where(lane == 0, hi, jnp.where(lane == 1, mid, jnp.where(lane == 2, lo, 0.0))).astype(BF16)

    sorted_buf[:, D_MODEL:] = (
        jnp.dot(onehot(hit1), pieces(gates[:, 0:1]), preferred_element_type=F32)
        + jnp.dot(onehot(hit2), pieces(gates[:, 1:2]), preferred_element_type=F32))

    def send(loc, dst):
        _chunk_copy(sorted_buf, loc, rows_ref, dst, sem).start()

    n_sent = _for_each_chunk(i, nch_ref, lstart_ref, segdst_ref, send)

    zero_buf[...] = jnp.zeros_like(zero_buf)

    def tails():
        def per_expert(e, total):
            n = tailn_ref[e]
            dst0 = taildst_ref[e]

            def per_chunk(c, carry):
                _chunk_copy(zero_buf, 0, rows_ref, pl.multiple_of(dst0 + c * ROW_CHUNK, ROW_CHUNK), sem).start()
                return carry

            lax.fori_loop(0, n, per_chunk, 0)
            return total + n

        return lax.fori_loop(0, N_EXPERTS + 1, per_expert, jnp.int32(0))

    n_sent = n_sent + lax.cond(i == 0, tails, lambda: jnp.int32(0))

    def drain(c, carry):
        _chunk_copy(sorted_buf, 0, rows_ref, 0, sem).wait()
        return carry

    lax.fori_loop(0, n_sent, drain, 0)


def _dispatch(meta, h2, pos, gates, n_rows):
    m = h2.shape[0]
    tm = TOKEN_TILE
    tile = lambda i, *_: (i, 0)
    return pl.pallas_call(
        functools.partial(_dispatch_kernel, tm=tm),
        grid_spec=pltpu.PrefetchScalarGridSpec(
            num_scalar_prefetch=5,
            grid=(m // tm,),
            in_specs=[pl.BlockSpec((tm, D_MODEL), tile), pl.BlockSpec((tm, LANES), tile),
                      pl.BlockSpec((tm, LANES), tile)],
            out_specs=pl.BlockSpec(memory_space=pl.ANY),
            scratch_shapes=[pltpu.VMEM((TILE_ROWS, ROW_WIDTH), F32), pltpu.VMEM((ROW_CHUNK, ROW_WIDTH), F32),
                            pltpu.SemaphoreType.DMA(())],
        ),
        out_shape=jax.ShapeDtypeStruct((n_rows, ROW_WIDTH), F32),
        compiler_params=_cparams("arbitrary"),
    )(*meta, h2, pos, gates)


def _expert_kernel(be_ref, nused_ref, rows_ref, wg_ref, wu_ref, wd_ref, out_ref, wgb, wub, wdb):
    i = pl.program_id(0)
    changed = jnp.logical_or(i == 0, be_ref[i] != be_ref[jnp.maximum(i - 1, 0)])

    @pl.when(changed)
    def _():
        wgb[...] = wg_ref[...].astype(BF16)
        wub[...] = wu_ref[...].astype(BF16)
        wdb[...] = wd_ref[...].astype(BF16)

    @pl.when(i < nused_ref[0])
    def _():
        xb = rows_ref[:, 0:D_MODEL].astype(BF16)
        tail = rows_ref[:, D_MODEL:]
        gate = tail[:, 0:1] + tail[:, 1:2] + tail[:, 2:3]
        g = jnp.dot(xb, wgb[...], preferred_element_type=F32)
        u = jnp.dot(xb, wub[...], preferred_element_type=F32)
        act = (g * jax.nn.sigmoid(g) * u).astype(BF16)
        out_ref[...] = jnp.dot(act, wdb[...], preferred_element_type=F32) * gate

    @pl.when(i >= nused_ref[0])
    def _():
        out_ref[...] = jnp.zeros_like(out_ref)


def _experts(block_expert, n_used, rows, w_gate, w_up, w_down, layer):
    n_rows = rows.shape[0]
    blk = EXPERT_BLOCK
    wmap = lambda i, be, nu: (layer, be[i], 0, 0)
    rmap = lambda i, be, nu: (jnp.minimum(i, nu[0] - 1), 0)
    return pl.pallas_call(
        _expert_kernel,
        grid_spec=pltpu.PrefetchScalarGridSpec(
            num_scalar_prefetch=2,
            grid=(n_rows // blk,),
            in_specs=[pl.BlockSpec((blk, ROW_WIDTH), rmap),
                      pl.BlockSpec((None, None, D_MODEL, D_EXPERT), wmap),
                      pl.BlockSpec((None, None, D_MODEL, D_EXPERT), wmap),
                      pl.BlockSpec((None, None, D_EXPERT, D_MODEL), wmap)],
            out_specs=pl.BlockSpec((blk, D_MODEL), lambda i, be, nu: (i, 0)),
            scratch_shapes=[pltpu.VMEM((D_MODEL, D_EXPERT), BF16), pltpu.VMEM((D_MODEL, D_EXPERT), BF16),
                            pltpu.VMEM((D_EXPERT, D_MODEL), BF16)],
        ),
        out_shape=jax.ShapeDtypeStruct((n_rows, D_MODEL), F32),
        compiler_params=_cparams("arbitrary"),
    )(block_expert, n_used, rows, w_gate, w_up, w_down)


def _combine_kernel(nch_ref, lstart_ref, segdst_ref, x1_ref, pos_ref, rows_ref, out_ref, local, sem, *, tm):
    i = pl.program_id(0)

    @pl.when(i == 0)
    def _():
        local[...] = jnp.zeros_like(local)

    def fetch(loc, src):
        _chunk_copy(rows_ref, src, local, loc, sem).start()

    n_sent = _for_each_chunk(i, nch_ref, lstart_ref, segdst_ref, fetch)

    def drain(c, carry):
        _chunk_copy(rows_ref, 0, local, 0, sem).wait()
        return carry

    lax.fori_loop(0, n_sent, drain, 0)

    pos = pos_ref[...]
    col = lax.broadcasted_iota(I32, (tm, TILE_ROWS), 1).astype(F32)
    pick = jnp.where(jnp.logical_or(col == pos[:, 0:1], col == pos[:, 1:2]), 1.0, 0.0).astype(BF16)
    hi, lo = _split_bf16(local[...])
    out_ref[...] = x1_ref[...] + (jnp.dot(pick, hi, preferred_element_type=F32)
                                  + jnp.dot(pick, lo, preferred_element_type=F32))


def _combine(meta, x1, pos, out_rows):
    m = x1.shape[0]
    tm = TOKEN_TILE
    tile = lambda i, *_: (i, 0)
    return pl.pallas_call(
        functools.partial(_combine_kernel, tm=tm),
        grid_spec=pltpu.PrefetchScalarGridSpec(
            num_scalar_prefetch=3,
            grid=(m // tm,),
            in_specs=[pl.BlockSpec((tm, D_MODEL), tile), pl.BlockSpec((tm, LANES), tile),
                      pl.BlockSpec(memory_space=pl.ANY)],
            out_specs=pl.BlockSpec((tm, D_MODEL), tile),
            scratch_shapes=[pltpu.VMEM((TILE_ROWS, D_MODEL), F32), pltpu.SemaphoreType.DMA(())],
        ),
        out_shape=jax.ShapeDtypeStruct((m, D_MODEL), F32),
        compiler_params=_cparams("arbitrary"),
    )(*meta, x1, pos, out_rows)


def _dispatch_plan(cnt_tiles, n_tiles, n_blocks):
    cnt = cnt_tiles.reshape(n_tiles, 8, LANES)[:, 0, ROUTER_LANE0:ROUTER_LANE0 + N_EXPERTS]
    seg = (cnt + ROW_CHUNK - 1) // ROW_CHUNK * ROW_CHUNK
    used = jnp.sum(seg, axis=0)
    padded = (used + EXPERT_BLOCK - 1) // EXPERT_BLOCK * EXPERT_BLOCK
    p_ends = jnp.cumsum(padded)
    p_starts = p_ends - padded
    seg_dst = p_starts[None, :] + jnp.cumsum(seg, axis=0) - seg
    loc_start = jnp.cumsum(seg, axis=1) - seg
    flat = lambda a: a.reshape(-1).astype(I32)
    block_row = jnp.arange(n_blocks, dtype=I32) * EXPERT_BLOCK
    block_expert = jnp.minimum(jnp.sum((p_ends[None, :] <= block_row[:, None]).astype(I32), axis=1), N_EXPERTS - 1)
    n_used = (p_ends[-1:] // EXPERT_BLOCK).astype(I32)
    chunk_meta = (flat(seg // ROW_CHUNK), flat(loc_start), flat(seg_dst))
    rest = n_blocks * EXPERT_BLOCK - p_ends[-1:]
    tail_meta = (flat(jnp.concatenate([padded - used, rest]) // ROW_CHUNK),
                 flat(jnp.concatenate([p_starts + used, p_ends[-1:]])))
    return chunk_meta, tail_meta, block_expert.astype(I32), n_used


def _constants():
    gm = np.kron(np.eye(ATTN_HEADS), np.full((HEAD_DIM, HEAD_DIM), 1.0 / HEAD_DIM))
    ex = np.zeros((LANES, D_ATTN), np.float32)
    for h in range(ATTN_HEADS):
        ex[h, h * HEAD_DIM:(h + 1) * HEAD_DIM] = 1.0
    tri = np.tril(np.ones((TOKEN_TILE, TOKEN_TILE), np.float32), -1)
    upper = np.triu(np.ones((LANES, LANES), np.float32), 1)
    return jnp.asarray(gm, BF16), jnp.asarray(ex, BF16), jnp.asarray(tri, BF16), jnp.asarray(upper, BF16)


def kernel(x, rel_bias_table, norm_mix, w_in, q_norm, k_norm, conv_w, conv_b, rg_w_a, rg_b_a, rg_w_x, rg_b_x,
           rg_lambda, norm_attn_out, norm_rnn_out, w_out, norm_ffn, router_group_w, router_group_b,
           router_expert_w, router_expert_b, expert_w_gate, expert_w_up, expert_w_down):
    b, s, d = x.shape
    depth = w_in.shape[0]
    m = b * s
    span = DILATED_BRANCHES[-1][1] * Q_BLOCK
    assert d == D_MODEL and s % span == 0 and s % RNN_TILE == 0 and m % TOKEN_TILE == 0

    gm, ex, tri, upper = _constants()
    bias = _bias_tables(rel_bias_table)
    scale = HEAD_DIM ** -0.5
    n_tiles = m // TOKEN_TILE
    n_blocks = -(-(m * TOP_K + n_tiles * N_EXPERTS * (ROW_CHUNK - 1)) // EXPERT_BLOCK) + N_EXPERTS
    n_rows = n_blocks * EXPERT_BLOCK
    row1 = lambda v: v.reshape(1, -1).astype(F32)

    x2 = x.reshape(m, d).astype(F32)
    for l in range(depth):
        q, k, v, xr, gr = _proj(x2, row1(norm_mix[l]), w_in[l].astype(BF16),
                                row1(jnp.tile(q_norm[l], ATTN_HEADS) * scale),
                                row1(jnp.tile(k_norm[l], ATTN_HEADS)), gm)
        outs, lses = [], []
        for n, (_, dil) in enumerate(DILATED_BRANCHES):
            o, lse = _attention_branch(q, k, v, bias[n], b, s, dil)
            outs.append(o)
            lses.append(lse)
        rnn = _rglru(xr, gr, conv_w[l].astype(F32), row1(conv_b[l]),
                     _block_diag(rg_w_a[l]).astype(BF16), row1(rg_b_a[l]),
                     _block_diag(rg_w_x[l]).astype(BF16), row1(rg_b_x[l]), row1(rg_lambda[l]), b, s)

        wr = jnp.zeros((D_MODEL, LANES), F32)
        wr = wr.at[:, :N_GROUPS].set(router_group_w[l]).at[:, N_GROUPS:N_GROUPS + N_EXPERTS].set(router_expert_w[l])
        wr_hi, wr_lo = _split_bf16(wr)
        br = jnp.zeros((1, LANES), F32)
        br = br.at[0, :N_GROUPS].set(router_group_b[l]).at[0, N_GROUPS:N_GROUPS + N_EXPERTS].set(router_expert_b[l])
        x1, h2, pos, gates, cnt = _mix(x2, outs, lses, rnn, row1(norm_attn_out[l]), row1(norm_rnn_out[l]),
                                       w_out[l].astype(BF16), row1(norm_ffn[l]), wr_hi, wr_lo, br, ex, tri, upper)

        chunk_meta, tail_meta, block_expert, n_used = _dispatch_plan(cnt, n_tiles, n_blocks)
        rows = _dispatch(chunk_meta + tail_meta, h2, pos, gates, n_rows)
        out_rows = _experts(block_expert, n_used, rows, expert_w_gate, expert_w_up, expert_w_down, l)
        x2 = _combine(chunk_meta, x1, pos, out_rows)
    return x2.reshape(b, s, d).astype(x.dtype)
```

```python
import functools
import math

import numpy as np
import jax
import jax.numpy as jnp
from jax import lax
from jax.experimental import pallas as pl
from jax.experimental.pallas import tpu as pltpu

F32 = jnp.float32
BF16 = jnp.bfloat16
I32 = jnp.int32
U32 = jnp.uint32

D_MODEL = 1024
ATTN_HEADS = 8
HEAD_DIM = 64
D_ATTN = ATTN_HEADS * HEAD_DIM
RNN_BLOCKS = 8
D_RNN = 512
D_MIX = D_ATTN + D_RNN
D_IN = 3 * D_ATTN + 2 * D_RNN
DILATED_BRANCHES = ((128, 1), (512, 4), (2048, 16))
Q_BLOCK = 128
REL_BUCKETS = 32
REL_MAX_DIST = 2048
CONV_WIDTH = 4
RG_C = 8.0
N_GROUPS = 4
EXPERTS_PER_GROUP = 8
N_EXPERTS = N_GROUPS * EXPERTS_PER_GROUP
TOP_K = 2
D_EXPERT = 512
EPS = 1e-6
NEG_INF = -1e30
LOG2E = math.log2(math.e)
LN2 = math.log(2.0)

LANES = 128
HEAD_PAIRS = D_ATTN // LANES
ROUTER_LANE0 = N_GROUPS
TOKEN_TILE = 512
ATTN_TILE = 2048
RNN_TILE = 512
RNN_CHUNK = 64
EXPERT_BLOCK = 256
ROW_CHUNK = 8
TILE_ROWS = -(-(TOKEN_TILE * TOP_K + N_EXPERTS * (ROW_CHUNK - 1)) // 256) * 256
ROW_WORDS = D_MODEL // 2 + LANES
VMEM_LIMIT = 48 * 1024 * 1024


def _cparams(*sem):
    return pltpu.CompilerParams(dimension_semantics=sem, vmem_limit_bytes=VMEM_LIMIT)


def _rms(x, gain):
    return x * lax.rsqrt(jnp.mean(x * x, axis=-1, keepdims=True) + EPS) * gain


def _proj_kernel(x_ref, g_ref, w_ref, qg_ref, kg_ref, gm_ref, q_ref, k_ref, v_ref, xr_ref, gr_ref):
    hb = _rms(x_ref[...], g_ref[...]).astype(BF16)

    def sec(n):
        return jnp.dot(hb, w_ref[:, n * D_ATTN:(n + 1) * D_ATTN], preferred_element_type=F32)

    def head_norm(z, gain):
        ms = jnp.dot((z * z).astype(BF16), gm_ref[...], preferred_element_type=F32)
        return z * lax.rsqrt(ms + EPS) * gain

    def put_slabs(ref, z):
        for hp in range(HEAD_PAIRS):
            ref[hp] = z[:, hp * LANES:(hp + 1) * LANES]

    put_slabs(q_ref, head_norm(sec(0), qg_ref[...]))
    put_slabs(k_ref, head_norm(sec(1), kg_ref[...]))
    put_slabs(v_ref, sec(2))
    xr_ref[...] = sec(3)
    gr_ref[...] = sec(4)


def _proj(x2, gain, w_in_bf, qg, kg, gm):
    m = x2.shape[0]
    tm = TOKEN_TILE
    row = lambda i: (i, 0)
    fix = lambda i: (0, 0)
    return pl.pallas_call(
        _proj_kernel,
        grid=(m // tm,),
        in_specs=[
            pl.BlockSpec((tm, D_MODEL), row),
            pl.BlockSpec((1, D_MODEL), fix),
            pl.BlockSpec((D_MODEL, D_IN), fix),
            pl.BlockSpec((1, D_ATTN), fix),
            pl.BlockSpec((1, D_ATTN), fix),
            pl.BlockSpec((D_ATTN, D_ATTN), fix),
        ],
        out_specs=[pl.BlockSpec((HEAD_PAIRS, tm, LANES), lambda i: (0, i, 0))] * 3
        + [pl.BlockSpec((tm, D_RNN), row)] * 2,
        out_shape=[jax.ShapeDtypeStruct((HEAD_PAIRS, m, LANES), F32)] * 3
        + [jax.ShapeDtypeStruct((m, D_RNN), F32)] * 2,
        compiler_params=_cparams("parallel"),
    )(x2, gain, w_in_bf, qg, kg, gm)


def _rows(start, size, stride):
    return pl.ds(start, size) if stride == 1 else pl.ds(start, size, stride=stride)


def _attn_kernel(q_ref, kc_ref, kp_ref, vc_ref, vp_ref, bias_ref, o_ref, lse_ref, qd, kd, vd, *, dil):
    i = pl.program_id(2)
    rows = q_ref.shape[0] // dil
    seg = Q_BLOCK + rows
    for r in range(dil):
        qd[r * rows:(r + 1) * rows, :] = q_ref[_rows(r, rows, dil), :].astype(BF16)
        for prev_ref, cur_ref, dst in ((kp_ref, kc_ref, kd), (vp_ref, vc_ref, vd)):
            dst[r * seg:r * seg + Q_BLOCK, :] = prev_ref[_rows(r, Q_BLOCK, dil), :].astype(BF16)
            dst[r * seg + Q_BLOCK:(r + 1) * seg, :] = cur_ref[_rows(r, rows, dil), :].astype(BF16)

    per = rows // Q_BLOCK
    lane = lax.broadcasted_iota(I32, (Q_BLOCK, LANES), 1)
    low = lane < HEAD_DIM

    def block(blk, carry):
        r = blk // per
        n = blk % per
        q0 = pl.multiple_of(r * rows + n * Q_BLOCK, Q_BLOCK)
        k0 = pl.multiple_of(r * seg + n * Q_BLOCK, Q_BLOCK)
        sel = jnp.where(jnp.logical_and(i == 0, n == 0), 1, 0)
        q = qd[pl.ds(q0, Q_BLOCK), :]
        kk = kd[pl.ds(k0, 2 * Q_BLOCK), :]
        vv = vd[pl.ds(k0, 2 * Q_BLOCK), :]
        outs, lses = [], []
        for hh in range(2):
            qm = jnp.where(low if hh == 0 else jnp.logical_not(low), q, jnp.zeros_like(q))
            s = lax.dot_general(qm, kk, (((1,), (1,)), ((), ())), preferred_element_type=F32)
            s = s + bias_ref[sel, hh]
            mx = jnp.max(s, axis=-1, keepdims=True)
            p = jnp.exp2(s - mx)
            l = jnp.sum(p, axis=-1, keepdims=True)
            pv = jnp.dot(p.astype(BF16), vv, preferred_element_type=F32)
            outs.append(pv * (1.0 / l))
            lses.append(mx * LN2 + jnp.log(l))
        dst = _rows(r + dil * Q_BLOCK * n, Q_BLOCK, dil)
        o_ref[dst, :] = jnp.where(low, outs[0], outs[1])
        lse_ref[dst, :] = jnp.where(low, lses[0], lses[1])
        return carry

    lax.fori_loop(0, dil * per, block, 0)


def _attention_branch(q, k, v, bias, b, s, dil):
    tile = ATTN_TILE
    nt = s // tile
    prev_rows = Q_BLOCK * dil
    ratio = tile // prev_rows
    cur = lambda bb, hp, i: (hp, bb * nt + i, 0)
    prev = lambda bb, hp, i: (hp, jnp.maximum((bb * nt + i) * ratio - 1, 0), 0)
    blk = pl.BlockSpec((None, tile, LANES), cur)
    pblk = pl.BlockSpec((None, prev_rows, LANES), prev)
    shape = jax.ShapeDtypeStruct(q.shape, F32)
    return pl.pallas_call(
        functools.partial(_attn_kernel, dil=dil),
        grid=(b, HEAD_PAIRS, nt),
        in_specs=[blk, blk, pblk, blk, pblk,
                  pl.BlockSpec((2, 2, Q_BLOCK, 2 * Q_BLOCK), lambda bb, hp, i: (0, hp, 0, 0))],
        out_specs=[blk, blk],
        out_shape=[shape, shape],
        scratch_shapes=[pltpu.VMEM((tile, LANES), BF16), pltpu.VMEM((tile + prev_rows, LANES), BF16),
                        pltpu.VMEM((tile + prev_rows, LANES), BF16)],
        compiler_params=_cparams("parallel", "parallel", "parallel"),
    )(q, k, k, v, v, bias)


def _bias_step_tables():
    exact = REL_BUCKETS // 2
    i = np.arange(Q_BLOCK)[:, None]
    j = np.arange(2 * Q_BLOCK)[None, :]
    steps = i + Q_BLOCK - j
    onehot, band = [], []
    for window, dil in DILATED_BRANCHES:
        dist = (np.arange(2 * Q_BLOCK) * dil).astype(np.int32)
        d = np.maximum(dist, 1).astype(np.float32)
        log_b = exact + (np.log(d / np.float32(exact)) / np.float32(math.log(REL_MAX_DIST / exact))
                         * np.float32(REL_BUCKETS - exact)).astype(np.int32)
        bucket = np.where(dist < exact, dist, np.minimum(log_b, REL_BUCKETS - 1))
        onehot.append(np.eye(REL_BUCKETS, dtype=np.float32)[bucket])
        band.append((steps >= 0) & (steps <= window // dil))
    first = np.broadcast_to(j >= Q_BLOCK, (Q_BLOCK, 2 * Q_BLOCK))
    return np.stack(onehot), np.stack(band), first


def _bias_tables(rel_table):
    onehot, band, first = _bias_step_tables()
    nb = len(DILATED_BRANCHES)
    vec = LOG2E * jnp.einsum('nsb,bh->nhs', onehot, rel_table.astype(F32), precision=lax.Precision.HIGHEST)
    width = 3 * Q_BLOCK
    ext = jnp.concatenate([jnp.zeros((nb, ATTN_HEADS, Q_BLOCK), F32), vec[..., ::-1]], axis=-1)
    hank = jnp.tile(ext, (1, 1, Q_BLOCK + 1))[..., :Q_BLOCK * (width + 1)].reshape(nb, ATTN_HEADS, Q_BLOCK, width + 1)
    bias = hank[:, :, ::-1, Q_BLOCK:width]
    regular = jnp.where(band[:, None], bias, NEG_INF)
    start = jnp.where((band & first[None])[:, None], bias, NEG_INF)
    return jnp.stack([regular, start], axis=1)


def _rglru_kernel(xr_ref, gr_ref, cw_ref, cb_ref, wa_ref, ba_ref, wx_ref, bx_ref, lam_ref, out_ref,
                  xbuf, a_scr, b_scr, h_scr, *, tt):
    t = pl.program_id(1)

    @pl.when(t == 0)
    def _():
        xbuf[0:8, :] = jnp.zeros((8, D_RNN), F32)
        h_scr[...] = jnp.zeros_like(h_scr)

    xbuf[8:8 + tt, :] = xr_ref[...]
    u = cb_ref[...] + cw_ref[3:4, :] * xbuf[8:8 + tt, :]
    for back in range(1, CONV_WIDTH):
        u = u + cw_ref[3 - back:4 - back, :] * xbuf[8 - back:8 - back + tt, :]
    xbuf[0:8, :] = xbuf[tt:tt + 8, :]

    ub = u.astype(BF16)
    r = jax.nn.sigmoid(jnp.dot(ub, wa_ref[...], preferred_element_type=F32) + ba_ref[...])
    gi = jax.nn.sigmoid(jnp.dot(ub, wx_ref[...], preferred_element_type=F32) + bx_ref[...])
    nl = -lam_ref[...]
    softplus = jnp.maximum(nl, 0.0) + jnp.log1p(jnp.exp(-jnp.abs(nl)))
    log_a = (-RG_C) * r * softplus
    a = jnp.exp(log_a)
    a_scr[...] = a
    b_scr[...] = jnp.sqrt(-jnp.tanh(log_a) * (a * a + 1.0)) * (gi * u)

    rc = RNN_CHUNK
    row = lax.broadcasted_iota(I32, (rc, LANES), 0)

    def chunk(c, carry):
        r0 = pl.multiple_of(c * rc, rc)
        for g in range(D_RNN // LANES):
            ls = slice(g * LANES, (g + 1) * LANES)
            aa = a_scr[pl.ds(r0, rc), ls]
            bb = b_scr[pl.ds(r0, rc), ls]
            k = 1
            while k < rc:
                keep = row >= k
                a_sh = pltpu.roll(aa, k, 0)
                b_sh = pltpu.roll(bb, k, 0)
                bb = jnp.where(keep, aa * b_sh + bb, bb)
                aa = jnp.where(keep, aa * a_sh, aa)
                k *= 2
            h = aa * h_scr[0:1, ls] + bb
            h_scr[0:1, ls] = h[rc - 1:rc, :]
            out_ref[pl.ds(r0, rc), ls] = h * jax.nn.gelu(gr_ref[pl.ds(r0, rc), ls], approximate=True)
        return carry

    lax.fori_loop(0, tt // rc, chunk, 0)


def _rglru(xr, gr, cw, cb, wa, ba, wx, bx, lam, b, s):
    tt = RNN_TILE
    nt = s // tt
    row = lambda bb, t: (bb * nt + t, 0)
    fix = lambda bb, t: (0, 0)
    vec = pl.BlockSpec((1, D_RNN), fix)
    mat = pl.BlockSpec((D_RNN, D_RNN), fix)
    return pl.pallas_call(
        functools.partial(_rglru_kernel, tt=tt),
        grid=(b, nt),
        in_specs=[pl.BlockSpec((tt, D_RNN), row), pl.BlockSpec((tt, D_RNN), row),
                  pl.BlockSpec((CONV_WIDTH, D_RNN), fix), vec, mat, vec, mat, vec, vec],
        out_specs=pl.BlockSpec((tt, D_RNN), row),
        out_shape=jax.ShapeDtypeStruct((b * s, D_RNN), F32),
        scratch_shapes=[pltpu.VMEM((tt + 8, D_RNN), F32), pltpu.VMEM((tt, D_RNN), F32),
                        pltpu.VMEM((tt, D_RNN), F32), pltpu.VMEM((8, D_RNN), F32)],
        compiler_params=_cparams("parallel", "arbitrary"),
    )(xr, gr, cw, cb, wa, ba, wx, bx, lam)


def _block_diag(w):
    nb, n, _ = w.shape
    eye = jnp.eye(nb, dtype=w.dtype)
    return (eye[:, None, :, None] * w[:, :, None, :]).reshape(nb * n, nb * n)


def _split_bf16(x):
    hi = x.astype(BF16)
    return hi, (x - hi.astype(F32)).astype(BF16)


def _mix_kernel(x_ref, o1_ref, o2_ref, o3_ref, l1_ref, l2_ref, l3_ref, rnn_ref, ga_ref, gn_ref, wo_ref, gf_ref,
                wrh_ref, wrl_ref, br_ref, tri_ref, upper_ref,
                x1_ref, h2_ref, pos_ref, gate_ref, cnt_ref):
    slabs, sumsq = [], 0.0
    for hp in range(HEAD_PAIRS):
        l1, l2, l3 = l1_ref[hp], l2_ref[hp], l3_ref[hp]
        mx = jnp.maximum(jnp.maximum(l1, l2), l3)
        e1, e2, e3 = jnp.exp(l1 - mx), jnp.exp(l2 - mx), jnp.exp(l3 - mx)
        a = (e1 * o1_ref[hp] + e2 * o2_ref[hp] + e3 * o3_ref[hp]) * (1.0 / (e1 + e2 + e3))
        slabs.append(a)
        sumsq = sumsq + jnp.sum(a * a, axis=-1, keepdims=True)
    scale = lax.rsqrt(sumsq * (1.0 / D_ATTN) + EPS)
    na = (jnp.concatenate([a * scale for a in slabs], axis=1) * ga_ref[...]).astype(BF16)
    nr = _rms(rnn_ref[...], gn_ref[...]).astype(BF16)
    x1 = (x_ref[...] + jnp.dot(na, wo_ref[0:D_ATTN, :], preferred_element_type=F32)
          + jnp.dot(nr, wo_ref[D_ATTN:, :], preferred_element_type=F32))
    x1_ref[...] = x1
    h2 = _rms(x1, gf_ref[...])
    h2_ref[...] = h2.astype(BF16)

    hh, hl = _split_bf16(h2)
    logits = (jnp.dot(hh, wrh_ref[...], preferred_element_type=F32)
              + jnp.dot(hl, wrh_ref[...], preferred_element_type=F32)
              + jnp.dot(hh, wrl_ref[...], preferred_element_type=F32)) + br_ref[...]
    tm = logits.shape[0]
    lane = lax.broadcasted_iota(I32, (tm, LANES), 1)
    lanef = lane.astype(F32)
    big = float(LANES)

    def top(vals):
        m = jnp.max(vals, axis=-1, keepdims=True)
        return m, jnp.min(jnp.where(vals == m, lanef, big), axis=-1, keepdims=True)

    is_group = lane < N_GROUPS
    gmax, gsel = top(jnp.where(is_group, logits, NEG_INF))
    g_w = 1.0 / jnp.sum(jnp.where(is_group, jnp.exp(logits - gmax), 0.0), axis=-1, keepdims=True)
    lo_lane = ROUTER_LANE0 + EXPERTS_PER_GROUP * gsel
    in_group = jnp.logical_and(lanef >= lo_lane, lanef < lo_lane + EXPERTS_PER_GROUP)
    el = jnp.where(in_group, logits, NEG_INF)
    v1, i1 = top(el)
    v2, i2 = top(jnp.where(lanef == i1, NEG_INF, el))
    t = jnp.exp(v2 - v1)
    p1 = 1.0 / (1.0 + t)
    gate1 = g_w * p1
    gate2 = g_w * (t * p1)

    oh1 = (lanef == i1).astype(F32)
    oh2 = (lanef == i2).astype(F32)
    cnt = oh1 + oh2
    prefix = jnp.dot(tri_ref[...], cnt.astype(BF16), preferred_element_type=F32)
    total = jnp.sum(cnt, axis=0, keepdims=True)
    chunks = jnp.floor((total + (ROW_CHUNK - 1)) * (1.0 / ROW_CHUNK))
    seg_start = ROW_CHUNK * jnp.dot(jnp.broadcast_to(chunks, (8, LANES)).astype(BF16), upper_ref[...],
                                    preferred_element_type=F32)[0:1, :]
    base = seg_start + prefix
    pos1 = jnp.sum(oh1 * base, axis=-1, keepdims=True)
    pos2 = jnp.sum(oh2 * base, axis=-1, keepdims=True)
    cnt_ref[...] = jnp.broadcast_to(total, cnt_ref.shape).astype(I32)
    pos_ref[...] = jnp.where(lane == 0, pos1, jnp.where(lane == 1, pos2, 0.0))
    gate_ref[...] = jnp.where(lane == 0, gate1, jnp.where(lane == 1, gate2, 0.0))


def _mix(x2, o, lse, rnn, ga, gn, wo_bf, gf, wr_hi, wr_lo, br, tri, upper):
    m = x2.shape[0]
    tm = TOKEN_TILE
    row = lambda i: (i, 0)
    fix = lambda i: (0, 0)
    t512 = pl.BlockSpec((tm, D_ATTN), row)
    t128 = pl.BlockSpec((tm, LANES), row)
    t1024 = pl.BlockSpec((tm, D_MODEL), row)
    slab = pl.BlockSpec((HEAD_PAIRS, tm, LANES), lambda i: (0, i, 0))
    return pl.pallas_call(
        _mix_kernel,
        grid=(m // tm,),
        in_specs=[t1024, slab, slab, slab, slab, slab, slab, t512,
                  pl.BlockSpec((1, D_ATTN), fix), pl.BlockSpec((1, D_RNN), fix),
                  pl.BlockSpec((D_MIX, D_MODEL), fix), pl.BlockSpec((1, D_MODEL), fix),
                  pl.BlockSpec((D_MODEL, LANES), fix), pl.BlockSpec((D_MODEL, LANES), fix),
                  pl.BlockSpec((1, LANES), fix),
                  pl.BlockSpec((tm, tm), fix), pl.BlockSpec((LANES, LANES), fix)],
        out_specs=[t1024, t1024, t128, t128, pl.BlockSpec((8, LANES), row)],
        out_shape=[jax.ShapeDtypeStruct((m, D_MODEL), F32), jax.ShapeDtypeStruct((m, D_MODEL), BF16),
                   jax.ShapeDtypeStruct((m, LANES), F32), jax.ShapeDtypeStruct((m, LANES), F32),
                   jax.ShapeDtypeStruct((m // tm * 8, LANES), I32)],
        compiler_params=_cparams("parallel"),
    )(x2, o[0], o[1], o[2], lse[0], lse[1], lse[2], rnn, ga, gn, wo_bf, gf, wr_hi, wr_lo, br, tri, upper)


def _chunk_copy(src, src_row, dst, dst_row, sem):
    return pltpu.make_async_copy(src.at[pl.ds(src_row, ROW_CHUNK)], dst.at[pl.ds(dst_row, ROW_CHUNK)], sem)


def _for_each_chunk(i, nch_ref, lstart_ref, segdst_ref, copy):
    def per_expert(e, total):
        n = nch_ref[i * N_EXPERTS + e]
        loc0 = lstart_ref[i * N_EXPERTS + e]
        dst0 = segdst_ref[i * N_EXPERTS + e]

        def per_chunk(c, carry):
            off = c * ROW_CHUNK
            copy(pl.multiple_of(loc0 + off, ROW_CHUNK), pl.multiple_of(dst0 + off, ROW_CHUNK))
            return carry

        lax.fori_loop(0, n, per_chunk, 0)
        return total + n

    return lax.fori_loop(0, N_EXPERTS, per_expert, jnp.int32(0))


def _pack_bf16_pairs(x):
    c = x.shape[1] // 2
    lo = lax.bitcast_convert_type(x[:, :c], U32) >> 16
    hi = lax.bitcast_convert_type(x[:, c:], U32) & jnp.uint32(0xFFFF0000)
    return hi | lo


def _unpack_bf16_pairs(w):
    lo = lax.bitcast_convert_type(w << 16, F32).astype(BF16)
    hi = lax.bitcast_convert_type(w & jnp.uint32(0xFFFF0000), F32).astype(BF16)
    return lo, hi


def _dispatch_kernel(nch_ref, lstart_ref, segdst_ref, tailn_ref, taildst_ref,
                     h_ref, pos_ref, gate_ref, rows_ref, sorted_buf, zero_buf, pending, sems, *, tm):
    i = pl.program_id(0)
    slot = i % 2
    pos_t = pos_ref[...].T
    gate_t = gate_ref[...].T
    row = lax.broadcasted_iota(I32, (TILE_ROWS, tm), 0).astype(F32)
    hit1 = row == pos_t[0:1, :]
    hit2 = row == pos_t[1:2, :]
    onehot = jnp.where(jnp.logical_or(hit1, hit2), 1.0, 0.0).astype(BF16)
    feat = jnp.dot(onehot, h_ref[...], preferred_element_type=F32)
    sorted_buf[slot, :, 0:ROW_WORDS - LANES] = _pack_bf16_pairs(feat)
    gate = jnp.sum(jnp.where(hit1, gate_t[0:1, :], 0.0) + jnp.where(hit2, gate_t[1:2, :], 0.0),
                   axis=-1, keepdims=True)
    lane = lax.broadcasted_iota(I32, (TILE_ROWS, LANES), 1)
    sorted_buf[slot, :, ROW_WORDS - LANES:] = jnp.where(lane == 0, lax.bitcast_convert_type(gate, U32), jnp.uint32(0))

    def drain(count, sem):
        def body(c, carry):
            _chunk_copy(sorted_buf.at[0], 0, rows_ref, 0, sem).wait()
            return carry

        lax.fori_loop(0, count, body, 0)

    @pl.when(i > 0)
    def _():
        drain(pending[0], sems.at[1 - slot])

    def send(loc, dst):
        _chunk_copy(sorted_buf.at[slot], loc, rows_ref, dst, sems.at[slot]).start()

    n_sent = _for_each_chunk(i, nch_ref, lstart_ref, segdst_ref, send)

    @pl.when(i == 0)
    def _():
        zero_buf[...] = jnp.zeros_like(zero_buf)

    def tails():
        def per_expert(e, total):
            n = tailn_ref[e]
            dst0 = taildst_ref[e]

            def per_chunk(c, carry):
                _chunk_copy(zero_buf, 0, rows_ref, pl.multiple_of(dst0 + c * ROW_CHUNK, ROW_CHUNK),
                            sems.at[slot]).start()
                return carry

            lax.fori_loop(0, n, per_chunk, 0)
            return total + n

        return lax.fori_loop(0, N_EXPERTS + 1, per_expert, jnp.int32(0))

    n_sent = n_sent + lax.cond(i == 0, tails, lambda: jnp.int32(0))
    pending[0] = n_sent

    @pl.when(i == pl.num_programs(0) - 1)
    def _():
        drain(n_sent, sems.at[slot])


def _dispatch(meta, h2, pos, gates, n_rows):
    m = h2.shape[0]
    tm = TOKEN_TILE
    tile = lambda i, *_: (i, 0)
    return pl.pallas_call(
        functools.partial(_dispatch_kernel, tm=tm),
        grid_spec=pltpu.PrefetchScalarGridSpec(
            num_scalar_prefetch=5,
            grid=(m // tm,),
            in_specs=[pl.BlockSpec((tm, D_MODEL), tile), pl.BlockSpec((tm, LANES), tile),
                      pl.BlockSpec((tm, LANES), tile)],
            out_specs=pl.BlockSpec(memory_space=pl.ANY),
            scratch_shapes=[pltpu.VMEM((2, TILE_ROWS, ROW_WORDS), U32), pltpu.VMEM((ROW_CHUNK, ROW_WORDS), U32),
                            pltpu.SMEM((1,), I32), pltpu.SemaphoreType.DMA((2,))],
        ),
        out_shape=jax.ShapeDtypeStruct((n_rows, ROW_WORDS), U32),
        compiler_params=_cparams("arbitrary"),
    )(*meta, h2, pos, gates)


def _expert_kernel(be_ref, nused_ref, next_ref, rows_ref, wg_hbm, wu_hbm, wd_hbm, out_ref,
                   wgf, wuf, wdf, wgb, wub, wdb, holder, sems, *, layer):
    i = pl.program_id(0)
    used = i < nused_ref[0]
    expert = be_ref[i]
    first = jnp.logical_or(i == 0, expert != be_ref[jnp.maximum(i - 1, 0)])

    def weight_copies(e, slot):
        return (pltpu.make_async_copy(wg_hbm.at[layer, e], wgf.at[slot], sems.at[slot, 0]),
                pltpu.make_async_copy(wu_hbm.at[layer, e], wuf.at[slot], sems.at[slot, 1]),
                pltpu.make_async_copy(wd_hbm.at[layer, e], wdf.at[slot], sems.at[slot, 2]))

    @pl.when(jnp.logical_and(used, first))
    def _():
        @pl.when(i == 0)
        def _():
            holder[0] = 0
            for c in weight_copies(expert, 0):
                c.start()

        slot = holder[0]
        for c in weight_copies(expert, slot):
            c.wait()
        wgb[...] = wgf[slot].astype(BF16)
        wub[...] = wuf[slot].astype(BF16)
        wdb[...] = wdf[slot].astype(BF16)
        nxt = next_ref[i]

        @pl.when(nxt >= 0)
        def _():
            for c in weight_copies(nxt, 1 - slot):
                c.start()

        holder[0] = 1 - slot

    @pl.when(used)
    def _():
        xb = jnp.concatenate(_unpack_bf16_pairs(rows_ref[:, 0:ROW_WORDS - LANES]), axis=1)
        gate = lax.bitcast_convert_type(rows_ref[:, ROW_WORDS - LANES:][:, 0:1], F32)
        g = jnp.dot(xb, wgb[...], preferred_element_type=F32)
        u = jnp.dot(xb, wub[...], preferred_element_type=F32)
        act = (g * jax.nn.sigmoid(g) * u).astype(BF16)
        y = jnp.dot(act, wdb[...], preferred_element_type=F32) * gate
        out_ref[...] = _pack_bf16_pairs(y.astype(BF16).astype(F32))

    @pl.when(jnp.logical_not(used))
    def _():
        out_ref[...] = jnp.zeros_like(out_ref)


def _experts(block_expert, n_used, next_expert, rows, w_gate, w_up, w_down, layer):
    n_rows = rows.shape[0]
    blk = EXPERT_BLOCK
    rmap = lambda i, be, nu, nx: (jnp.minimum(i, nu[0] - 1), 0)
    hbm = pl.BlockSpec(memory_space=pl.ANY)
    return pl.pallas_call(
        functools.partial(_expert_kernel, layer=layer),
        grid_spec=pltpu.PrefetchScalarGridSpec(
            num_scalar_prefetch=3,
            grid=(n_rows // blk,),
            in_specs=[pl.BlockSpec((blk, ROW_WORDS), rmap), hbm, hbm, hbm],
            out_specs=pl.BlockSpec((blk, D_MODEL // 2), lambda i, be, nu, nx: (i, 0)),
            scratch_shapes=[pltpu.VMEM((2, D_MODEL, D_EXPERT), F32), pltpu.VMEM((2, D_MODEL, D_EXPERT), F32),
                            pltpu.VMEM((2, D_EXPERT, D_MODEL), F32),
                            pltpu.VMEM((D_MODEL, D_EXPERT), BF16), pltpu.VMEM((D_MODEL, D_EXPERT), BF16),
                            pltpu.VMEM((D_EXPERT, D_MODEL), BF16),
                            pltpu.SMEM((1,), I32), pltpu.SemaphoreType.DMA((2, 3))],
        ),
        out_shape=jax.ShapeDtypeStruct((n_rows, D_MODEL // 2), U32),
        compiler_params=_cparams("arbitrary"),
    )(block_expert, n_used, next_expert, rows, w_gate, w_up, w_down)


def _combine_kernel(nch_ref, lstart_ref, segdst_ref, x1_ref, pos_ref, rows_ref, out_ref, local, pending, sems, *, tm):
    i = pl.program_id(0)
    slot = i % 2

    def fetch_tile(t, s):
        def fetch(loc, src):
            _chunk_copy(rows_ref, src, local.at[s], loc, sems.at[s]).start()

        return _for_each_chunk(t, nch_ref, lstart_ref, segdst_ref, fetch)

    @pl.when(i == 0)
    def _():
        local[...] = jnp.zeros_like(local)
        pending[0] = fetch_tile(0, 0)

    @pl.when(i + 1 < pl.num_programs(0))
    def _():
        pending[1 - slot] = fetch_tile(i + 1, 1 - slot)

    def drain(c, carry):
        _chunk_copy(rows_ref, 0, local.at[0], 0, sems.at[slot]).wait()
        return carry

    lax.fori_loop(0, pending[slot], drain, 0)

    pos = pos_ref[...]
    col = lax.broadcasted_iota(I32, (tm, TILE_ROWS), 1).astype(F32)
    pick = jnp.where(jnp.logical_or(col == pos[:, 0:1], col == pos[:, 1:2]), 1.0, 0.0).astype(BF16)
    lo, hi = _unpack_bf16_pairs(local[slot])
    moe = jnp.concatenate([jnp.dot(pick, lo, preferred_element_type=F32),
                           jnp.dot(pick, hi, preferred_element_type=F32)], axis=1)
    out_ref[...] = x1_ref[...] + moe


def _combine(meta, x1, pos, out_rows):
    m = x1.shape[0]
    tm = TOKEN_TILE
    tile = lambda i, *_: (i, 0)
    return pl.pallas_call(
        functools.partial(_combine_kernel, tm=tm),
        grid_spec=pltpu.PrefetchScalarGridSpec(
            num_scalar_prefetch=3,
            grid=(m // tm,),
            in_specs=[pl.BlockSpec((tm, D_MODEL), tile), pl.BlockSpec((tm, LANES), tile),
                      pl.BlockSpec(memory_space=pl.ANY)],
            out_specs=pl.BlockSpec((tm, D_MODEL), tile),
            scratch_shapes=[pltpu.VMEM((2, TILE_ROWS, D_MODEL // 2), U32), pltpu.SMEM((2,), I32),
                            pltpu.SemaphoreType.DMA((2,))],
        ),
        out_shape=jax.ShapeDtypeStruct((m, D_MODEL), F32),
        compiler_params=_cparams("arbitrary"),
    )(*meta, x1, pos, out_rows)


def _dispatch_plan(cnt_tiles, n_tiles, n_blocks):
    cnt = cnt_tiles.reshape(n_tiles, 8, LANES)[:, 0, ROUTER_LANE0:ROUTER_LANE0 + N_EXPERTS]
    seg = (cnt + ROW_CHUNK - 1) // ROW_CHUNK * ROW_CHUNK
    used = jnp.sum(seg, axis=0)
    padded = (used + EXPERT_BLOCK - 1) // EXPERT_BLOCK * EXPERT_BLOCK
    p_ends = jnp.cumsum(padded)
    p_starts = p_ends - padded
    seg_dst = p_starts[None, :] + jnp.cumsum(seg, axis=0) - seg
    loc_start = jnp.cumsum(seg, axis=1) - seg
    flat = lambda a: a.reshape(-1).astype(I32)
    block_row = jnp.arange(n_blocks, dtype=I32) * EXPERT_BLOCK
    block_expert = jnp.minimum(jnp.sum((p_ends[None, :] <= block_row[:, None]).astype(I32), axis=1), N_EXPERTS - 1)
    n_used = (p_ends[-1:] // EXPERT_BLOCK).astype(I32)
    after = (p_ends // EXPERT_BLOCK).astype(I32)[block_expert]
    next_expert = jnp.where(after < n_used[0], block_expert[jnp.minimum(after, n_blocks - 1)], -1).astype(I32)
    chunk_meta = (flat(seg // ROW_CHUNK), flat(loc_start), flat(seg_dst))
    rest = n_blocks * EXPERT_BLOCK - p_ends[-1:]
    tail_meta = (flat(jnp.concatenate([padded - used, rest]) // ROW_CHUNK),
                 flat(jnp.concatenate([p_starts + used, p_ends[-1:]])))
    return chunk_meta, tail_meta, (block_expert.astype(I32), n_used, next_expert)


def _constants():
    gm = np.kron(np.eye(ATTN_HEADS), np.full((HEAD_DIM, HEAD_DIM), 1.0 / HEAD_DIM))
    tri = np.tril(np.ones((TOKEN_TILE, TOKEN_TILE), np.float32), -1)
    upper = np.triu(np.ones((LANES, LANES), np.float32), 1)
    return jnp.asarray(gm, BF16), jnp.asarray(tri, BF16), jnp.asarray(upper, BF16)


def kernel(x, rel_bias_table, norm_mix, w_in, q_norm, k_norm, conv_w, conv_b, rg_w_a, rg_b_a, rg_w_x, rg_b_x,
           rg_lambda, norm_attn_out, norm_rnn_out, w_out, norm_ffn, router_group_w, router_group_b,
           router_expert_w, router_expert_b, expert_w_gate, expert_w_up, expert_w_down):
    b, s, d = x.shape
    depth = w_in.shape[0]
    m = b * s
    assert d == D_MODEL and s % ATTN_TILE == 0 and s % RNN_TILE == 0 and m % TOKEN_TILE == 0

    gm, tri, upper = _constants()
    bias = _bias_tables(rel_bias_table)
    scale = HEAD_DIM ** -0.5
    n_tiles = m // TOKEN_TILE
    n_blocks = -(-(m * TOP_K + n_tiles * N_EXPERTS * (ROW_CHUNK - 1)) // EXPERT_BLOCK) + N_EXPERTS
    n_rows = n_blocks * EXPERT_BLOCK
    row1 = lambda v: v.reshape(1, -1).astype(F32)

    x2 = x.reshape(m, d).astype(F32)
    for l in range(depth):
        q, k, v, xr, gr = _proj(x2, row1(norm_mix[l]), w_in[l].astype(BF16),
                                row1(jnp.tile(q_norm[l], ATTN_HEADS) * (scale * LOG2E)),
                                row1(jnp.tile(k_norm[l], ATTN_HEADS)), gm)
        outs, lses = [], []
        for n, (_, dil) in enumerate(DILATED_BRANCHES):
            o, lse = _attention_branch(q, k, v, bias[n], b, s, dil)
            outs.append(o)
            lses.append(lse)
        rnn = _rglru(xr, gr, conv_w[l].astype(F32), row1(conv_b[l]),
                     _block_diag(rg_w_a[l]).astype(BF16), row1(rg_b_a[l]),
                     _block_diag(rg_w_x[l]).astype(BF16), row1(rg_b_x[l]), row1(rg_lambda[l]), b, s)

        wr = jnp.zeros((D_MODEL, LANES), F32)
        wr = wr.at[:, :N_GROUPS].set(router_group_w[l]).at[:, N_GROUPS:N_GROUPS + N_EXPERTS].set(router_expert_w[l])
        wr_hi, wr_lo = _split_bf16(wr)
        br = jnp.zeros((1, LANES), F32)
        br = br.at[0, :N_GROUPS].set(router_group_b[l]).at[0, N_GROUPS:N_GROUPS + N_EXPERTS].set(router_expert_b[l])
        x1, h2, pos, gates, cnt = _mix(x2, outs, lses, rnn, row1(norm_attn_out[l]), row1(norm_rnn_out[l]),
                                       w_out[l].astype(BF16), row1(norm_ffn[l]), wr_hi, wr_lo, br, tri, upper)

        chunk_meta, tail_meta, block_meta = _dispatch_plan(cnt, n_tiles, n_blocks)
        rows = _dispatch(chunk_meta + tail_meta, h2, pos, gates, n_rows)
        out_rows = _experts(*block_meta, rows, expert_w_gate, expert_w_up, expert_w_down, l)
        x2 = _combine(chunk_meta, x1, pos, out_rows)
    return x2.reshape(b, s, d).astype(x.dtype)
```

```python
import functools
import math

import numpy as np
import jax
import jax.numpy as jnp
from jax import lax
from jax.experimental import pallas as pl
from jax.experimental.pallas import tpu as pltpu

F32 = jnp.float32
BF16 = jnp.bfloat16
I32 = jnp.int32
U32 = jnp.uint32

D_MODEL = 1024
ATTN_HEADS = 8
HEAD_DIM = 64
D_ATTN = ATTN_HEADS * HEAD_DIM
RNN_BLOCKS = 8
D_RNN = 512
D_MIX = D_ATTN + D_RNN
D_IN = 3 * D_ATTN + 2 * D_RNN
DILATED_BRANCHES = ((128, 1), (512, 4), (2048, 16))
Q_BLOCK = 128
REL_BUCKETS = 32
REL_MAX_DIST = 2048
CONV_WIDTH = 4
RG_C = 8.0
N_GROUPS = 4
EXPERTS_PER_GROUP = 8
N_EXPERTS = N_GROUPS * EXPERTS_PER_GROUP
TOP_K = 2
D_EXPERT = 512
EPS = 1e-6
NEG_INF = -1e30
LOG2E = math.log2(math.e)
LN2 = math.log(2.0)

LANES = 128
HEAD_PAIRS = D_ATTN // LANES
ROUTER_LANE0 = N_GROUPS
TOKEN_TILE = 512
ATTN_TILE = 2048
ATTN_UNROLL = 16
RNN_TILE = 512
RNN_CHUNK = 64
EXPERT_BLOCK = 512
ROW_CHUNK = 8
TILE_ROWS = -(-(TOKEN_TILE * TOP_K + N_EXPERTS * (ROW_CHUNK - 1)) // 256) * 256
TILE_CHUNKS = TILE_ROWS // ROW_CHUNK
WAIT_BATCH = 16
ROW_WORDS = D_MODEL // 2 + LANES
VMEM_LIMIT = 48 * 1024 * 1024


def _cparams(*sem):
    return pltpu.CompilerParams(dimension_semantics=sem, vmem_limit_bytes=VMEM_LIMIT)


def _rms(x, gain):
    return x * lax.rsqrt(jnp.mean(x * x, axis=-1, keepdims=True) + EPS) * gain


def _proj_kernel(x_ref, g_ref, w_ref, qg_ref, kg_ref, gm_ref, q_ref, k_ref, v_ref, xr_ref, gr_ref):
    hb = _rms(x_ref[...], g_ref[...]).astype(BF16)

    def sec(n):
        return jnp.dot(hb, w_ref[:, n * D_ATTN:(n + 1) * D_ATTN], preferred_element_type=F32)

    def head_norm(z, gain):
        ms = jnp.dot((z * z).astype(BF16), gm_ref[...], preferred_element_type=F32)
        return z * lax.rsqrt(ms + EPS) * gain

    def put_slabs(ref, z):
        for hp in range(HEAD_PAIRS):
            ref[hp] = z[:, hp * LANES:(hp + 1) * LANES]

    put_slabs(q_ref, head_norm(sec(0), qg_ref[...]))
    put_slabs(k_ref, head_norm(sec(1), kg_ref[...]))
    put_slabs(v_ref, sec(2))
    xr_ref[...] = sec(3)
    gr_ref[...] = sec(4)


def _proj(x2, gain, w_in_bf, qg, kg, gm):
    m = x2.shape[0]
    tm = TOKEN_TILE
    row = lambda i: (i, 0)
    fix = lambda i: (0, 0)
    return pl.pallas_call(
        _proj_kernel,
        grid=(m // tm,),
        in_specs=[
            pl.BlockSpec((tm, D_MODEL), row),
            pl.BlockSpec((1, D_MODEL), fix),
            pl.BlockSpec((D_MODEL, D_IN), fix),
            pl.BlockSpec((1, D_ATTN), fix),
            pl.BlockSpec((1, D_ATTN), fix),
            pl.BlockSpec((D_ATTN, D_ATTN), fix),
        ],
        out_specs=[pl.BlockSpec((HEAD_PAIRS, tm, LANES), lambda i: (0, i, 0))] * 3
        + [pl.BlockSpec((tm, D_RNN), row)] * 2,
        out_shape=[jax.ShapeDtypeStruct((HEAD_PAIRS, m, LANES), F32)] * 3
        + [jax.ShapeDtypeStruct((m, D_RNN), F32)] * 2,
        compiler_params=_cparams("parallel"),
    )(x2, gain, w_in_bf, qg, kg, gm)


def _rows(start, size, stride):
    return pl.ds(start, size) if stride == 1 else pl.ds(start, size, stride=stride)


def _attn_kernel(q_ref, k_ref, v_ref, bias_ref, o_ref, lse_ref, qd, kd, vd, stage, *, dil):
    i = pl.program_id(2)
    tile = q_ref.shape[0]
    rows = tile // dil
    seg = Q_BLOCK + rows

    @pl.when(i == 0)
    def _():
        for r in range(dil):
            kd[r * seg:r * seg + Q_BLOCK, :] = jnp.zeros((Q_BLOCK, LANES), BF16)
            vd[r * seg:r * seg + Q_BLOCK, :] = jnp.zeros((Q_BLOCK, LANES), BF16)

    @pl.when(i > 0)
    def _():
        for r in range(dil):
            kd[r * seg:r * seg + Q_BLOCK, :] = kd[r * seg + rows:(r + 1) * seg, :]
            vd[r * seg:r * seg + Q_BLOCK, :] = vd[r * seg + rows:(r + 1) * seg, :]

    def regroup(src_ref, dst, base, pitch):
        if dil % 16 == 0:
            quarter = tile // 4
            for c in range(4):
                stage[c * quarter:(c + 1) * quarter, :] = src_ref[pl.ds(c, quarter, stride=4), :]
            for r in range(dil):
                picked = stage[_rows((r % 4) * quarter + r // 4, rows, dil // 4), :]
                dst[base + r * pitch:base + r * pitch + rows, :] = picked.astype(BF16)
        else:
            for r in range(dil):
                dst[base + r * pitch:base + r * pitch + rows, :] = src_ref[_rows(r, rows, dil), :].astype(BF16)

    regroup(q_ref, qd, 0, rows)
    regroup(k_ref, kd, Q_BLOCK, seg)
    regroup(v_ref, vd, Q_BLOCK, seg)

    per = rows // Q_BLOCK
    lane = lax.broadcasted_iota(I32, (Q_BLOCK, LANES), 1)
    low = lane < HEAD_DIM

    def block(blk, carry):
        r = blk // per
        n = blk % per
        q0 = pl.multiple_of(r * rows + n * Q_BLOCK, Q_BLOCK)
        k0 = pl.multiple_of(r * seg + n * Q_BLOCK, Q_BLOCK)
        sel = jnp.where(jnp.logical_and(i == 0, n == 0), 1, 0)
        q = qd[pl.ds(q0, Q_BLOCK), :]
        kk = kd[pl.ds(k0, 2 * Q_BLOCK), :]
        vv = vd[pl.ds(k0, 2 * Q_BLOCK), :]
        pvs, maxes, sums = [], [], []
        for hh in range(2):
            qm = jnp.where(low if hh == 0 else jnp.logical_not(low), q, jnp.zeros_like(q))
            s = lax.dot_general(qm, kk, (((1,), (1,)), ((), ())), preferred_element_type=F32)
            s = s + bias_ref[sel, hh]
            mx = jnp.max(s, axis=-1, keepdims=True)
            p = jnp.exp2(s - mx)
            sums.append(jnp.sum(p, axis=-1, keepdims=True))
            maxes.append(mx)
            pvs.append(jnp.dot(p.astype(BF16), vv, preferred_element_type=F32))
        l = jnp.where(low, sums[0], sums[1])
        dst = _rows(r + dil * Q_BLOCK * n, Q_BLOCK, dil)
        o_ref[dst, :] = jnp.where(low, pvs[0], pvs[1]) * (1.0 / l)
        lse_ref[dst, :] = jnp.where(low, maxes[0], maxes[1]) * LN2 + jnp.log(l)
        return carry

    lax.fori_loop(0, dil * per, block, 0, unroll=ATTN_UNROLL)


def _attention_branch(q, k, v, bias, b, s, dil):
    tile = ATTN_TILE
    nt = s // tile
    blk = pl.BlockSpec((None, tile, LANES), lambda bb, hp, i: (hp, bb * nt + i, 0))
    shape = jax.ShapeDtypeStruct(q.shape, F32)
    keys = tile + Q_BLOCK * dil
    return pl.pallas_call(
        functools.partial(_attn_kernel, dil=dil),
        grid=(b, HEAD_PAIRS, nt),
        in_specs=[blk, blk, blk,
                  pl.BlockSpec((2, 2, Q_BLOCK, 2 * Q_BLOCK), lambda bb, hp, i: (0, hp, 0, 0))],
        out_specs=[blk, blk],
        out_shape=[shape, shape],
        scratch_shapes=[pltpu.VMEM((tile, LANES), BF16), pltpu.VMEM((keys, LANES), BF16),
                        pltpu.VMEM((keys, LANES), BF16), pltpu.VMEM((tile, LANES), F32)],
        compiler_params=_cparams("parallel", "parallel", "arbitrary"),
    )(q, k, v, bias)


def _bias_step_tables():
    exact = REL_BUCKETS // 2
    i = np.arange(Q_BLOCK)[:, None]
    j = np.arange(2 * Q_BLOCK)[None, :]
    steps = i + Q_BLOCK - j
    onehot, band = [], []
    for window, dil in DILATED_BRANCHES:
        dist = (np.arange(2 * Q_BLOCK) * dil).astype(np.int32)
        d = np.maximum(dist, 1).astype(np.float32)
        log_b = exact + (np.log(d / np.float32(exact)) / np.float32(math.log(REL_MAX_DIST / exact))
                         * np.float32(REL_BUCKETS - exact)).astype(np.int32)
        bucket = np.where(dist < exact, dist, np.minimum(log_b, REL_BUCKETS - 1))
        onehot.append(np.eye(REL_BUCKETS, dtype=np.float32)[bucket])
        band.append((steps >= 0) & (steps <= window // dil))
    first = np.broadcast_to(j >= Q_BLOCK, (Q_BLOCK, 2 * Q_BLOCK))
    return np.stack(onehot), np.stack(band), first


def _bias_tables(rel_table):
    onehot, band, first = _bias_step_tables()
    nb = len(DILATED_BRANCHES)
    vec = LOG2E * jnp.einsum('nsb,bh->nhs', onehot, rel_table.astype(F32), precision=lax.Precision.HIGHEST)
    width = 3 * Q_BLOCK
    ext = jnp.concatenate([jnp.zeros((nb, ATTN_HEADS, Q_BLOCK), F32), vec[..., ::-1]], axis=-1)
    hank = jnp.tile(ext, (1, 1, Q_BLOCK + 1))[..., :Q_BLOCK * (width + 1)].reshape(nb, ATTN_HEADS, Q_BLOCK, width + 1)
    bias = hank[:, :, ::-1, Q_BLOCK:width]
    regular = jnp.where(band[:, None], bias, NEG_INF)
    start = jnp.where((band & first[None])[:, None], bias, NEG_INF)
    return jnp.stack([regular, start], axis=1)


def _rglru_kernel(xr_ref, gr_ref, cw_ref, cb_ref, wa_ref, ba_ref, wx_ref, bx_ref, lam_ref, out_ref,
                  xbuf, a_scr, b_scr, h_scr, *, tt):
    t = pl.program_id(1)

    @pl.when(t == 0)
    def _():
        xbuf[0:8, :] = jnp.zeros((8, D_RNN), F32)
        h_scr[...] = jnp.zeros_like(h_scr)

    xbuf[8:8 + tt, :] = xr_ref[...]
    u = cb_ref[...] + cw_ref[3:4, :] * xbuf[8:8 + tt, :]
    for back in range(1, CONV_WIDTH):
        u = u + cw_ref[3 - back:4 - back, :] * xbuf[8 - back:8 - back + tt, :]
    xbuf[0:8, :] = xbuf[tt:tt + 8, :]

    ub = u.astype(BF16)
    r = jax.nn.sigmoid(jnp.dot(ub, wa_ref[...], preferred_element_type=F32) + ba_ref[...])
    gi = jax.nn.sigmoid(jnp.dot(ub, wx_ref[...], preferred_element_type=F32) + bx_ref[...])
    nl = -lam_ref[...]
    softplus = jnp.maximum(nl, 0.0) + jnp.log1p(jnp.exp(-jnp.abs(nl)))
    log_a = (-RG_C) * r * softplus
    a = jnp.exp(log_a)
    a_scr[...] = a
    b_scr[...] = jnp.sqrt(-jnp.tanh(log_a) * (a * a + 1.0)) * (gi * u)

    rc = RNN_CHUNK
    row = lax.broadcasted_iota(I32, (rc, LANES), 0)

    def chunk(c, carry):
        r0 = pl.multiple_of(c * rc, rc)
        for g in range(D_RNN // LANES):
            ls = slice(g * LANES, (g + 1) * LANES)
            aa = a_scr[pl.ds(r0, rc), ls]
            bb = b_scr[pl.ds(r0, rc), ls]
            k = 1
            while k < rc:
                keep = row >= k
                a_sh = pltpu.roll(aa, k, 0)
                b_sh = pltpu.roll(bb, k, 0)
                bb = jnp.where(keep, aa * b_sh + bb, bb)
                aa = jnp.where(keep, aa * a_sh, aa)
                k *= 2
            h = aa * h_scr[0:1, ls] + bb
            h_scr[0:1, ls] = h[rc - 1:rc, :]
            out_ref[pl.ds(r0, rc), ls] = h * jax.nn.gelu(gr_ref[pl.ds(r0, rc), ls], approximate=True)
        return carry

    lax.fori_loop(0, tt // rc, chunk, 0)


def _rglru(xr, gr, cw, cb, wa, ba, wx, bx, lam, b, s):
    tt = RNN_TILE
    nt = s // tt
    row = lambda bb, t: (bb * nt + t, 0)
    fix = lambda bb, t: (0, 0)
    vec = pl.BlockSpec((1, D_RNN), fix)
    mat = pl.BlockSpec((D_RNN, D_RNN), fix)
    return pl.pallas_call(
        functools.partial(_rglru_kernel, tt=tt),
        grid=(b, nt),
        in_specs=[pl.BlockSpec((tt, D_RNN), row), pl.BlockSpec((tt, D_RNN), row),
                  pl.BlockSpec((CONV_WIDTH, D_RNN), fix), vec, mat, vec, mat, vec, vec],
        out_specs=pl.BlockSpec((tt, D_RNN), row),
        out_shape=jax.ShapeDtypeStruct((b * s, D_RNN), F32),
        scratch_shapes=[pltpu.VMEM((tt + 8, D_RNN), F32), pltpu.VMEM((tt, D_RNN), F32),
                        pltpu.VMEM((tt, D_RNN), F32), pltpu.VMEM((8, D_RNN), F32)],
        compiler_params=_cparams("parallel", "arbitrary"),
    )(xr, gr, cw, cb, wa, ba, wx, bx, lam)


def _block_diag(w):
    nb, n, _ = w.shape
    eye = jnp.eye(nb, dtype=w.dtype)
    return (eye[:, None, :, None] * w[:, :, None, :]).reshape(nb * n, nb * n)


def _split_bf16(x):
    hi = x.astype(BF16)
    return hi, (x - hi.astype(F32)).astype(BF16)


def _mix_kernel(x_ref, o1_ref, o2_ref, o3_ref, l1_ref, l2_ref, l3_ref, rnn_ref, ga_ref, gn_ref, wo_ref, gf_ref,
                wr_ref, br_ref, tri_ref, upper_ref,
                x1_ref, h2_ref, pos_ref, gate_ref, cnt_ref):
    slabs, sumsq = [], 0.0
    for hp in range(HEAD_PAIRS):
        l1, l2, l3 = l1_ref[hp], l2_ref[hp], l3_ref[hp]
        mx = jnp.maximum(jnp.maximum(l1, l2), l3)
        e1, e2, e3 = jnp.exp(l1 - mx), jnp.exp(l2 - mx), jnp.exp(l3 - mx)
        a = (e1 * o1_ref[hp] + e2 * o2_ref[hp] + e3 * o3_ref[hp]) * (1.0 / (e1 + e2 + e3))
        slabs.append(a)
        sumsq = sumsq + jnp.sum(a * a, axis=-1, keepdims=True)
    scale = lax.rsqrt(sumsq * (1.0 / D_ATTN) + EPS)
    na = (jnp.concatenate([a * scale for a in slabs], axis=1) * ga_ref[...]).astype(BF16)
    nr = _rms(rnn_ref[...], gn_ref[...]).astype(BF16)
    x1 = (x_ref[...] + jnp.dot(na, wo_ref[0:D_ATTN, :], preferred_element_type=F32)
          + jnp.dot(nr, wo_ref[D_ATTN:, :], preferred_element_type=F32))
    x1_ref[...] = x1
    h2 = _rms(x1, gf_ref[...])
    h2_ref[...] = h2.astype(BF16)

    hh, hl = _split_bf16(h2)
    by_hi = jnp.dot(hh, wr_ref[...], preferred_element_type=F32)
    by_lo = jnp.dot(hl, wr_ref[...], preferred_element_type=F32)
    logits = by_hi[:, :LANES] + by_hi[:, LANES:] + by_lo[:, :LANES] + br_ref[...]
    tm = logits.shape[0]
    lane = lax.broadcasted_iota(I32, (tm, LANES), 1)
    lanef = lane.astype(F32)
    big = float(LANES)

    def top(vals):
        m = jnp.max(vals, axis=-1, keepdims=True)
        return m, jnp.min(jnp.where(vals == m, lanef, big), axis=-1, keepdims=True)

    is_group = lane < N_GROUPS
    gmax, gsel = top(jnp.where(is_group, logits, NEG_INF))
    g_w = 1.0 / jnp.sum(jnp.where(is_group, jnp.exp(logits - gmax), 0.0), axis=-1, keepdims=True)
    lo_lane = ROUTER_LANE0 + EXPERTS_PER_GROUP * gsel
    in_group = jnp.logical_and(lanef >= lo_lane, lanef < lo_lane + EXPERTS_PER_GROUP)
    el = jnp.where(in_group, logits, NEG_INF)
    v1, i1 = top(el)
    v2, i2 = top(jnp.where(lanef == i1, NEG_INF, el))
    t = jnp.exp(v2 - v1)
    p1 = 1.0 / (1.0 + t)
    gate1 = g_w * p1
    gate2 = g_w * (t * p1)

    oh1 = (lanef == i1).astype(F32)
    oh2 = (lanef == i2).astype(F32)
    cnt = oh1 + oh2
    prefix = jnp.dot(tri_ref[...], cnt.astype(BF16), preferred_element_type=F32)
    total = jnp.sum(cnt, axis=0, keepdims=True)
    chunks = jnp.floor((total + (ROW_CHUNK - 1)) * (1.0 / ROW_CHUNK))
    seg_start = ROW_CHUNK * jnp.dot(jnp.broadcast_to(chunks, (8, LANES)).astype(BF16), upper_ref[...],
                                    preferred_element_type=F32)[0:1, :]
    base = seg_start + prefix
    pos1 = jnp.sum(oh1 * base, axis=-1, keepdims=True)
    pos2 = jnp.sum(oh2 * base, axis=-1, keepdims=True)
    cnt_ref[...] = jnp.broadcast_to(total, cnt_ref.shape).astype(I32)
    pos_ref[...] = jnp.where(lane == 0, pos1, jnp.where(lane == 1, pos2, 0.0))
    gate_ref[...] = jnp.where(lane == 0, gate1, jnp.where(lane == 1, gate2, 0.0))


def _mix(x2, o, lse, rnn, ga, gn, wo_bf, gf, wr_pair, br, tri, upper):
    m = x2.shape[0]
    tm = TOKEN_TILE
    row = lambda i: (i, 0)
    fix = lambda i: (0, 0)
    t512 = pl.BlockSpec((tm, D_ATTN), row)
    t128 = pl.BlockSpec((tm, LANES), row)
    t1024 = pl.BlockSpec((tm, D_MODEL), row)
    slab = pl.BlockSpec((HEAD_PAIRS, tm, LANES), lambda i: (0, i, 0))
    return pl.pallas_call(
        _mix_kernel,
        grid=(m // tm,),
        in_specs=[t1024, slab, slab, slab, slab, slab, slab, t512,
                  pl.BlockSpec((1, D_ATTN), fix), pl.BlockSpec((1, D_RNN), fix),
                  pl.BlockSpec((D_MIX, D_MODEL), fix), pl.BlockSpec((1, D_MODEL), fix),
                  pl.BlockSpec((D_MODEL, 2 * LANES), fix), pl.BlockSpec((1, LANES), fix),
                  pl.BlockSpec((tm, tm), fix), pl.BlockSpec((LANES, LANES), fix)],
        out_specs=[t1024, t1024, t128, t128, pl.BlockSpec((8, LANES), row)],
        out_shape=[jax.ShapeDtypeStruct((m, D_MODEL), F32), jax.ShapeDtypeStruct((m, D_MODEL), BF16),
                   jax.ShapeDtypeStruct((m, LANES), F32), jax.ShapeDtypeStruct((m, LANES), F32),
                   jax.ShapeDtypeStruct((m // tm * 8, LANES), I32)],
        compiler_params=_cparams("parallel"),
    )(x2, o[0], o[1], o[2], lse[0], lse[1], lse[2], rnn, ga, gn, wo_bf, gf, wr_pair, br, tri, upper)


def _rows_copy(src, src_row, dst, dst_row, sem, rows=ROW_CHUNK):
    return pltpu.make_async_copy(src.at[pl.ds(src_row, rows)], dst.at[pl.ds(dst_row, rows)], sem)


def _for_each_chunk(i, ntot_ref, dst_ref, copy):
    n = ntot_ref[i]

    def body(j, carry):
        copy(pl.multiple_of(j * ROW_CHUNK, ROW_CHUNK), pl.multiple_of(dst_ref[i * TILE_CHUNKS + j], ROW_CHUNK))
        return carry

    lax.fori_loop(0, n, body, 0)
    return n


def _wait_chunks(count, wait_rows):
    def batch(c, carry):
        wait_rows(WAIT_BATCH * ROW_CHUNK)
        return carry

    def single(c, carry):
        wait_rows(ROW_CHUNK)
        return carry

    lax.fori_loop(0, count // WAIT_BATCH, batch, 0)
    lax.fori_loop(0, count % WAIT_BATCH, single, 0)


def _pack_bf16_pairs(x):
    c = x.shape[1] // 2
    lo = lax.bitcast_convert_type(x[:, :c], U32) >> 16
    hi = lax.bitcast_convert_type(x[:, c:], U32) & jnp.uint32(0xFFFF0000)
    return hi | lo


def _unpack_bf16_pairs(w):
    lo = lax.bitcast_convert_type(w << 16, F32).astype(BF16)
    hi = lax.bitcast_convert_type(w & jnp.uint32(0xFFFF0000), F32).astype(BF16)
    return lo, hi


def _dispatch_kernel(ntot_ref, dst_ref, tailn_ref, taildst_ref,
                     h_ref, pos_ref, gate_ref, rows_ref, sorted_buf, zero_buf, pending, sems, *, tm):
    i = pl.program_id(0)
    slot = i % 2
    pos_t = pos_ref[...].T
    gate_t = gate_ref[...].T
    row = lax.broadcasted_iota(I32, (TILE_ROWS, tm), 0).astype(F32)
    hit1 = row == pos_t[0:1, :]
    hit2 = row == pos_t[1:2, :]
    onehot = jnp.where(jnp.logical_or(hit1, hit2), 1.0, 0.0).astype(BF16)
    feat = jnp.dot(onehot, h_ref[...], preferred_element_type=F32)
    sorted_buf[slot, :, 0:ROW_WORDS - LANES] = _pack_bf16_pairs(feat)
    gate = jnp.sum(jnp.where(hit1, gate_t[0:1, :], 0.0) + jnp.where(hit2, gate_t[1:2, :], 0.0),
                   axis=-1, keepdims=True)
    lane = lax.broadcasted_iota(I32, (TILE_ROWS, LANES), 1)
    sorted_buf[slot, :, ROW_WORDS - LANES:] = jnp.where(lane == 0, lax.bitcast_convert_type(gate, U32), jnp.uint32(0))

    def wait_rows(sem):
        return lambda r: _rows_copy(sorted_buf.at[0], 0, rows_ref, 0, sem, r).wait()

    @pl.when(i > 0)
    def _():
        _wait_chunks(pending[0], wait_rows(sems.at[1 - slot]))

    def send(loc, dst):
        _rows_copy(sorted_buf.at[slot], loc, rows_ref, dst, sems.at[slot]).start()

    n_sent = _for_each_chunk(i, ntot_ref, dst_ref, send)

    @pl.when(i == 0)
    def _():
        zero_buf[...] = jnp.zeros_like(zero_buf)

    def tails():
        def per_expert(e, total):
            n = tailn_ref[e]
            dst0 = taildst_ref[e]
            big = n // WAIT_BATCH

            def batch(c, carry):
                _rows_copy(zero_buf, 0, rows_ref, pl.multiple_of(dst0 + c * (WAIT_BATCH * ROW_CHUNK), ROW_CHUNK),
                           sems.at[slot], WAIT_BATCH * ROW_CHUNK).start()
                return carry

            def single(c, carry):
                _rows_copy(zero_buf, 0, rows_ref, pl.multiple_of(dst0 + c * ROW_CHUNK, ROW_CHUNK),
                           sems.at[slot]).start()
                return carry

            lax.fori_loop(0, big, batch, 0)
            lax.fori_loop(big * WAIT_BATCH, n, single, 0)
            return total + n

        return lax.fori_loop(0, N_EXPERTS + 1, per_expert, jnp.int32(0))

    n_sent = n_sent + lax.cond(i == 0, tails, lambda: jnp.int32(0))
    pending[0] = n_sent

    @pl.when(i == pl.num_programs(0) - 1)
    def _():
        _wait_chunks(n_sent, wait_rows(sems.at[slot]))


def _dispatch(meta, h2, pos, gates, n_rows):
    m = h2.shape[0]
    tm = TOKEN_TILE
    tile = lambda i, *_: (i, 0)
    return pl.pallas_call(
        functools.partial(_dispatch_kernel, tm=tm),
        grid_spec=pltpu.PrefetchScalarGridSpec(
            num_scalar_prefetch=4,
            grid=(m // tm,),
            in_specs=[pl.BlockSpec((tm, D_MODEL), tile), pl.BlockSpec((tm, LANES), tile),
                      pl.BlockSpec((tm, LANES), tile)],
            out_specs=pl.BlockSpec(memory_space=pl.ANY),
            scratch_shapes=[pltpu.VMEM((2, TILE_ROWS, ROW_WORDS), U32),
                            pltpu.VMEM((WAIT_BATCH * ROW_CHUNK, ROW_WORDS), U32),
                            pltpu.SMEM((1,), I32), pltpu.SemaphoreType.DMA((2,))],
        ),
        out_shape=jax.ShapeDtypeStruct((n_rows, ROW_WORDS), U32),
        compiler_params=_cparams("arbitrary"),
    )(*meta, h2, pos, gates)


def _expert_kernel(be_ref, nused_ref, next_ref, rows_ref, wg_hbm, wu_hbm, wd_hbm, out_ref,
                   wgf, wuf, wdf, wgb, wub, wdb, holder, sems, *, layer):
    i = pl.program_id(0)
    used = i < nused_ref[0]
    expert = be_ref[i]
    first = jnp.logical_or(i == 0, expert != be_ref[jnp.maximum(i - 1, 0)])

    def weight_copies(e, slot):
        return (pltpu.make_async_copy(wg_hbm.at[layer, e], wgf.at[slot], sems.at[slot, 0]),
                pltpu.make_async_copy(wu_hbm.at[layer, e], wuf.at[slot], sems.at[slot, 1]),
                pltpu.make_async_copy(wd_hbm.at[layer, e], wdf.at[slot], sems.at[slot, 2]))

    @pl.when(jnp.logical_and(used, first))
    def _():
        @pl.when(i == 0)
        def _():
            holder[0] = 0
            for c in weight_copies(expert, 0):
                c.start()

        slot = holder[0]
        for c in weight_copies(expert, slot):
            c.wait()
        wgb[...] = wgf[slot].astype(BF16)
        wub[...] = wuf[slot].astype(BF16)
        wdb[...] = wdf[slot].astype(BF16)
        nxt = next_ref[i]

        @pl.when(nxt >= 0)
        def _():
            for c in weight_copies(nxt, 1 - slot):
                c.start()

        holder[0] = 1 - slot

    @pl.when(used)
    def _():
        xb = jnp.concatenate(_unpack_bf16_pairs(rows_ref[:, 0:ROW_WORDS - LANES]), axis=1)
        gate = lax.bitcast_convert_type(rows_ref[:, ROW_WORDS - LANES:][:, 0:1], F32)
        g = jnp.dot(xb, wgb[...], preferred_element_type=F32)
        u = jnp.dot(xb, wub[...], preferred_element_type=F32)
        act = (g * jax.nn.sigmoid(g) * u).astype(BF16)
        y = jnp.dot(act, wdb[...], preferred_element_type=F32) * gate
        out_ref[...] = _pack_bf16_pairs(y.astype(BF16).astype(F32))

    @pl.when(jnp.logical_not(used))
    def _():
        out_ref[...] = jnp.zeros_like(out_ref)


def _experts(block_expert, n_used, next_expert, rows, w_gate, w_up, w_down, layer):
    n_rows = rows.shape[0]
    blk = EXPERT_BLOCK
    rmap = lambda i, be, nu, nx: (jnp.minimum(i, nu[0] - 1), 0)
    hbm = pl.BlockSpec(memory_space=pl.ANY)
    return pl.pallas_call(
        functools.partial(_expert_kernel, layer=layer),
        grid_spec=pltpu.PrefetchScalarGridSpec(
            num_scalar_prefetch=3,
            grid=(n_rows // blk,),
            in_specs=[pl.BlockSpec((blk, ROW_WORDS), rmap), hbm, hbm, hbm],
            out_specs=pl.BlockSpec((blk, D_MODEL // 2), lambda i, be, nu, nx: (i, 0)),
            scratch_shapes=[pltpu.VMEM((2, D_MODEL, D_EXPERT), F32), pltpu.VMEM((2, D_MODEL, D_EXPERT), F32),
                            pltpu.VMEM((2, D_EXPERT, D_MODEL), F32),
                            pltpu.VMEM((D_MODEL, D_EXPERT), BF16), pltpu.VMEM((D_MODEL, D_EXPERT), BF16),
                            pltpu.VMEM((D_EXPERT, D_MODEL), BF16),
                            pltpu.SMEM((1,), I32), pltpu.SemaphoreType.DMA((2, 3))],
        ),
        out_shape=jax.ShapeDtypeStruct((n_rows, D_MODEL // 2), U32),
        compiler_params=_cparams("arbitrary"),
    )(block_expert, n_used, next_expert, rows, w_gate, w_up, w_down)


def _combine_kernel(ntot_ref, dst_ref, x1_ref, pos_ref, rows_ref, out_ref, local, pending, sems, *, tm):
    i = pl.program_id(0)
    slot = i % 2

    def fetch_tile(t, s):
        def fetch(loc, src):
            _rows_copy(rows_ref, src, local.at[s], loc, sems.at[s]).start()

        return _for_each_chunk(t, ntot_ref, dst_ref, fetch)

    @pl.when(i == 0)
    def _():
        local[...] = jnp.zeros_like(local)
        pending[0] = fetch_tile(0, 0)

    @pl.when(i + 1 < pl.num_programs(0))
    def _():
        pending[1 - slot] = fetch_tile(i + 1, 1 - slot)

    _wait_chunks(pending[slot], lambda r: _rows_copy(rows_ref, 0, local.at[0], 0, sems.at[slot], r).wait())

    pos = pos_ref[...]
    col = lax.broadcasted_iota(I32, (tm, TILE_ROWS), 1).astype(F32)
    pick = jnp.where(jnp.logical_or(col == pos[:, 0:1], col == pos[:, 1:2]), 1.0, 0.0).astype(BF16)
    lo, hi = _unpack_bf16_pairs(local[slot])
    moe = jnp.concatenate([jnp.dot(pick, lo, preferred_element_type=F32),
                           jnp.dot(pick, hi, preferred_element_type=F32)], axis=1)
    out_ref[...] = x1_ref[...] + moe


def _combine(meta, x1, pos, out_rows):
    m = x1.shape[0]
    tm = TOKEN_TILE
    tile = lambda i, *_: (i, 0)
    return pl.pallas_call(
        functools.partial(_combine_kernel, tm=tm),
        grid_spec=pltpu.PrefetchScalarGridSpec(
            num_scalar_prefetch=2,
            grid=(m // tm,),
            in_specs=[pl.BlockSpec((tm, D_MODEL), tile), pl.BlockSpec((tm, LANES), tile),
                      pl.BlockSpec(memory_space=pl.ANY)],
            out_specs=pl.BlockSpec((tm, D_MODEL), tile),
            scratch_shapes=[pltpu.VMEM((2, TILE_ROWS, D_MODEL // 2), U32), pltpu.SMEM((2,), I32),
                            pltpu.SemaphoreType.DMA((2,))],
        ),
        out_shape=jax.ShapeDtypeStruct((m, D_MODEL), F32),
        compiler_params=_cparams("arbitrary"),
    )(*meta, x1, pos, out_rows)


def _dispatch_plan(cnt_tiles, n_tiles, n_blocks):
    cnt = cnt_tiles.reshape(n_tiles, 8, LANES)[:, 0, ROUTER_LANE0:ROUTER_LANE0 + N_EXPERTS]
    seg = (cnt + ROW_CHUNK - 1) // ROW_CHUNK * ROW_CHUNK
    used = jnp.sum(seg, axis=0)
    padded = (used + EXPERT_BLOCK - 1) // EXPERT_BLOCK * EXPERT_BLOCK
    p_ends = jnp.cumsum(padded)
    p_starts = p_ends - padded
    seg_dst = p_starts[None, :] + jnp.cumsum(seg, axis=0) - seg
    loc_start = jnp.cumsum(seg, axis=1) - seg
    flat = lambda a: a.reshape(-1).astype(I32)
    block_row = jnp.arange(n_blocks, dtype=I32) * EXPERT_BLOCK
    block_expert = jnp.minimum(jnp.sum((p_ends[None, :] <= block_row[:, None]).astype(I32), axis=1), N_EXPERTS - 1)
    n_used = (p_ends[-1:] // EXPERT_BLOCK).astype(I32)
    after = (p_ends // EXPERT_BLOCK).astype(I32)[block_expert]
    next_expert = jnp.where(after < n_used[0], block_expert[jnp.minimum(after, n_blocks - 1)], -1).astype(I32)
    ends = jnp.cumsum(seg // ROW_CHUNK, axis=1)
    j = jnp.arange(TILE_CHUNKS, dtype=I32)
    owner = jnp.minimum(jnp.sum((ends[:, None, :] <= j[None, :, None]).astype(I32), axis=2), N_EXPERTS - 1)
    take = lambda a: jnp.take_along_axis(a, owner, axis=1)
    chunk_dst = take(seg_dst) + ROW_CHUNK * j[None, :] - take(loc_start)
    chunk_meta = (flat(ends[:, -1]), flat(chunk_dst))
    rest = n_blocks * EXPERT_BLOCK - p_ends[-1:]
    tail_meta = (flat(jnp.concatenate([padded - used, rest]) // ROW_CHUNK),
                 flat(jnp.concatenate([p_starts + used, p_ends[-1:]])))
    return chunk_meta, tail_meta, (block_expert.astype(I32), n_used, next_expert)


def _constants():
    gm = np.kron(np.eye(ATTN_HEADS), np.full((HEAD_DIM, HEAD_DIM), 1.0 / HEAD_DIM))
    tri = np.tril(np.ones((TOKEN_TILE, TOKEN_TILE), np.float32), -1)
    upper = np.triu(np.ones((LANES, LANES), np.float32), 1)
    return jnp.asarray(gm, BF16), jnp.asarray(tri, BF16), jnp.asarray(upper, BF16)


def kernel(x, rel_bias_table, norm_mix, w_in, q_norm, k_norm, conv_w, conv_b, rg_w_a, rg_b_a, rg_w_x, rg_b_x,
           rg_lambda, norm_attn_out, norm_rnn_out, w_out, norm_ffn, router_group_w, router_group_b,
           router_expert_w, router_expert_b, expert_w_gate, expert_w_up, expert_w_down):
    b, s, d = x.shape
    depth = w_in.shape[0]
    m = b * s
    assert d == D_MODEL and s % ATTN_TILE == 0 and s % RNN_TILE == 0 and m % TOKEN_TILE == 0

    gm, tri, upper = _constants()
    bias = _bias_tables(rel_bias_table)
    scale = HEAD_DIM ** -0.5
    n_tiles = m // TOKEN_TILE
    n_blocks = -(-(m * TOP_K + n_tiles * N_EXPERTS * (ROW_CHUNK - 1)) // EXPERT_BLOCK) + N_EXPERTS
    n_rows = n_blocks * EXPERT_BLOCK
    row1 = lambda v: v.reshape(1, -1).astype(F32)

    x2 = x.reshape(m, d).astype(F32)
    for l in range(depth):
        q, k, v, xr, gr = _proj(x2, row1(norm_mix[l]), w_in[l].astype(BF16),
                                row1(jnp.tile(q_norm[l], ATTN_HEADS) * (scale * LOG2E)),
                                row1(jnp.tile(k_norm[l], ATTN_HEADS)), gm)
        outs, lses = [], []
        for n, (_, dil) in enumerate(DILATED_BRANCHES):
            o, lse = _attention_branch(q, k, v, bias[n], b, s, dil)
            outs.append(o)
            lses.append(lse)
        rnn = _rglru(xr, gr, conv_w[l].astype(F32), row1(conv_b[l]),
                     _block_diag(rg_w_a[l]).astype(BF16), row1(rg_b_a[l]),
                     _block_diag(rg_w_x[l]).astype(BF16), row1(rg_b_x[l]), row1(rg_lambda[l]), b, s)

        wr = jnp.zeros((D_MODEL, LANES), F32)
        wr = wr.at[:, :N_GROUPS].set(router_group_w[l]).at[:, N_GROUPS:N_GROUPS + N_EXPERTS].set(router_expert_w[l])
        wr_pair = jnp.concatenate(_split_bf16(wr), axis=1)
        br = jnp.zeros((1, LANES), F32)
        br = br.at[0, :N_GROUPS].set(router_group_b[l]).at[0, N_GROUPS:N_GROUPS + N_EXPERTS].set(router_expert_b[l])
        x1, h2, pos, gates, cnt = _mix(x2, outs, lses, rnn, row1(norm_attn_out[l]), row1(norm_rnn_out[l]),
                                       w_out[l].astype(BF16), row1(norm_ffn[l]), wr_pair, br, tri, upper)

        chunk_meta, tail_meta, block_meta = _dispatch_plan(cnt, n_tiles, n_blocks)
        rows = _dispatch(chunk_meta + tail_meta, h2, pos, gates, n_rows)
        out_rows = _experts(*block_meta, rows, expert_w_gate, expert_w_up, expert_w_down, l)
        x2 = _combine(chunk_meta, x1, pos, out_rows)
    return x2.reshape(b, s, d).astype(x.dtype)
```

```python
import functools
import math

import numpy as np
import jax
import jax.numpy as jnp
from jax import lax
from jax.experimental import pallas as pl
from jax.experimental.pallas import tpu as pltpu

F32 = jnp.float32
BF16 = jnp.bfloat16
I32 = jnp.int32
U32 = jnp.uint32

D_MODEL = 1024
ATTN_HEADS = 8
HEAD_DIM = 64
D_ATTN = ATTN_HEADS * HEAD_DIM
RNN_BLOCKS = 8
D_RNN = 512
D_MIX = D_ATTN + D_RNN
D_IN = 3 * D_ATTN + 2 * D_RNN
DILATED_BRANCHES = ((128, 1), (512, 4), (2048, 16))
Q_BLOCK = 128
REL_BUCKETS = 32
REL_MAX_DIST = 2048
CONV_WIDTH = 4
RG_C = 8.0
N_GROUPS = 4
EXPERTS_PER_GROUP = 8
N_EXPERTS = N_GROUPS * EXPERTS_PER_GROUP
TOP_K = 2
D_EXPERT = 512
EPS = 1e-6
NEG_INF = -1e30
LOG2E = math.log2(math.e)
LN2 = math.log(2.0)

LANES = 128
MXU_DEPTH = 256
HEAD_PAIRS = D_ATTN // LANES
ROUTER_LANE0 = N_GROUPS
TOKEN_TILE = 512
ATTN_TILE = 2048
ATTN_UNROLL = 16
RNN_CHUNK = 64
EXPERT_BLOCK = 512
ROW_CHUNK = 8
TILE_ROWS = -(-(TOKEN_TILE * TOP_K + N_EXPERTS * (ROW_CHUNK - 1)) // 256) * 256
TILE_CHUNKS = TILE_ROWS // ROW_CHUNK
WAIT_BATCH = 16
ROW_WORDS = D_MODEL // 2 + LANES
VMEM_LIMIT = 48 * 1024 * 1024


def _cparams(*sem):
    return pltpu.CompilerParams(dimension_semantics=sem, vmem_limit_bytes=VMEM_LIMIT)


def _rms(x, gain):
    return x * lax.rsqrt(jnp.mean(x * x, axis=-1, keepdims=True) + EPS) * gain


def _rows(start, size, stride):
    return pl.ds(start, size) if stride == 1 else pl.ds(start, size, stride=stride)


def _attn_kernel(q_ref, k_ref, v_ref, bias_ref, o_ref, lse_ref, qd, kd, vd, stage, *, dil):
    i = pl.program_id(2)
    tile = q_ref.shape[0]
    rows = tile // dil
    seg = Q_BLOCK + rows

    @pl.when(i == 0)
    def _():
        for r in range(dil):
            kd[r * seg:r * seg + Q_BLOCK, :] = jnp.zeros((Q_BLOCK, LANES), BF16)
            vd[r * seg:r * seg + Q_BLOCK, :] = jnp.zeros((Q_BLOCK, LANES), BF16)

    @pl.when(i > 0)
    def _():
        for r in range(dil):
            kd[r * seg:r * seg + Q_BLOCK, :] = kd[r * seg + rows:(r + 1) * seg, :]
            vd[r * seg:r * seg + Q_BLOCK, :] = vd[r * seg + rows:(r + 1) * seg, :]

    def regroup(src_ref, dst, base, pitch):
        if dil % 16 == 0:
            quarter = tile // 4
            for c in range(4):
                stage[c * quarter:(c + 1) * quarter, :] = src_ref[pl.ds(c, quarter, stride=4), :]
            for r in range(dil):
                picked = stage[_rows((r % 4) * quarter + r // 4, rows, dil // 4), :]
                dst[base + r * pitch:base + r * pitch + rows, :] = picked.astype(BF16)
        else:
            for r in range(dil):
                dst[base + r * pitch:base + r * pitch + rows, :] = src_ref[_rows(r, rows, dil), :].astype(BF16)

    regroup(q_ref, qd, 0, rows)
    regroup(k_ref, kd, Q_BLOCK, seg)
    regroup(v_ref, vd, Q_BLOCK, seg)

    per = rows // Q_BLOCK
    lane = lax.broadcasted_iota(I32, (Q_BLOCK, LANES), 1)
    low = lane < HEAD_DIM

    def block(blk, carry):
        r = blk // per
        n = blk % per
        q0 = pl.multiple_of(r * rows + n * Q_BLOCK, Q_BLOCK)
        k0 = pl.multiple_of(r * seg + n * Q_BLOCK, Q_BLOCK)
        sel = jnp.where(jnp.logical_and(i == 0, n == 0), 1, 0)
        q = qd[pl.ds(q0, Q_BLOCK), :]
        kk = kd[pl.ds(k0, 2 * Q_BLOCK), :]
        vv = vd[pl.ds(k0, 2 * Q_BLOCK), :]
        pvs, maxes, sums = [], [], []
        for hh in range(2):
            qm = jnp.where(low if hh == 0 else jnp.logical_not(low), q, jnp.zeros_like(q))
            s = lax.dot_general(qm, kk, (((1,), (1,)), ((), ())), preferred_element_type=F32)
            s = s + bias_ref[sel, hh]
            mx = jnp.max(s, axis=-1, keepdims=True)
            p = jnp.exp2(s - mx)
            sums.append(jnp.sum(p, axis=-1, keepdims=True))
            maxes.append(mx)
            pvs.append(jnp.dot(p.astype(BF16), vv, preferred_element_type=F32))
        l = jnp.where(low, sums[0], sums[1])
        dst = _rows(r + dil * Q_BLOCK * n, Q_BLOCK, dil)
        o_ref[dst, :] = jnp.where(low, pvs[0], pvs[1]) * (1.0 / l)
        lse_ref[dst, :] = jnp.where(low, maxes[0], maxes[1]) * LN2 + jnp.log(l)
        return carry

    lax.fori_loop(0, dil * per, block, 0, unroll=ATTN_UNROLL)


def _attention_branch(q, k, v, bias, b, s, dil):
    tile = ATTN_TILE
    nt = s // tile
    blk = pl.BlockSpec((None, tile, LANES), lambda bb, hp, i: (hp, bb * nt + i, 0))
    shape = jax.ShapeDtypeStruct(q.shape, F32)
    keys = tile + Q_BLOCK * dil
    return pl.pallas_call(
        functools.partial(_attn_kernel, dil=dil),
        grid=(b, HEAD_PAIRS, nt),
        in_specs=[blk, blk, blk,
                  pl.BlockSpec((2, 2, Q_BLOCK, 2 * Q_BLOCK), lambda bb, hp, i: (0, hp, 0, 0))],
        out_specs=[blk, blk],
        out_shape=[shape, shape],
        scratch_shapes=[pltpu.VMEM((tile, LANES), BF16), pltpu.VMEM((keys, LANES), BF16),
                        pltpu.VMEM((keys, LANES), BF16), pltpu.VMEM((tile, LANES), F32)],
        compiler_params=_cparams("parallel", "parallel", "arbitrary"),
    )(q, k, v, bias)


def _bias_step_tables():
    exact = REL_BUCKETS // 2
    i = np.arange(Q_BLOCK)[:, None]
    j = np.arange(2 * Q_BLOCK)[None, :]
    steps = i + Q_BLOCK - j
    onehot, band = [], []
    for window, dil in DILATED_BRANCHES:
        dist = (np.arange(2 * Q_BLOCK) * dil).astype(np.int32)
        d = np.maximum(dist, 1).astype(np.float32)
        log_b = exact + (np.log(d / np.float32(exact)) / np.float32(math.log(REL_MAX_DIST / exact))
                         * np.float32(REL_BUCKETS - exact)).astype(np.int32)
        bucket = np.where(dist < exact, dist, np.minimum(log_b, REL_BUCKETS - 1))
        onehot.append(np.eye(REL_BUCKETS, dtype=np.float32)[bucket])
        band.append((steps >= 0) & (steps <= window // dil))
    first = np.broadcast_to(j >= Q_BLOCK, (Q_BLOCK, 2 * Q_BLOCK))
    return np.stack(onehot), np.stack(band), first


def _bias_tables(rel_table):
    onehot, band, first = _bias_step_tables()
    nb = len(DILATED_BRANCHES)
    vec = LOG2E * jnp.einsum('nsb,bh->nhs', onehot, rel_table.astype(F32), precision=lax.Precision.HIGHEST)
    width = 3 * Q_BLOCK
    ext = jnp.concatenate([jnp.zeros((nb, ATTN_HEADS, Q_BLOCK), F32), vec[..., ::-1]], axis=-1)
    hank = jnp.tile(ext, (1, 1, Q_BLOCK + 1))[..., :Q_BLOCK * (width + 1)].reshape(nb, ATTN_HEADS, Q_BLOCK, width + 1)
    bias = hank[:, :, ::-1, Q_BLOCK:width]
    regular = jnp.where(band[:, None], bias, NEG_INF)
    start = jnp.where((band & first[None])[:, None], bias, NEG_INF)
    return jnp.stack([regular, start], axis=1)


def _block_diag(w):
    nb, n, _ = w.shape
    eye = jnp.eye(nb, dtype=w.dtype)
    return (eye[:, None, :, None] * w[:, :, None, :]).reshape(nb * n, nb * n)


def _split_bf16(x):
    hi = x.astype(BF16)
    return hi, (x - hi.astype(F32)).astype(BF16)


def _mix_kernel(x_ref, o1_ref, o2_ref, o3_ref, l1_ref, l2_ref, l3_ref, rnn_ref, ga_ref, gn_ref, wo_ref, gf_ref,
                wr_ref, br_ref, tri_ref, upper_ref,
                x1_ref, h2_ref, pos_ref, gate_ref, cnt_ref):
    slabs, sumsq = [], 0.0
    for hp in range(HEAD_PAIRS):
        l1, l2, l3 = l1_ref[hp], l2_ref[hp], l3_ref[hp]
        mx = jnp.maximum(jnp.maximum(l1, l2), l3)
        e1, e2, e3 = jnp.exp(l1 - mx), jnp.exp(l2 - mx), jnp.exp(l3 - mx)
        a = (e1 * o1_ref[hp] + e2 * o2_ref[hp] + e3 * o3_ref[hp]) * (1.0 / (e1 + e2 + e3))
        slabs.append(a)
        sumsq = sumsq + jnp.sum(a * a, axis=-1, keepdims=True)
    scale = lax.rsqrt(sumsq * (1.0 / D_ATTN) + EPS)
    na = (jnp.concatenate([a * scale for a in slabs], axis=1) * ga_ref[...]).astype(BF16)
    nr = _rms(rnn_ref[...], gn_ref[...]).astype(BF16)
    x1 = (x_ref[...] + jnp.dot(na, wo_ref[0:D_ATTN, :], preferred_element_type=F32)
          + jnp.dot(nr, wo_ref[D_ATTN:, :], preferred_element_type=F32))
    x1_ref[...] = x1
    h2 = _rms(x1, gf_ref[...])
    h2_ref[...] = h2.astype(BF16)

    hh, hl = _split_bf16(h2)
    by_hi = jnp.dot(hh, wr_ref[...], preferred_element_type=F32)
    by_lo = jnp.dot(hl, wr_ref[...], preferred_element_type=F32)
    logits = by_hi[:, :LANES] + by_hi[:, LANES:] + by_lo[:, :LANES] + br_ref[...]
    tm = logits.shape[0]
    lane = lax.broadcasted_iota(I32, (tm, LANES), 1)
    lanef = lane.astype(F32)
    big = float(LANES)

    def top(vals):
        m = jnp.max(vals, axis=-1, keepdims=True)
        return m, jnp.min(jnp.where(vals == m, lanef, big), axis=-1, keepdims=True)

    is_group = lane < N_GROUPS
    gmax, gsel = top(jnp.where(is_group, logits, NEG_INF))
    g_w = 1.0 / jnp.sum(jnp.where(is_group, jnp.exp(logits - gmax), 0.0), axis=-1, keepdims=True)
    lo_lane = ROUTER_LANE0 + EXPERTS_PER_GROUP * gsel
    in_group = jnp.logical_and(lanef >= lo_lane, lanef < lo_lane + EXPERTS_PER_GROUP)
    el = jnp.where(in_group, logits, NEG_INF)
    v1, i1 = top(el)
    v2, i2 = top(jnp.where(lanef == i1, NEG_INF, el))
    t = jnp.exp(v2 - v1)
    p1 = 1.0 / (1.0 + t)
    gate1 = g_w * p1
    gate2 = g_w * (t * p1)

    oh1 = (lanef == i1).astype(F32)
    oh2 = (lanef == i2).astype(F32)
    cnt = oh1 + oh2
    prefix = jnp.dot(tri_ref[...], cnt.astype(BF16), preferred_element_type=F32)
    total = jnp.sum(cnt, axis=0, keepdims=True)
    chunks = jnp.floor((total + (ROW_CHUNK - 1)) * (1.0 / ROW_CHUNK))
    seg_start = ROW_CHUNK * jnp.dot(jnp.broadcast_to(chunks, (8, LANES)).astype(BF16), upper_ref[...],
                                    preferred_element_type=F32)[0:1, :]
    base = seg_start + prefix
    pos1 = jnp.sum(oh1 * base, axis=-1, keepdims=True)
    pos2 = jnp.sum(oh2 * base, axis=-1, keepdims=True)
    cnt_ref[...] = jnp.broadcast_to(total, cnt_ref.shape).astype(I32)
    pos_ref[...] = jnp.where(lane == 0, pos1, jnp.where(lane == 1, pos2, 0.0))
    gate_ref[...] = jnp.where(lane == 0, gate1, jnp.where(lane == 1, gate2, 0.0))


def _mix(x2, o, lse, rnn, ga, gn, wo_bf, gf, wr_pair, br, tri, upper):
    m = x2.shape[0]
    tm = TOKEN_TILE
    row = lambda i: (i, 0)
    fix = lambda i: (0, 0)
    t512 = pl.BlockSpec((tm, D_ATTN), row)
    t128 = pl.BlockSpec((tm, LANES), row)
    t1024 = pl.BlockSpec((tm, D_MODEL), row)
    slab = pl.BlockSpec((HEAD_PAIRS, tm, LANES), lambda i: (0, i, 0))
    return pl.pallas_call(
        _mix_kernel,
        grid=(m // tm,),
        in_specs=[t1024, slab, slab, slab, slab, slab, slab, t512,
                  pl.BlockSpec((1, D_ATTN), fix), pl.BlockSpec((1, D_RNN), fix),
                  pl.BlockSpec((D_MIX, D_MODEL), fix), pl.BlockSpec((1, D_MODEL), fix),
                  pl.BlockSpec((D_MODEL, 2 * LANES), fix), pl.BlockSpec((1, LANES), fix),
                  pl.BlockSpec((tm, tm), fix), pl.BlockSpec((LANES, LANES), fix)],
        out_specs=[t1024, t1024, t128, t128, pl.BlockSpec((8, LANES), row)],
        out_shape=[jax.ShapeDtypeStruct((m, D_MODEL), F32), jax.ShapeDtypeStruct((m, D_MODEL), BF16),
                   jax.ShapeDtypeStruct((m, LANES), F32), jax.ShapeDtypeStruct((m, LANES), F32),
                   jax.ShapeDtypeStruct((m // tm * 8, LANES), I32)],
        compiler_params=_cparams("parallel"),
    )(x2, o[0], o[1], o[2], lse[0], lse[1], lse[2], rnn, ga, gn, wo_bf, gf, wr_pair, br, tri, upper)


def _rows_copy(src, src_row, dst, dst_row, sem, rows=ROW_CHUNK):
    return pltpu.make_async_copy(src.at[pl.ds(src_row, rows)], dst.at[pl.ds(dst_row, rows)], sem)


def _for_each_chunk(i, ntot_ref, dst_ref, copy):
    n = ntot_ref[i]

    def body(j, carry):
        copy(pl.multiple_of(j * ROW_CHUNK, ROW_CHUNK), pl.multiple_of(dst_ref[i * TILE_CHUNKS + j], ROW_CHUNK))
        return carry

    lax.fori_loop(0, n, body, 0)
    return n


def _wait_chunks(count, wait_rows):
    def batch(c, carry):
        wait_rows(WAIT_BATCH * ROW_CHUNK)
        return carry

    def single(c, carry):
        wait_rows(ROW_CHUNK)
        return carry

    lax.fori_loop(0, count // WAIT_BATCH, batch, 0)
    lax.fori_loop(0, count % WAIT_BATCH, single, 0)


def _pack_bf16_pairs(x):
    c = x.shape[1] // 2
    lo = lax.bitcast_convert_type(x[:, :c], U32) >> 16
    hi = lax.bitcast_convert_type(x[:, c:], U32) & jnp.uint32(0xFFFF0000)
    return hi | lo


def _unpack_bf16_pairs(w):
    lo = lax.bitcast_convert_type(w << 16, F32).astype(BF16)
    hi = lax.bitcast_convert_type(w & jnp.uint32(0xFFFF0000), F32).astype(BF16)
    return lo, hi


def _dispatch_kernel(ntot_ref, dst_ref, tailn_ref, taildst_ref,
                     h_ref, pos_ref, gate_ref, rows_ref, sorted_buf, zero_buf, pending, sems, *, tm):
    i = pl.program_id(0)
    slot = i % 2
    pos_t = pos_ref[...].T
    gate_t = gate_ref[...].T
    row = lax.broadcasted_iota(I32, (TILE_ROWS, tm), 0).astype(F32)
    hit1 = row == pos_t[0:1, :]
    hit2 = row == pos_t[1:2, :]
    onehot = jnp.where(jnp.logical_or(hit1, hit2), 1.0, 0.0).astype(BF16)
    feat = jnp.dot(onehot, h_ref[...], preferred_element_type=F32)
    sorted_buf[slot, :, 0:ROW_WORDS - LANES] = _pack_bf16_pairs(feat)
    gate = jnp.sum(jnp.where(hit1, gate_t[0:1, :], 0.0) + jnp.where(hit2, gate_t[1:2, :], 0.0),
                   axis=-1, keepdims=True)
    lane = lax.broadcasted_iota(I32, (TILE_ROWS, LANES), 1)
    sorted_buf[slot, :, ROW_WORDS - LANES:] = jnp.where(lane == 0, lax.bitcast_convert_type(gate, U32), jnp.uint32(0))

    def wait_rows(sem):
        return lambda r: _rows_copy(sorted_buf.at[0], 0, rows_ref, 0, sem, r).wait()

    @pl.when(i > 0)
    def _():
        _wait_chunks(pending[0], wait_rows(sems.at[1 - slot]))

    def send(loc, dst):
        _rows_copy(sorted_buf.at[slot], loc, rows_ref, dst, sems.at[slot]).start()

    n_sent = _for_each_chunk(i, ntot_ref, dst_ref, send)

    @pl.when(i == 0)
    def _():
        zero_buf[...] = jnp.zeros_like(zero_buf)

    def tails():
        def per_expert(e, total):
            n = tailn_ref[e]
            dst0 = taildst_ref[e]
            big = n // WAIT_BATCH

            def batch(c, carry):
                _rows_copy(zero_buf, 0, rows_ref, pl.multiple_of(dst0 + c * (WAIT_BATCH * ROW_CHUNK), ROW_CHUNK),
                           sems.at[slot], WAIT_BATCH * ROW_CHUNK).start()
                return carry

            def single(c, carry):
                _rows_copy(zero_buf, 0, rows_ref, pl.multiple_of(dst0 + c * ROW_CHUNK, ROW_CHUNK),
                           sems.at[slot]).start()
                return carry

            lax.fori_loop(0, big, batch, 0)
            lax.fori_loop(big * WAIT_BATCH, n, single, 0)
            return total + n

        return lax.fori_loop(0, N_EXPERTS + 1, per_expert, jnp.int32(0))

    n_sent = n_sent + lax.cond(i == 0, tails, lambda: jnp.int32(0))
    pending[0] = n_sent

    @pl.when(i == pl.num_programs(0) - 1)
    def _():
        _wait_chunks(n_sent, wait_rows(sems.at[slot]))


def _dispatch(meta, h2, pos, gates, n_rows):
    m = h2.shape[0]
    tm = TOKEN_TILE
    tile = lambda i, *_: (i, 0)
    return pl.pallas_call(
        functools.partial(_dispatch_kernel, tm=tm),
        grid_spec=pltpu.PrefetchScalarGridSpec(
            num_scalar_prefetch=4,
            grid=(m // tm,),
            in_specs=[pl.BlockSpec((tm, D_MODEL), tile), pl.BlockSpec((tm, LANES), tile),
                      pl.BlockSpec((tm, LANES), tile)],
            out_specs=pl.BlockSpec(memory_space=pl.ANY),
            scratch_shapes=[pltpu.VMEM((2, TILE_ROWS, ROW_WORDS), U32),
                            pltpu.VMEM((WAIT_BATCH * ROW_CHUNK, ROW_WORDS), U32),
                            pltpu.SMEM((1,), I32), pltpu.SemaphoreType.DMA((2,))],
        ),
        out_shape=jax.ShapeDtypeStruct((n_rows, ROW_WORDS), U32),
        compiler_params=_cparams("arbitrary"),
    )(*meta, h2, pos, gates)


def _expert_kernel(be_ref, nused_ref, next_ref, rows_ref, wg_hbm, wu_hbm, wd_hbm, out_ref,
                   wgf, wuf, wdf, wgb, wub, wdb, holder, sems, *, layer):
    i = pl.program_id(0)
    used = i < nused_ref[0]
    expert = be_ref[i]
    first = jnp.logical_or(i == 0, expert != be_ref[jnp.maximum(i - 1, 0)])

    def weight_copies(e, slot):
        return (pltpu.make_async_copy(wg_hbm.at[layer, e], wgf.at[slot], sems.at[slot, 0]),
                pltpu.make_async_copy(wu_hbm.at[layer, e], wuf.at[slot], sems.at[slot, 1]),
                pltpu.make_async_copy(wd_hbm.at[layer, e], wdf.at[slot], sems.at[slot, 2]))

    @pl.when(jnp.logical_and(used, first))
    def _():
        @pl.when(i == 0)
        def _():
            holder[0] = 0
            for c in weight_copies(expert, 0):
                c.start()

        slot = holder[0]
        for c in weight_copies(expert, slot):
            c.wait()
        wgb[...] = wgf[slot].astype(BF16)
        wub[...] = wuf[slot].astype(BF16)
        wdb[...] = wdf[slot].astype(BF16)
        nxt = next_ref[i]

        @pl.when(nxt >= 0)
        def _():
            for c in weight_copies(nxt, 1 - slot):
                c.start()

        holder[0] = 1 - slot

    @pl.when(used)
    def _():
        xb = jnp.concatenate(_unpack_bf16_pairs(rows_ref[:, 0:ROW_WORDS - LANES]), axis=1)
        gate = lax.bitcast_convert_type(rows_ref[:, ROW_WORDS - LANES:][:, 0:1], F32)
        g = jnp.dot(xb, wgb[...], preferred_element_type=F32)
        u = jnp.dot(xb, wub[...], preferred_element_type=F32)
        act = (g * jax.nn.sigmoid(g) * u).astype(BF16)
        y = jnp.dot(act, wdb[...], preferred_element_type=F32) * gate
        out_ref[...] = _pack_bf16_pairs(y.astype(BF16).astype(F32))

    @pl.when(jnp.logical_not(used))
    def _():
        out_ref[...] = jnp.zeros_like(out_ref)


def _experts(block_expert, n_used, next_expert, rows, w_gate, w_up, w_down, layer):
    n_rows = rows.shape[0]
    blk = EXPERT_BLOCK
    rmap = lambda i, be, nu, nx: (jnp.minimum(i, nu[0] - 1), 0)
    hbm = pl.BlockSpec(memory_space=pl.ANY)
    return pl.pallas_call(
        functools.partial(_expert_kernel, layer=layer),
        grid_spec=pltpu.PrefetchScalarGridSpec(
            num_scalar_prefetch=3,
            grid=(n_rows // blk,),
            in_specs=[pl.BlockSpec((blk, ROW_WORDS), rmap), hbm, hbm, hbm],
            out_specs=pl.BlockSpec((blk, D_MODEL // 2), lambda i, be, nu, nx: (i, 0)),
            scratch_shapes=[pltpu.VMEM((2, D_MODEL, D_EXPERT), F32), pltpu.VMEM((2, D_MODEL, D_EXPERT), F32),
                            pltpu.VMEM((2, D_EXPERT, D_MODEL), F32),
                            pltpu.VMEM((D_MODEL, D_EXPERT), BF16), pltpu.VMEM((D_MODEL, D_EXPERT), BF16),
                            pltpu.VMEM((D_EXPERT, D_MODEL), BF16),
                            pltpu.SMEM((1,), I32), pltpu.SemaphoreType.DMA((2, 3))],
        ),
        out_shape=jax.ShapeDtypeStruct((n_rows, D_MODEL // 2), U32),
        compiler_params=_cparams("arbitrary"),
    )(block_expert, n_used, next_expert, rows, w_gate, w_up, w_down)


def _proj_section(hb, w_ref, n):
    return jnp.dot(hb, w_ref[:, n * D_ATTN:(n + 1) * D_ATTN], preferred_element_type=F32)


def _proj_qkv(hb, w_ref, qg_ref, kg_ref, gm_ref, q_ref, k_ref, v_ref):
    sec = functools.partial(_proj_section, hb, w_ref)

    def head_norm(z, gain):
        zz = (z * z).astype(BF16)
        half = gm_ref.shape[0]
        ms = jnp.concatenate([jnp.dot(zz[:, c:c + half], gm_ref[...], preferred_element_type=F32)
                              for c in range(0, D_ATTN, half)], axis=1)
        return z * lax.rsqrt(ms + EPS) * gain

    def put_slabs(ref, z):
        for hp in range(HEAD_PAIRS):
            ref[hp] = z[:, hp * LANES:(hp + 1) * LANES]

    put_slabs(q_ref, head_norm(sec(0), qg_ref[...]))
    put_slabs(k_ref, head_norm(sec(1), kg_ref[...]))
    put_slabs(v_ref, sec(2))


def _rglru_gates(xr, gr, cw_ref, cb_ref, wa_ref, ba_ref, wx_ref, bx_ref, lam_ref, xbuf, a_scr, b_scr, g_scr):
    tt = xr.shape[0]
    xbuf[8:8 + tt, :] = xr
    g_scr[...] = jax.nn.gelu(gr, approximate=True)
    u = cb_ref[...] + cw_ref[3:4, :] * xr
    for back in range(1, CONV_WIDTH):
        u = u + cw_ref[3 - back:4 - back, :] * xbuf[8 - back:8 - back + tt, :]
    xbuf[0:8, :] = xbuf[tt:tt + 8, :]

    ub = u.astype(BF16)
    r = jax.nn.sigmoid(jnp.dot(ub, wa_ref[...], preferred_element_type=F32) + ba_ref[...])
    gi = jax.nn.sigmoid(jnp.dot(ub, wx_ref[...], preferred_element_type=F32) + bx_ref[...])
    nl = -lam_ref[...]
    softplus = jnp.maximum(nl, 0.0) + jnp.log1p(jnp.exp(-jnp.abs(nl)))
    log_a = (-RG_C) * r * softplus
    a = jnp.exp(log_a)
    a_scr[...] = a
    b_scr[...] = jnp.sqrt(-jnp.tanh(log_a) * (a * a + 1.0)) * (gi * u)


def _rglru_scan(out_ref, a_scr, b_scr, h_scr, g_scr):
    tt = out_ref.shape[0]
    rc = RNN_CHUNK
    row = lax.broadcasted_iota(I32, (rc, LANES), 0)

    def chunk(c, carry):
        r0 = pl.multiple_of(c * rc, rc)
        for g in range(D_RNN // LANES):
            ls = slice(g * LANES, (g + 1) * LANES)
            aa = a_scr[pl.ds(r0, rc), ls]
            bb = b_scr[pl.ds(r0, rc), ls]
            k = 1
            while k < rc:
                keep = row >= k
                a_sh = pltpu.roll(aa, k, 0)
                b_sh = pltpu.roll(bb, k, 0)
                bb = jnp.where(keep, aa * b_sh + bb, bb)
                aa = jnp.where(keep, aa * a_sh, aa)
                k *= 2
            h = aa * h_scr[0:1, ls] + bb
            h_scr[0:1, ls] = h[rc - 1:rc, :]
            out_ref[pl.ds(r0, rc), ls] = h * g_scr[pl.ds(r0, rc), ls]
        return carry

    lax.fori_loop(0, tt // rc, chunk, 0)


def _combine_body(ntot_ref, dst_ref, x1_ref, pos_ref, rows_ref, local, pending, sems):
    i = pl.program_id(0)
    slot = i % 2
    tm = x1_ref.shape[0]

    def fetch_tile(t, s):
        def fetch(loc, src):
            _rows_copy(rows_ref, src, local.at[s], loc, sems.at[s]).start()

        return _for_each_chunk(t, ntot_ref, dst_ref, fetch)

    @pl.when(i == 0)
    def _():
        local[...] = jnp.zeros_like(local)
        pending[0] = fetch_tile(0, 0)

    @pl.when(i + 1 < pl.num_programs(0))
    def _():
        pending[1 - slot] = fetch_tile(i + 1, 1 - slot)

    _wait_chunks(pending[slot], lambda r: _rows_copy(rows_ref, 0, local.at[0], 0, sems.at[slot], r).wait())

    pos = pos_ref[...]
    col = lax.broadcasted_iota(I32, (tm, TILE_ROWS), 1).astype(F32)
    pick = jnp.where(jnp.logical_or(col == pos[:, 0:1], col == pos[:, 1:2]), 1.0, 0.0).astype(BF16)
    lo, hi = _unpack_bf16_pairs(local[slot])
    moe = jnp.concatenate([jnp.dot(pick, lo, preferred_element_type=F32),
                           jnp.dot(pick, hi, preferred_element_type=F32)], axis=1)
    return x1_ref[...] + moe


N_PROJ_IN = 12


def _edge_kernel(*refs, with_combine, with_proj, tiles_per_seq):
    refs = list(refs)
    take = lambda n: [refs.pop(0) for _ in range(n)]
    if with_combine:
        ntot_ref, dst_ref = take(2)
    (x_ref,) = take(1)
    if with_combine:
        pos_ref, rows_ref = take(2)
    if with_proj:
        g_ref, w_ref, qg_ref, kg_ref, gm_ref, cw_ref, cb_ref, wa_ref, ba_ref, wx_ref, bx_ref, lam_ref = take(N_PROJ_IN)
    if with_combine:
        (x2_ref,) = take(1)
    if with_proj:
        q_ref, k_ref, v_ref, rnn_ref = take(4)
    if with_combine:
        local, pending, sems = take(3)
    if with_proj:
        xbuf, a_scr, b_scr, h_scr, g_scr = take(5)

    if with_proj:
        @pl.when(pl.program_id(0) % tiles_per_seq == 0)
        def _():
            xbuf[0:8, :] = jnp.zeros((8, D_RNN), F32)
            h_scr[...] = jnp.zeros_like(h_scr)

    if with_combine:
        x = _combine_body(ntot_ref, dst_ref, x_ref, pos_ref, rows_ref, local, pending, sems)
        x2_ref[...] = x
    else:
        x = x_ref[...]
    if with_proj:
        hb = _rms(x, g_ref[...]).astype(BF16)
        _rglru_gates(_proj_section(hb, w_ref, 3), _proj_section(hb, w_ref, 4), cw_ref, cb_ref, wa_ref, ba_ref,
                     wx_ref, bx_ref, lam_ref, xbuf, a_scr, b_scr, g_scr)
        _proj_qkv(hb, w_ref, qg_ref, kg_ref, gm_ref, q_ref, k_ref, v_ref)
        _rglru_scan(rnn_ref, a_scr, b_scr, h_scr, g_scr)


def _edge(x_in, seq_len, combine_in=None, proj_in=None):
    m = x_in.shape[0]
    tm = TOKEN_TILE
    with_combine, with_proj = combine_in is not None, proj_in is not None
    tile = lambda i, *_: (i, 0)
    fix = lambda i, *_: (0, 0)
    operands, in_specs, out_specs, out_shape, scratch = [], [], [], [], []
    n_prefetch = 0
    if with_combine:
        chunk_meta, pos, out_rows = combine_in
        operands += list(chunk_meta)
        n_prefetch = len(chunk_meta)
    operands.append(x_in)
    in_specs.append(pl.BlockSpec((tm, D_MODEL), tile))
    if with_combine:
        operands += [pos, out_rows]
        in_specs += [pl.BlockSpec((tm, LANES), tile), pl.BlockSpec(memory_space=pl.ANY)]
        out_specs.append(pl.BlockSpec((tm, D_MODEL), tile))
        out_shape.append(jax.ShapeDtypeStruct((m, D_MODEL), F32))
        scratch += [pltpu.VMEM((2, TILE_ROWS, D_MODEL // 2), U32), pltpu.SMEM((2,), I32),
                    pltpu.SemaphoreType.DMA((2,))]
    if with_proj:
        assert len(proj_in) == N_PROJ_IN
        operands += list(proj_in)
        in_specs += [pl.BlockSpec(a.shape, fix) for a in proj_in]
        slab = pl.BlockSpec((HEAD_PAIRS, tm, LANES), lambda i, *_: (0, i, 0))
        out_specs += [slab, slab, slab, pl.BlockSpec((tm, D_RNN), tile)]
        out_shape += [jax.ShapeDtypeStruct((HEAD_PAIRS, m, LANES), F32)] * 3 + [jax.ShapeDtypeStruct((m, D_RNN), F32)]
        scratch += [pltpu.VMEM((tm + 8, D_RNN), F32), pltpu.VMEM((tm, D_RNN), F32), pltpu.VMEM((tm, D_RNN), F32),
                    pltpu.VMEM((8, D_RNN), F32), pltpu.VMEM((tm, D_RNN), F32)]
    return pl.pallas_call(
        functools.partial(_edge_kernel, with_combine=with_combine, with_proj=with_proj,
                          tiles_per_seq=seq_len // tm),
        grid_spec=pltpu.PrefetchScalarGridSpec(
            num_scalar_prefetch=n_prefetch, grid=(m // tm,),
            in_specs=in_specs, out_specs=out_specs, scratch_shapes=scratch),
        out_shape=out_shape,
        compiler_params=_cparams("arbitrary"),
    )(*operands)


def _dispatch_plan(cnt_tiles, n_tiles, n_blocks):
    cnt = cnt_tiles.reshape(n_tiles, 8, LANES)[:, 0, ROUTER_LANE0:ROUTER_LANE0 + N_EXPERTS]
    seg = (cnt + ROW_CHUNK - 1) // ROW_CHUNK * ROW_CHUNK
    used = jnp.sum(seg, axis=0)
    padded = (used + EXPERT_BLOCK - 1) // EXPERT_BLOCK * EXPERT_BLOCK
    p_ends = jnp.cumsum(padded)
    p_starts = p_ends - padded
    seg_dst = p_starts[None, :] + jnp.cumsum(seg, axis=0) - seg
    loc_start = jnp.cumsum(seg, axis=1) - seg
    flat = lambda a: a.reshape(-1).astype(I32)
    block_row = jnp.arange(n_blocks, dtype=I32) * EXPERT_BLOCK
    block_expert = jnp.minimum(jnp.sum((p_ends[None, :] <= block_row[:, None]).astype(I32), axis=1), N_EXPERTS - 1)
    n_used = (p_ends[-1:] // EXPERT_BLOCK).astype(I32)
    after = (p_ends // EXPERT_BLOCK).astype(I32)[block_expert]
    next_expert = jnp.where(after < n_used[0], block_expert[jnp.minimum(after, n_blocks - 1)], -1).astype(I32)
    ends = jnp.cumsum(seg // ROW_CHUNK, axis=1)
    j = jnp.arange(TILE_CHUNKS, dtype=I32)
    owner = jnp.minimum(jnp.sum((ends[:, None, :] <= j[None, :, None]).astype(I32), axis=2), N_EXPERTS - 1)
    is_owner = owner[:, :, None] == jnp.arange(N_EXPERTS, dtype=I32)[None, None, :]
    chunk_dst = jnp.sum(jnp.where(is_owner, (seg_dst - loc_start)[:, None, :], 0), axis=2) + ROW_CHUNK * j[None, :]
    chunk_meta = (flat(ends[:, -1]), flat(chunk_dst))
    rest = n_blocks * EXPERT_BLOCK - p_ends[-1:]
    tail_meta = (flat(jnp.concatenate([padded - used, rest]) // ROW_CHUNK),
                 flat(jnp.concatenate([p_starts + used, p_ends[-1:]])))
    return chunk_meta, tail_meta, (block_expert.astype(I32), n_used, next_expert)


def _constants():
    gm = np.kron(np.eye(MXU_DEPTH // HEAD_DIM), np.full((HEAD_DIM, HEAD_DIM), 1.0 / HEAD_DIM))
    tri = np.tril(np.ones((TOKEN_TILE, TOKEN_TILE), np.float32), -1)
    upper = np.triu(np.ones((LANES, LANES), np.float32), 1)
    return jnp.asarray(gm, BF16), jnp.asarray(tri, BF16), jnp.asarray(upper, BF16)


def kernel(x, rel_bias_table, norm_mix, w_in, q_norm, k_norm, conv_w, conv_b, rg_w_a, rg_b_a, rg_w_x, rg_b_x,
           rg_lambda, norm_attn_out, norm_rnn_out, w_out, norm_ffn, router_group_w, router_group_b,
           router_expert_w, router_expert_b, expert_w_gate, expert_w_up, expert_w_down):
    b, s, d = x.shape
    depth = w_in.shape[0]
    m = b * s
    assert d == D_MODEL and s % ATTN_TILE == 0 and s % TOKEN_TILE == 0

    gm, tri, upper = _constants()
    bias = _bias_tables(rel_bias_table)
    scale = HEAD_DIM ** -0.5
    n_tiles = m // TOKEN_TILE
    n_blocks = -(-(m * TOP_K + n_tiles * N_EXPERTS * (ROW_CHUNK - 1)) // EXPERT_BLOCK) + N_EXPERTS
    n_rows = n_blocks * EXPERT_BLOCK
    row1 = lambda v: v.reshape(1, -1).astype(F32)

    def proj_params(l):
        return (row1(norm_mix[l]), w_in[l].astype(BF16),
                row1(jnp.tile(q_norm[l], ATTN_HEADS) * (scale * LOG2E)), row1(jnp.tile(k_norm[l], ATTN_HEADS)), gm,
                conv_w[l].astype(F32), row1(conv_b[l]),
                _block_diag(rg_w_a[l]).astype(BF16), row1(rg_b_a[l]),
                _block_diag(rg_w_x[l]).astype(BF16), row1(rg_b_x[l]), row1(rg_lambda[l]))

    x2 = x.reshape(m, d).astype(F32)
    q, k, v, rnn = _edge(x2, s, proj_in=proj_params(0))
    for l in range(depth):
        outs, lses = [], []
        for n, (_, dil) in enumerate(DILATED_BRANCHES):
            o, lse = _attention_branch(q, k, v, bias[n], b, s, dil)
            outs.append(o)
            lses.append(lse)

        wr = jnp.zeros((D_MODEL, LANES), F32)
        wr = wr.at[:, :N_GROUPS].set(router_group_w[l]).at[:, N_GROUPS:N_GROUPS + N_EXPERTS].set(router_expert_w[l])
        wr_pair = jnp.concatenate(_split_bf16(wr), axis=1)
        br = jnp.zeros((1, LANES), F32)
        br = br.at[0, :N_GROUPS].set(router_group_b[l]).at[0, N_GROUPS:N_GROUPS + N_EXPERTS].set(router_expert_b[l])
        x1, h2, pos, gates, cnt = _mix(x2, outs, lses, rnn, row1(norm_attn_out[l]), row1(norm_rnn_out[l]),
                                       w_out[l].astype(BF16), row1(norm_ffn[l]), wr_pair, br, tri, upper)

        chunk_meta, tail_meta, block_meta = _dispatch_plan(cnt, n_tiles, n_blocks)
        rows = _dispatch(chunk_meta + tail_meta, h2, pos, gates, n_rows)
        out_rows = _experts(*block_meta, rows, expert_w_gate, expert_w_up, expert_w_down, l)
        nxt = proj_params(l + 1) if l + 1 < depth else None
        x2, *started = _edge(x1, s, combine_in=(chunk_meta, pos, out_rows), proj_in=nxt)
        if started:
            q, k, v, rnn = started
    return x2.reshape(b, s, d).astype(x.dtype)
```

```python
import functools
import math

import numpy as np
import jax
import jax.numpy as jnp
from jax import lax
from jax.experimental import pallas as pl
from jax.experimental.pallas import tpu as pltpu

F32 = jnp.float32
BF16 = jnp.bfloat16
I32 = jnp.int32
U32 = jnp.uint32

D_MODEL = 1024
ATTN_HEADS = 8
HEAD_DIM = 64
D_ATTN = ATTN_HEADS * HEAD_DIM
RNN_BLOCKS = 8
D_RNN = 512
D_MIX = D_ATTN + D_RNN
D_IN = 3 * D_ATTN + 2 * D_RNN
DILATED_BRANCHES = ((128, 1), (512, 4), (2048, 16))
Q_BLOCK = 128
REL_BUCKETS = 32
REL_MAX_DIST = 2048
CONV_WIDTH = 4
RG_C = 8.0
N_GROUPS = 4
EXPERTS_PER_GROUP = 8
N_EXPERTS = N_GROUPS * EXPERTS_PER_GROUP
TOP_K = 2
D_EXPERT = 512
EPS = 1e-6
NEG_INF = -1e30
LOG2E = math.log2(math.e)

LANES = 128
MXU_DEPTH = 256
HEAD_PAIRS = D_ATTN // LANES
ROUTER_LANE0 = N_GROUPS
TOKEN_TILE = 512
ATTN_TILE = 2048
ATTN_PAIRS = 2
ATTN_UNROLL = 16
RNN_CHUNK = 64
EXPERT_SUBBLOCK = 128
EXPERT_BLOCK = 512
ROW_CHUNK = 8
TILE_ROWS = -(-(TOKEN_TILE * TOP_K + N_EXPERTS * (ROW_CHUNK - 1)) // 256) * 256
TILE_CHUNKS = TILE_ROWS // ROW_CHUNK
WAIT_BATCH = 16
ROW_WORDS = D_MODEL // 2 + LANES
VMEM_LIMIT = 48 * 1024 * 1024


def _cparams(*sem):
    return pltpu.CompilerParams(dimension_semantics=sem, vmem_limit_bytes=VMEM_LIMIT)


def _rms(x, gain):
    return x * lax.rsqrt(jnp.mean(x * x, axis=-1, keepdims=True) + EPS) * gain


def _rows(start, size, stride):
    return pl.ds(start, size) if stride == 1 else pl.ds(start, size, stride=stride)


def _attn_kernel(q_ref, k_ref, v_ref, bias_ref, o_ref, lse_ref, qd, kd, vd, stage, *, dil):
    for p in range(ATTN_PAIRS):
        _attn_pair(q_ref.at[p], k_ref.at[p], v_ref.at[p], bias_ref, 2 * p, o_ref.at[p], lse_ref.at[p],
                   qd, kd.at[p], vd.at[p], stage, dil)


def _attn_pair(q_ref, k_ref, v_ref, bias_ref, head0, o_ref, lse_ref, qd, kd, vd, stage, dil):
    i = pl.program_id(2)
    tile = q_ref.shape[0]
    rows = tile // dil
    seg = Q_BLOCK + rows

    @pl.when(i == 0)
    def _():
        for r in range(dil):
            kd[r * seg:r * seg + Q_BLOCK, :] = jnp.zeros((Q_BLOCK, LANES), BF16)
            vd[r * seg:r * seg + Q_BLOCK, :] = jnp.zeros((Q_BLOCK, LANES), BF16)

    @pl.when(i > 0)
    def _():
        for r in range(dil):
            kd[r * seg:r * seg + Q_BLOCK, :] = kd[r * seg + rows:(r + 1) * seg, :]
            vd[r * seg:r * seg + Q_BLOCK, :] = vd[r * seg + rows:(r + 1) * seg, :]

    def regroup(src_ref, dst, base, pitch):
        if dil % 16 == 0:
            quarter = tile // 4
            for c in range(4):
                stage[c * quarter:(c + 1) * quarter, :] = src_ref[pl.ds(c, quarter, stride=4), :]
            for r in range(dil):
                picked = stage[_rows((r % 4) * quarter + r // 4, rows, dil // 4), :]
                dst[base + r * pitch:base + r * pitch + rows, :] = picked.astype(BF16)
        else:
            for r in range(dil):
                dst[base + r * pitch:base + r * pitch + rows, :] = src_ref[_rows(r, rows, dil), :].astype(BF16)

    regroup(q_ref, qd, 0, rows)
    regroup(k_ref, kd, Q_BLOCK, seg)
    regroup(v_ref, vd, Q_BLOCK, seg)

    per = rows // Q_BLOCK
    lane = lax.broadcasted_iota(I32, (Q_BLOCK, LANES), 1)
    low = lane < HEAD_DIM

    def block(blk, carry):
        r = blk // per
        n = blk % per
        q0 = pl.multiple_of(r * rows + n * Q_BLOCK, Q_BLOCK)
        k0 = pl.multiple_of(r * seg + n * Q_BLOCK, Q_BLOCK)
        sel = jnp.where(jnp.logical_and(i == 0, n == 0), 1, 0)
        q = qd[pl.ds(q0, Q_BLOCK), :]
        kk = kd[pl.ds(k0, 2 * Q_BLOCK), :]
        vv = vd[pl.ds(k0, 2 * Q_BLOCK), :]
        pvs, maxes, sums = [], [], []
        for hh in range(2):
            qm = jnp.where(low if hh == 0 else jnp.logical_not(low), q, jnp.zeros_like(q))
            s = lax.dot_general(qm, kk, (((1,), (1,)), ((), ())), preferred_element_type=F32)
            s = s + bias_ref[sel, head0 + hh]
            mx = jnp.max(s, axis=-1, keepdims=True)
            p = jnp.exp2(s - mx)
            sums.append(jnp.sum(p, axis=-1, keepdims=True))
            maxes.append(mx)
            pvs.append(jnp.dot(p.astype(BF16), vv, preferred_element_type=F32))
        l = jnp.where(low, sums[0], sums[1])
        dst = _rows(r + dil * Q_BLOCK * n, Q_BLOCK, dil)
        o_ref[dst, :] = jnp.where(low, pvs[0], pvs[1]) * (1.0 / l)
        lse_ref[dst, :] = jnp.where(low, maxes[0], maxes[1]) + jnp.log(l) * LOG2E
        return carry

    lax.fori_loop(0, dil * per, block, 0, unroll=ATTN_UNROLL)


def _attention_branch(q, k, v, bias, b, s, dil):
    tile = ATTN_TILE
    nt = s // tile
    blk = pl.BlockSpec((ATTN_PAIRS, tile, LANES), lambda bb, g, i: (g, bb * nt + i, 0))
    shape = jax.ShapeDtypeStruct(q.shape, F32)
    keys = tile + Q_BLOCK * dil
    return pl.pallas_call(
        functools.partial(_attn_kernel, dil=dil),
        grid=(b, HEAD_PAIRS // ATTN_PAIRS, nt),
        in_specs=[blk, blk, blk,
                  pl.BlockSpec((2, 2 * ATTN_PAIRS, Q_BLOCK, 2 * Q_BLOCK), lambda bb, g, i: (0, g, 0, 0))],
        out_specs=[blk, blk],
        out_shape=[shape, shape],
        scratch_shapes=[pltpu.VMEM((tile, LANES), BF16), pltpu.VMEM((ATTN_PAIRS, keys, LANES), BF16),
                        pltpu.VMEM((ATTN_PAIRS, keys, LANES), BF16), pltpu.VMEM((tile, LANES), F32)],
        compiler_params=_cparams("parallel", "parallel", "arbitrary"),
    )(q, k, v, bias)


def _bias_step_tables():
    exact = REL_BUCKETS // 2
    i = np.arange(Q_BLOCK)[:, None]
    j = np.arange(2 * Q_BLOCK)[None, :]
    steps = i + Q_BLOCK - j
    onehot, band = [], []
    for window, dil in DILATED_BRANCHES:
        dist = (np.arange(2 * Q_BLOCK) * dil).astype(np.int32)
        d = np.maximum(dist, 1).astype(np.float32)
        log_b = exact + (np.log(d / np.float32(exact)) / np.float32(math.log(REL_MAX_DIST / exact))
                         * np.float32(REL_BUCKETS - exact)).astype(np.int32)
        bucket = np.where(dist < exact, dist, np.minimum(log_b, REL_BUCKETS - 1))
        onehot.append(np.eye(REL_BUCKETS, dtype=np.float32)[bucket])
        band.append((steps >= 0) & (steps <= window // dil))
    first = np.broadcast_to(j >= Q_BLOCK, (Q_BLOCK, 2 * Q_BLOCK))
    return np.stack(onehot), np.stack(band), first


def _bias_tables(rel_table):
    onehot, band, first = _bias_step_tables()
    nb = len(DILATED_BRANCHES)
    vec = LOG2E * jnp.einsum('nsb,bh->nhs', onehot, rel_table.astype(F32), precision=lax.Precision.HIGHEST)
    width = 3 * Q_BLOCK
    ext = jnp.concatenate([jnp.zeros((nb, ATTN_HEADS, Q_BLOCK), F32), vec[..., ::-1]], axis=-1)
    hank = jnp.tile(ext, (1, 1, Q_BLOCK + 1))[..., :Q_BLOCK * (width + 1)].reshape(nb, ATTN_HEADS, Q_BLOCK, width + 1)
    bias = hank[:, :, ::-1, Q_BLOCK:width]
    regular = jnp.where(band[:, None], bias, NEG_INF)
    start = jnp.where((band & first[None])[:, None], bias, NEG_INF)
    return jnp.stack([regular, start], axis=1)


def _block_diag(w):
    nb, n, _ = w.shape
    eye = jnp.eye(nb, dtype=w.dtype)
    return (eye[:, None, :, None] * w[:, :, None, :]).reshape(nb * n, nb * n)


def _split_bf16(x):
    hi = x.astype(BF16)
    return hi, (x - hi.astype(F32)).astype(BF16)


def _mix_kernel(x_ref, o1_ref, o2_ref, o3_ref, l1_ref, l2_ref, l3_ref, rnn_ref, ga_ref, gn_ref, wo_ref, gf_ref,
                wr_ref, br_ref, tri_ref, upper_ref,
                x1_ref, h2_ref, pos_ref, gate_ref, cnt_ref):
    slabs, sumsq = [], 0.0
    for hp in range(HEAD_PAIRS):
        l1, l2, l3 = l1_ref[hp], l2_ref[hp], l3_ref[hp]
        mx = jnp.maximum(jnp.maximum(l1, l2), l3)
        e1, e2, e3 = jnp.exp2(l1 - mx), jnp.exp2(l2 - mx), jnp.exp2(l3 - mx)
        a = (e1 * o1_ref[hp] + e2 * o2_ref[hp] + e3 * o3_ref[hp]) * (1.0 / (e1 + e2 + e3))
        slabs.append(a)
        sumsq = sumsq + jnp.sum(a * a, axis=-1, keepdims=True)
    scale = lax.rsqrt(sumsq * (1.0 / D_ATTN) + EPS)
    na = (jnp.concatenate([a * scale for a in slabs], axis=1) * ga_ref[...]).astype(BF16)
    nr = _rms(rnn_ref[...], gn_ref[...]).astype(BF16)
    x1 = (x_ref[...] + jnp.dot(na, wo_ref[0:D_ATTN, :], preferred_element_type=F32)
          + jnp.dot(nr, wo_ref[D_ATTN:, :], preferred_element_type=F32))
    x1_ref[...] = x1
    h2 = _rms(x1, gf_ref[...])
    h2_ref[...] = h2.astype(BF16)

    hh, hl = _split_bf16(h2)
    by_hi = jnp.dot(hh, wr_ref[...], preferred_element_type=F32)
    by_lo = jnp.dot(hl, wr_ref[...], preferred_element_type=F32)
    logits = by_hi[:, :LANES] + by_hi[:, LANES:] + by_lo[:, :LANES] + br_ref[...]
    tm = logits.shape[0]
    lane = lax.broadcasted_iota(I32, (tm, LANES), 1)
    lanef = lane.astype(F32)
    big = float(LANES)

    def top(vals):
        m = jnp.max(vals, axis=-1, keepdims=True)
        return m, jnp.min(jnp.where(vals == m, lanef, big), axis=-1, keepdims=True)

    is_group = lane < N_GROUPS
    gmax, gsel = top(jnp.where(is_group, logits, NEG_INF))
    g_w = 1.0 / jnp.sum(jnp.where(is_group, jnp.exp(logits - gmax), 0.0), axis=-1, keepdims=True)
    lo_lane = ROUTER_LANE0 + EXPERTS_PER_GROUP * gsel
    in_group = jnp.logical_and(lanef >= lo_lane, lanef < lo_lane + EXPERTS_PER_GROUP)
    el = jnp.where(in_group, logits, NEG_INF)
    v1, i1 = top(el)
    v2, i2 = top(jnp.where(lanef == i1, NEG_INF, el))
    t = jnp.exp(v2 - v1)
    p1 = 1.0 / (1.0 + t)
    gate1 = g_w * p1
    gate2 = g_w * (t * p1)

    oh1 = (lanef == i1).astype(F32)
    oh2 = (lanef == i2).astype(F32)
    cnt = oh1 + oh2
    prefix = jnp.dot(tri_ref[...], cnt.astype(BF16), preferred_element_type=F32)
    total = jnp.sum(cnt, axis=0, keepdims=True)
    chunks = jnp.floor((total + (ROW_CHUNK - 1)) * (1.0 / ROW_CHUNK))
    seg_start = ROW_CHUNK * jnp.dot(jnp.broadcast_to(chunks, (8, LANES)).astype(BF16), upper_ref[...],
                                    preferred_element_type=F32)[0:1, :]
    base = seg_start + prefix
    pos1 = jnp.sum(oh1 * base, axis=-1, keepdims=True)
    pos2 = jnp.sum(oh2 * base, axis=-1, keepdims=True)
    cnt_ref[...] = jnp.broadcast_to(total, cnt_ref.shape).astype(I32)
    pos_ref[...] = jnp.where(lane == 0, pos1, jnp.where(lane == 1, pos2, 0.0))
    gate_ref[...] = jnp.where(lane == 0, gate1, jnp.where(lane == 1, gate2, 0.0))


def _mix(x2, o, lse, rnn, ga, gn, wo_bf, gf, wr_pair, br, tri, upper):
    m = x2.shape[0]
    tm = TOKEN_TILE
    row = lambda i: (i, 0)
    fix = lambda i: (0, 0)
    t512 = pl.BlockSpec((tm, D_ATTN), row)
    t128 = pl.BlockSpec((tm, LANES), row)
    t1024 = pl.BlockSpec((tm, D_MODEL), row)
    slab = pl.BlockSpec((HEAD_PAIRS, tm, LANES), lambda i: (0, i, 0))
    return pl.pallas_call(
        _mix_kernel,
        grid=(m // tm,),
        in_specs=[t1024, slab, slab, slab, slab, slab, slab, t512,
                  pl.BlockSpec((1, D_ATTN), fix), pl.BlockSpec((1, D_RNN), fix),
                  pl.BlockSpec((D_MIX, D_MODEL), fix), pl.BlockSpec((1, D_MODEL), fix),
                  pl.BlockSpec((D_MODEL, 2 * LANES), fix), pl.BlockSpec((1, LANES), fix),
                  pl.BlockSpec((tm, tm), fix), pl.BlockSpec((LANES, LANES), fix)],
        out_specs=[t1024, t1024, t128, t128, pl.BlockSpec((8, LANES), row)],
        out_shape=[jax.ShapeDtypeStruct((m, D_MODEL), F32), jax.ShapeDtypeStruct((m, D_MODEL), BF16),
                   jax.ShapeDtypeStruct((m, LANES), F32), jax.ShapeDtypeStruct((m, LANES), F32),
                   jax.ShapeDtypeStruct((m // tm * 8, LANES), I32)],
        compiler_params=_cparams("parallel"),
    )(x2, o[0], o[1], o[2], lse[0], lse[1], lse[2], rnn, ga, gn, wo_bf, gf, wr_pair, br, tri, upper)


def _rows_copy(src, src_row, dst, dst_row, sem, rows=ROW_CHUNK):
    return pltpu.make_async_copy(src.at[pl.ds(src_row, rows)], dst.at[pl.ds(dst_row, rows)], sem)


def _for_each_chunk(i, ntot_ref, dst_ref, copy):
    n = ntot_ref[i]

    def body(j, carry):
        copy(pl.multiple_of(j * ROW_CHUNK, ROW_CHUNK), pl.multiple_of(dst_ref[i * TILE_CHUNKS + j], ROW_CHUNK))
        return carry

    lax.fori_loop(0, n, body, 0)
    return n


def _wait_chunks(count, wait_rows):
    def batch(c, carry):
        wait_rows(WAIT_BATCH * ROW_CHUNK)
        return carry

    def single(c, carry):
        wait_rows(ROW_CHUNK)
        return carry

    lax.fori_loop(0, count // WAIT_BATCH, batch, 0)
    lax.fori_loop(0, count % WAIT_BATCH, single, 0)


def _pack_bf16_pairs(x):
    c = x.shape[1] // 2
    lo = lax.bitcast_convert_type(x[:, :c], U32) >> 16
    hi = lax.bitcast_convert_type(x[:, c:], U32) & jnp.uint32(0xFFFF0000)
    return hi | lo


def _unpack_bf16_pairs(w):
    lo = lax.bitcast_convert_type(w << 16, F32).astype(BF16)
    hi = lax.bitcast_convert_type(w & jnp.uint32(0xFFFF0000), F32).astype(BF16)
    return lo, hi


def _dispatch_kernel(ntot_ref, dst_ref, tailn_ref, taildst_ref,
                     h_ref, pos_ref, gate_ref, rows_ref, sorted_buf, zero_buf, pending, sems, *, tm):
    i = pl.program_id(0)
    slot = i % 2
    pos_t = pos_ref[...].T
    gate_t = gate_ref[...].T
    row = lax.broadcasted_iota(I32, (TILE_ROWS, tm), 0).astype(F32)
    hit1 = row == pos_t[0:1, :]
    hit2 = row == pos_t[1:2, :]
    onehot = jnp.where(jnp.logical_or(hit1, hit2), 1.0, 0.0).astype(BF16)
    feat = jnp.dot(onehot, h_ref[...], preferred_element_type=F32)
    sorted_buf[slot, :, 0:ROW_WORDS - LANES] = _pack_bf16_pairs(feat)
    gate = jnp.sum(jnp.where(hit1, gate_t[0:1, :], 0.0) + jnp.where(hit2, gate_t[1:2, :], 0.0),
                   axis=-1, keepdims=True)
    lane = lax.broadcasted_iota(I32, (TILE_ROWS, LANES), 1)
    sorted_buf[slot, :, ROW_WORDS - LANES:] = jnp.where(lane == 0, lax.bitcast_convert_type(gate, U32), jnp.uint32(0))

    def wait_rows(sem):
        return lambda r: _rows_copy(sorted_buf.at[0], 0, rows_ref, 0, sem, r).wait()

    @pl.when(i > 0)
    def _():
        _wait_chunks(pending[0], wait_rows(sems.at[1 - slot]))

    def send(loc, dst):
        _rows_copy(sorted_buf.at[slot], loc, rows_ref, dst, sems.at[slot]).start()

    n_sent = _for_each_chunk(i, ntot_ref, dst_ref, send)

    @pl.when(i == 0)
    def _():
        zero_buf[...] = jnp.zeros_like(zero_buf)

    def tails():
        def per_expert(e, total):
            n = tailn_ref[e]
            dst0 = taildst_ref[e]
            big = n // WAIT_BATCH

            def batch(c, carry):
                _rows_copy(zero_buf, 0, rows_ref, pl.multiple_of(dst0 + c * (WAIT_BATCH * ROW_CHUNK), ROW_CHUNK),
                           sems.at[slot], WAIT_BATCH * ROW_CHUNK).start()
                return carry

            def single(c, carry):
                _rows_copy(zero_buf, 0, rows_ref, pl.multiple_of(dst0 + c * ROW_CHUNK, ROW_CHUNK),
                           sems.at[slot]).start()
                return carry

            lax.fori_loop(0, big, batch, 0)
            lax.fori_loop(big * WAIT_BATCH, n, single, 0)
            return total + n

        return lax.fori_loop(0, N_EXPERTS + 1, per_expert, jnp.int32(0))

    n_sent = n_sent + lax.cond(i == 0, tails, lambda: jnp.int32(0))
    pending[0] = n_sent

    @pl.when(i == pl.num_programs(0) - 1)
    def _():
        _wait_chunks(n_sent, wait_rows(sems.at[slot]))


def _dispatch(meta, h2, pos, gates, n_rows):
    m = h2.shape[0]
    tm = TOKEN_TILE
    tile = lambda i, *_: (i, 0)
    return pl.pallas_call(
        functools.partial(_dispatch_kernel, tm=tm),
        grid_spec=pltpu.PrefetchScalarGridSpec(
            num_scalar_prefetch=4,
            grid=(m // tm,),
            in_specs=[pl.BlockSpec((tm, D_MODEL), tile), pl.BlockSpec((tm, LANES), tile),
                      pl.BlockSpec((tm, LANES), tile)],
            out_specs=pl.BlockSpec(memory_space=pl.ANY),
            scratch_shapes=[pltpu.VMEM((2, TILE_ROWS, ROW_WORDS), U32),
                            pltpu.VMEM((WAIT_BATCH * ROW_CHUNK, ROW_WORDS), U32),
                            pltpu.SMEM((1,), I32), pltpu.SemaphoreType.DMA((2,))],
        ),
        out_shape=jax.ShapeDtypeStruct((n_rows, ROW_WORDS), U32),
        compiler_params=_cparams("arbitrary"),
    )(*meta, h2, pos, gates)


def _expert_kernel(be_ref, nused_ref, next_ref, valid_ref, rows_ref, wg_hbm, wu_hbm, wd_hbm, out_ref,
                   wgf, wuf, wdf, wgb, wub, wdb, holder, sems, *, layer):
    i = pl.program_id(0)
    used = i < nused_ref[0]
    expert = be_ref[i]
    first = jnp.logical_or(i == 0, expert != be_ref[jnp.maximum(i - 1, 0)])

    def weight_copies(e, slot):
        return (pltpu.make_async_copy(wg_hbm.at[layer, e], wgf.at[slot], sems.at[slot, 0]),
                pltpu.make_async_copy(wu_hbm.at[layer, e], wuf.at[slot], sems.at[slot, 1]),
                pltpu.make_async_copy(wd_hbm.at[layer, e], wdf.at[slot], sems.at[slot, 2]))

    @pl.when(jnp.logical_and(used, first))
    def _():
        @pl.when(i == 0)
        def _():
            holder[0] = 0
            for c in weight_copies(expert, 0):
                c.start()

        slot = holder[0]
        for c in weight_copies(expert, slot):
            c.wait()
        wgb[...] = wgf[slot].astype(BF16)
        wub[...] = wuf[slot].astype(BF16)
        wdb[...] = wdf[slot].astype(BF16)
        nxt = next_ref[i]

        @pl.when(nxt >= 0)
        def _():
            for c in weight_copies(nxt, 1 - slot):
                c.start()

        holder[0] = 1 - slot

    def mlp(r0, nrows):
        rows = rows_ref[r0:r0 + nrows, :]
        xb = jnp.concatenate(_unpack_bf16_pairs(rows[:, 0:ROW_WORDS - LANES]), axis=1)
        gate = lax.bitcast_convert_type(rows[:, ROW_WORDS - LANES:][:, 0:1], F32)
        g = jnp.dot(xb, wgb[...], preferred_element_type=F32)
        u = jnp.dot(xb, wub[...], preferred_element_type=F32)
        act = (g * jax.nn.sigmoid(g) * u).astype(BF16)
        y = jnp.dot(act, wdb[...], preferred_element_type=F32) * gate
        out_ref[r0:r0 + nrows, :] = _pack_bf16_pairs(y.astype(BF16).astype(F32))

    blk = rows_ref.shape[0]
    valid = valid_ref[i]

    @pl.when(valid == blk)
    def _():
        mlp(0, blk)

    @pl.when(valid < blk)
    def _():
        for r0 in range(0, blk, EXPERT_SUBBLOCK):
            @pl.when(r0 < valid)
            def _():
                mlp(r0, EXPERT_SUBBLOCK)

            @pl.when(r0 >= valid)
            def _():
                out_ref[r0:r0 + EXPERT_SUBBLOCK, :] = jnp.zeros((EXPERT_SUBBLOCK, out_ref.shape[1]), out_ref.dtype)


def _experts(block_expert, n_used, next_expert, valid, rows, w_gate, w_up, w_down, layer):
    n_rows = rows.shape[0]
    blk = EXPERT_BLOCK
    rmap = lambda i, be, nu, nx, va: (jnp.minimum(i, nu[0] - 1), 0)
    hbm = pl.BlockSpec(memory_space=pl.ANY)
    return pl.pallas_call(
        functools.partial(_expert_kernel, layer=layer),
        grid_spec=pltpu.PrefetchScalarGridSpec(
            num_scalar_prefetch=4,
            grid=(n_rows // blk,),
            in_specs=[pl.BlockSpec((blk, ROW_WORDS), rmap), hbm, hbm, hbm],
            out_specs=pl.BlockSpec((blk, D_MODEL // 2), lambda i, be, nu, nx, va: (i, 0)),
            scratch_shapes=[pltpu.VMEM((2, D_MODEL, D_EXPERT), F32), pltpu.VMEM((2, D_MODEL, D_EXPERT), F32),
                            pltpu.VMEM((2, D_EXPERT, D_MODEL), F32),
                            pltpu.VMEM((D_MODEL, D_EXPERT), BF16), pltpu.VMEM((D_MODEL, D_EXPERT), BF16),
                            pltpu.VMEM((D_EXPERT, D_MODEL), BF16),
                            pltpu.SMEM((1,), I32), pltpu.SemaphoreType.DMA((2, 3))],
        ),
        out_shape=jax.ShapeDtypeStruct((n_rows, D_MODEL // 2), U32),
        compiler_params=_cparams("arbitrary"),
    )(block_expert, n_used, next_expert, valid, rows, w_gate, w_up, w_down)


def _proj_section(hb, w_ref, n):
    return jnp.dot(hb, w_ref[:, n * D_ATTN:(n + 1) * D_ATTN], preferred_element_type=F32)


def _proj_qkv(hb, w_ref, qg_ref, kg_ref, gm_ref, q_ref, k_ref, v_ref):
    sec = functools.partial(_proj_section, hb, w_ref)

    def head_norm(z, gain):
        zz = (z * z).astype(BF16)
        half = gm_ref.shape[0]
        ms = jnp.concatenate([jnp.dot(zz[:, c:c + half], gm_ref[...], preferred_element_type=F32)
                              for c in range(0, D_ATTN, half)], axis=1)
        return z * lax.rsqrt(ms + EPS) * gain

    def put_slabs(ref, z):
        for hp in range(HEAD_PAIRS):
            ref[hp] = z[:, hp * LANES:(hp + 1) * LANES]

    put_slabs(q_ref, head_norm(sec(0), qg_ref[...]))
    put_slabs(k_ref, head_norm(sec(1), kg_ref[...]))
    put_slabs(v_ref, sec(2))


def _rglru_gates(xr, gr, cw_ref, cb_ref, wa_ref, ba_ref, wx_ref, bx_ref, lam_ref, xbuf, a_scr, b_scr, g_scr):
    tt = xr.shape[0]
    xbuf[8:8 + tt, :] = xr
    g_scr[...] = jax.nn.gelu(gr, approximate=True)
    u = cb_ref[...] + cw_ref[3:4, :] * xr
    for back in range(1, CONV_WIDTH):
        u = u + cw_ref[3 - back:4 - back, :] * xbuf[8 - back:8 - back + tt, :]
    xbuf[0:8, :] = xbuf[tt:tt + 8, :]

    ub = u.astype(BF16)
    r = jax.nn.sigmoid(jnp.dot(ub, wa_ref[...], preferred_element_type=F32) + ba_ref[...])
    gi = jax.nn.sigmoid(jnp.dot(ub, wx_ref[...], preferred_element_type=F32) + bx_ref[...])
    nl = -lam_ref[...]
    softplus = jnp.maximum(nl, 0.0) + jnp.log1p(jnp.exp(-jnp.abs(nl)))
    log_a = (-RG_C) * r * softplus
    a = jnp.exp(log_a)
    a_scr[...] = a
    b_scr[...] = jnp.sqrt(-jnp.tanh(log_a) * (a * a + 1.0)) * (gi * u)


def _rglru_scan(out_ref, a_scr, b_scr, h_scr, g_scr):
    tt = out_ref.shape[0]
    rc = RNN_CHUNK
    row = lax.broadcasted_iota(I32, (rc, LANES), 0)

    def chunk(c, carry):
        r0 = pl.multiple_of(c * rc, rc)
        for g in range(D_RNN // LANES):
            ls = slice(g * LANES, (g + 1) * LANES)
            aa = a_scr[pl.ds(r0, rc), ls]
            bb = b_scr[pl.ds(r0, rc), ls]
            k = 1
            while k < rc:
                keep = row >= k
                a_sh = pltpu.roll(aa, k, 0)
                b_sh = pltpu.roll(bb, k, 0)
                bb = jnp.where(keep, aa * b_sh + bb, bb)
                aa = jnp.where(keep, aa * a_sh, aa)
                k *= 2
            h = aa * h_scr[0:1, ls] + bb
            h_scr[0:1, ls] = h[rc - 1:rc, :]
            out_ref[pl.ds(r0, rc), ls] = h * g_scr[pl.ds(r0, rc), ls]
        return carry

    lax.fori_loop(0, tt // rc, chunk, 0)


def _combine_body(ntot_ref, dst_ref, x1_ref, pos_ref, rows_ref, local, pending, sems):
    i = pl.program_id(0)
    slot = i % 2
    tm = x1_ref.shape[0]

    def fetch_tile(t, s):
        def fetch(loc, src):
            _rows_copy(rows_ref, src, local.at[s], loc, sems.at[s]).start()

        return _for_each_chunk(t, ntot_ref, dst_ref, fetch)

    @pl.when(i == 0)
    def _():
        local[...] = jnp.zeros_like(local)
        pending[0] = fetch_tile(0, 0)

    @pl.when(i + 1 < pl.num_programs(0))
    def _():
        pending[1 - slot] = fetch_tile(i + 1, 1 - slot)

    _wait_chunks(pending[slot], lambda r: _rows_copy(rows_ref, 0, local.at[0], 0, sems.at[slot], r).wait())

    pos = pos_ref[...]
    col = lax.broadcasted_iota(I32, (tm, TILE_ROWS), 1).astype(F32)
    pick = jnp.where(jnp.logical_or(col == pos[:, 0:1], col == pos[:, 1:2]), 1.0, 0.0).astype(BF16)
    lo, hi = _unpack_bf16_pairs(local[slot])
    moe = jnp.concatenate([jnp.dot(pick, lo, preferred_element_type=F32),
                           jnp.dot(pick, hi, preferred_element_type=F32)], axis=1)
    return x1_ref[...] + moe


N_PROJ_IN = 12


def _edge_kernel(*refs, with_combine, with_proj, tiles_per_seq):
    refs = list(refs)
    take = lambda n: [refs.pop(0) for _ in range(n)]
    if with_combine:
        ntot_ref, dst_ref = take(2)
    (x_ref,) = take(1)
    if with_combine:
        pos_ref, rows_ref = take(2)
    if with_proj:
        g_ref, w_ref, qg_ref, kg_ref, gm_ref, cw_ref, cb_ref, wa_ref, ba_ref, wx_ref, bx_ref, lam_ref = take(N_PROJ_IN)
    if with_combine:
        (x2_ref,) = take(1)
    if with_proj:
        q_ref, k_ref, v_ref, rnn_ref = take(4)
    if with_combine:
        local, pending, sems = take(3)
    if with_proj:
        xbuf, a_scr, b_scr, h_scr, g_scr = take(5)

    if with_proj:
        @pl.when(pl.program_id(0) % tiles_per_seq == 0)
        def _():
            xbuf[0:8, :] = jnp.zeros((8, D_RNN), F32)
            h_scr[...] = jnp.zeros_like(h_scr)

    if with_combine:
        x = _combine_body(ntot_ref, dst_ref, x_ref, pos_ref, rows_ref, local, pending, sems)
        x2_ref[...] = x
    else:
        x = x_ref[...]
    if with_proj:
        hb = _rms(x, g_ref[...]).astype(BF16)
        _rglru_gates(_proj_section(hb, w_ref, 3), _proj_section(hb, w_ref, 4), cw_ref, cb_ref, wa_ref, ba_ref,
                     wx_ref, bx_ref, lam_ref, xbuf, a_scr, b_scr, g_scr)
        _proj_qkv(hb, w_ref, qg_ref, kg_ref, gm_ref, q_ref, k_ref, v_ref)
        _rglru_scan(rnn_ref, a_scr, b_scr, h_scr, g_scr)


def _edge(x_in, seq_len, combine_in=None, proj_in=None):
    m = x_in.shape[0]
    tm = TOKEN_TILE
    with_combine, with_proj = combine_in is not None, proj_in is not None
    tile = lambda i, *_: (i, 0)
    fix = lambda i, *_: (0, 0)
    operands, in_specs, out_specs, out_shape, scratch = [], [], [], [], []
    n_prefetch = 0
    if with_combine:
        chunk_meta, pos, out_rows = combine_in
        operands += list(chunk_meta)
        n_prefetch = len(chunk_meta)
    operands.append(x_in)
    in_specs.append(pl.BlockSpec((tm, D_MODEL), tile))
    if with_combine:
        operands += [pos, out_rows]
        in_specs += [pl.BlockSpec((tm, LANES), tile), pl.BlockSpec(memory_space=pl.ANY)]
        out_specs.append(pl.BlockSpec((tm, D_MODEL), tile))
        out_shape.append(jax.ShapeDtypeStruct((m, D_MODEL), F32))
        scratch += [pltpu.VMEM((2, TILE_ROWS, D_MODEL // 2), U32), pltpu.SMEM((2,), I32),
                    pltpu.SemaphoreType.DMA((2,))]
    if with_proj:
        assert len(proj_in) == N_PROJ_IN
        operands += list(proj_in)
        in_specs += [pl.BlockSpec(a.shape, fix) for a in proj_in]
        slab = pl.BlockSpec((HEAD_PAIRS, tm, LANES), lambda i, *_: (0, i, 0))
        out_specs += [slab, slab, slab, pl.BlockSpec((tm, D_RNN), tile)]
        out_shape += [jax.ShapeDtypeStruct((HEAD_PAIRS, m, LANES), F32)] * 3 + [jax.ShapeDtypeStruct((m, D_RNN), F32)]
        scratch += [pltpu.VMEM((tm + 8, D_RNN), F32), pltpu.VMEM((tm, D_RNN), F32), pltpu.VMEM((tm, D_RNN), F32),
                    pltpu.VMEM((8, D_RNN), F32), pltpu.VMEM((tm, D_RNN), F32)]
    return pl.pallas_call(
        functools.partial(_edge_kernel, with_combine=with_combine, with_proj=with_proj,
                          tiles_per_seq=seq_len // tm),
        grid_spec=pltpu.PrefetchScalarGridSpec(
            num_scalar_prefetch=n_prefetch, grid=(m // tm,),
            in_specs=in_specs, out_specs=out_specs, scratch_shapes=scratch),
        out_shape=out_shape,
        compiler_params=_cparams("arbitrary"),
    )(*operands)


def _dispatch_plan(cnt_tiles, n_tiles, n_blocks):
    cnt = cnt_tiles.reshape(n_tiles, 8, LANES)[:, 0, ROUTER_LANE0:ROUTER_LANE0 + N_EXPERTS]
    seg = (cnt + ROW_CHUNK - 1) // ROW_CHUNK * ROW_CHUNK
    used = jnp.sum(seg, axis=0)
    padded = (used + EXPERT_BLOCK - 1) // EXPERT_BLOCK * EXPERT_BLOCK
    p_ends = jnp.cumsum(padded)
    p_starts = p_ends - padded
    seg_dst = p_starts[None, :] + jnp.cumsum(seg, axis=0) - seg
    loc_start = jnp.cumsum(seg, axis=1) - seg
    flat = lambda a: a.reshape(-1).astype(I32)
    block_row = jnp.arange(n_blocks, dtype=I32) * EXPERT_BLOCK
    block_expert = jnp.minimum(jnp.sum((p_ends[None, :] <= block_row[:, None]).astype(I32), axis=1), N_EXPERTS - 1)
    n_used = (p_ends[-1:] // EXPERT_BLOCK).astype(I32)
    after = (p_ends // EXPERT_BLOCK).astype(I32)[block_expert]
    next_expert = jnp.where(after < n_used[0], block_expert[jnp.minimum(after, n_blocks - 1)], -1).astype(I32)
    ends = jnp.cumsum(seg // ROW_CHUNK, axis=1)
    j = jnp.arange(TILE_CHUNKS, dtype=I32)
    owner = jnp.minimum(jnp.sum((ends[:, None, :] <= j[None, :, None]).astype(I32), axis=2), N_EXPERTS - 1)
    is_owner = owner[:, :, None] == jnp.arange(N_EXPERTS, dtype=I32)[None, None, :]
    chunk_dst = jnp.sum(jnp.where(is_owner, (seg_dst - loc_start)[:, None, :], 0), axis=2) + ROW_CHUNK * j[None, :]
    chunk_meta = (flat(ends[:, -1]), flat(chunk_dst))
    rest = n_blocks * EXPERT_BLOCK - p_ends[-1:]
    tail_meta = (flat(jnp.concatenate([padded - used, rest]) // ROW_CHUNK),
                 flat(jnp.concatenate([p_starts + used, p_ends[-1:]])))
    occupied_end = jnp.sum(jnp.where(block_expert[:, None] == jnp.arange(N_EXPERTS, dtype=I32)[None, :],
                                     (p_starts + used)[None, :], 0), axis=1)
    valid = jnp.clip(occupied_end - block_row, 0, EXPERT_BLOCK).astype(I32)
    return chunk_meta, tail_meta, (block_expert.astype(I32), n_used, next_expert, valid)


def _constants():
    gm = np.kron(np.eye(MXU_DEPTH // HEAD_DIM), np.full((HEAD_DIM, HEAD_DIM), 1.0 / HEAD_DIM))
    tri = np.tril(np.ones((TOKEN_TILE, TOKEN_TILE), np.float32), -1)
    upper = np.triu(np.ones((LANES, LANES), np.float32), 1)
    return jnp.asarray(gm, BF16), jnp.asarray(tri, BF16), jnp.asarray(upper, BF16)


def kernel(x, rel_bias_table, norm_mix, w_in, q_norm, k_norm, conv_w, conv_b, rg_w_a, rg_b_a, rg_w_x, rg_b_x,
           rg_lambda, norm_attn_out, norm_rnn_out, w_out, norm_ffn, router_group_w, router_group_b,
           router_expert_w, router_expert_b, expert_w_gate, expert_w_up, expert_w_down):
    b, s, d = x.shape
    depth = w_in.shape[0]
    m = b * s
    assert d == D_MODEL and s % ATTN_TILE == 0 and s % TOKEN_TILE == 0

    gm, tri, upper = _constants()
    bias = _bias_tables(rel_bias_table)
    scale = HEAD_DIM ** -0.5
    n_tiles = m // TOKEN_TILE
    n_blocks = -(-(m * TOP_K + n_tiles * N_EXPERTS * (ROW_CHUNK - 1)) // EXPERT_BLOCK) + N_EXPERTS
    n_rows = n_blocks * EXPERT_BLOCK
    row1 = lambda v: v.reshape(1, -1).astype(F32)

    def proj_params(l):
        return (row1(norm_mix[l]), w_in[l].astype(BF16),
                row1(jnp.tile(q_norm[l], ATTN_HEADS) * (scale * LOG2E)), row1(jnp.tile(k_norm[l], ATTN_HEADS)), gm,
                conv_w[l].astype(F32), row1(conv_b[l]),
                _block_diag(rg_w_a[l]).astype(BF16), row1(rg_b_a[l]),
                _block_diag(rg_w_x[l]).astype(BF16), row1(rg_b_x[l]), row1(rg_lambda[l]))

    x2 = x.reshape(m, d).astype(F32)
    q, k, v, rnn = _edge(x2, s, proj_in=proj_params(0))
    for l in range(depth):
        outs, lses = [], []
        for n, (_, dil) in enumerate(DILATED_BRANCHES):
            o, lse = _attention_branch(q, k, v, bias[n], b, s, dil)
            outs.append(o)
            lses.append(lse)

        wr = jnp.zeros((D_MODEL, LANES), F32)
        wr = wr.at[:, :N_GROUPS].set(router_group_w[l]).at[:, N_GROUPS:N_GROUPS + N_EXPERTS].set(router_expert_w[l])
        wr_pair = jnp.concatenate(_split_bf16(wr), axis=1)
        br = jnp.zeros((1, LANES), F32)
        br = br.at[0, :N_GROUPS].set(router_group_b[l]).at[0, N_GROUPS:N_GROUPS + N_EXPERTS].set(router_expert_b[l])
        x1, h2, pos, gates, cnt = _mix(x2, outs, lses, rnn, row1(norm_attn_out[l]), row1(norm_rnn_out[l]),
                                       w_out[l].astype(BF16), row1(norm_ffn[l]), wr_pair, br, tri, upper)

        chunk_meta, tail_meta, block_meta = _dispatch_plan(cnt, n_tiles, n_blocks)
        rows = _dispatch(chunk_meta + tail_meta, h2, pos, gates, n_rows)
        out_rows = _experts(*block_meta, rows, expert_w_gate, expert_w_up, expert_w_down, l)
        nxt = proj_params(l + 1) if l + 1 < depth else None
        x2, *started = _edge(x1, s, combine_in=(chunk_meta, pos, out_rows), proj_in=nxt)
        if started:
            q, k, v, rnn = started
    return x2.reshape(b, s, d).astype(x.dtype)
```

```python
import functools
import math

import numpy as np
import jax
import jax.numpy as jnp
from jax import lax
from jax.experimental import pallas as pl
from jax.experimental.pallas import tpu as pltpu

F32 = jnp.float32
BF16 = jnp.bfloat16
I32 = jnp.int32
U32 = jnp.uint32

D_MODEL = 1024
ATTN_HEADS = 8
HEAD_DIM = 64
D_ATTN = ATTN_HEADS * HEAD_DIM
RNN_BLOCKS = 8
D_RNN = 512
D_MIX = D_ATTN + D_RNN
D_IN = 3 * D_ATTN + 2 * D_RNN
DILATED_BRANCHES = ((128, 1), (512, 4), (2048, 16))
Q_BLOCK = 128
REL_BUCKETS = 32
REL_MAX_DIST = 2048
CONV_WIDTH = 4
RG_C = 8.0
N_GROUPS = 4
EXPERTS_PER_GROUP = 8
N_EXPERTS = N_GROUPS * EXPERTS_PER_GROUP
TOP_K = 2
D_EXPERT = 512
EPS = 1e-6
NEG_INF = -1e30
LOG2E = math.log2(math.e)

LANES = 128
SUBLANES = 8
MXU_DEPTH = 256
HEAD_PAIRS = D_ATTN // LANES
ROUTER_LANE0 = N_GROUPS
TOKEN_TILE = 512
ATTN_TILE = 2048
ATTN_PAIRS = 2
ATTN_UNROLL = 16
RNN_CHUNK = 64
EXPERT_SUBBLOCK = 128
EXPERT_BLOCK = 512
ROW_CHUNK = 8
TILE_ROWS = -(-(TOKEN_TILE * TOP_K + N_EXPERTS * (ROW_CHUNK - 1)) // 256) * 256
TILE_CHUNKS = TILE_ROWS // ROW_CHUNK
WAIT_BATCH = 16
ROW_WORDS = D_MODEL // 2 + LANES
MERGE_ROWS = 256
VMEM_LIMIT = 48 * 1024 * 1024
ATTN_VMEM_LIMIT = 56 * 1024 * 1024


def _cparams(*sem):
    return pltpu.CompilerParams(dimension_semantics=sem, vmem_limit_bytes=VMEM_LIMIT)


def _rms(x, gain):
    return x * lax.rsqrt(jnp.mean(x * x, axis=-1, keepdims=True) + EPS) * gain


def _rows(start, size, stride):
    return pl.ds(start, size) if stride == 1 else pl.ds(start, size, stride=stride)


def _attn_kernel(q_ref, k_ref, v_ref, bias_ref, out_ref, qd, stage, o_scr, l_scr, *kv_scr):
    last = len(DILATED_BRANCHES) - 1
    for p in range(ATTN_PAIRS):
        for n, (_, dil) in enumerate(DILATED_BRANCHES):
            o_dst = out_ref.at[p] if n == last else o_scr.at[n, p]
            _attn_pair(q_ref.at[p], k_ref.at[p], v_ref.at[p], bias_ref.at[n], 2 * p, o_dst, l_scr.at[n, p],
                       qd, kv_scr[2 * n].at[p], kv_scr[2 * n + 1].at[p], stage, dil)

        def merge(c, carry):
            rows = pl.ds(pl.multiple_of(c * MERGE_ROWS, MERGE_ROWS), MERGE_ROWS)
            lse = [l_scr[n, p, rows, :] for n in range(last + 1)]
            mx = functools.reduce(jnp.maximum, lse)
            w = [jnp.exp2(l - mx) for l in lse]
            o = [o_scr[n, p, rows, :] for n in range(last)] + [out_ref[p, rows, :]]
            num = functools.reduce(jnp.add, [wi * oi for wi, oi in zip(w, o)])
            out_ref[p, rows, :] = num * (1.0 / functools.reduce(jnp.add, w))
            return carry

        lax.fori_loop(0, out_ref.shape[1] // MERGE_ROWS, merge, 0)


def _attn_pair(q_ref, k_ref, v_ref, bias_ref, head0, o_ref, lse_ref, qd, kd, vd, stage, dil):
    i = pl.program_id(2)
    tile = q_ref.shape[0]
    rows = tile // dil
    seg = Q_BLOCK + rows

    @pl.when(i == 0)
    def _():
        for r in range(dil):
            kd[r * seg:r * seg + Q_BLOCK, :] = jnp.zeros((Q_BLOCK, LANES), BF16)
            vd[r * seg:r * seg + Q_BLOCK, :] = jnp.zeros((Q_BLOCK, LANES), BF16)

    @pl.when(i > 0)
    def _():
        for r in range(dil):
            kd[r * seg:r * seg + Q_BLOCK, :] = kd[r * seg + rows:(r + 1) * seg, :]
            vd[r * seg:r * seg + Q_BLOCK, :] = vd[r * seg + rows:(r + 1) * seg, :]

    def regroup(src_ref, dst, base, pitch):
        if dil % 16 == 0:
            quarter = tile // 4
            for c in range(4):
                stage[c * quarter:(c + 1) * quarter, :] = src_ref[pl.ds(c, quarter, stride=4), :]
            for r in range(dil):
                picked = stage[_rows((r % 4) * quarter + r // 4, rows, dil // 4), :]
                dst[base + r * pitch:base + r * pitch + rows, :] = picked.astype(BF16)
        else:
            for r in range(dil):
                dst[base + r * pitch:base + r * pitch + rows, :] = src_ref[_rows(r, rows, dil), :].astype(BF16)

    regroup(q_ref, qd, 0, rows)
    regroup(k_ref, kd, Q_BLOCK, seg)
    regroup(v_ref, vd, Q_BLOCK, seg)

    per = rows // Q_BLOCK
    lane = lax.broadcasted_iota(I32, (Q_BLOCK, LANES), 1)
    low = lane < HEAD_DIM

    def block(blk, carry):
        r = blk // per
        n = blk % per
        q0 = pl.multiple_of(r * rows + n * Q_BLOCK, Q_BLOCK)
        k0 = pl.multiple_of(r * seg + n * Q_BLOCK, Q_BLOCK)
        sel = jnp.where(jnp.logical_and(i == 0, n == 0), 1, 0)
        q = qd[pl.ds(q0, Q_BLOCK), :]
        kk = kd[pl.ds(k0, 2 * Q_BLOCK), :]
        vv = vd[pl.ds(k0, 2 * Q_BLOCK), :]
        pvs, maxes, sums = [], [], []
        for hh in range(2):
            qm = jnp.where(low if hh == 0 else jnp.logical_not(low), q, jnp.zeros_like(q))
            s = lax.dot_general(qm, kk, (((1,), (1,)), ((), ())), preferred_element_type=F32)
            s = s + bias_ref[sel, head0 + hh]
            mx = jnp.max(s, axis=-1, keepdims=True)
            p = jnp.exp2(s - mx)
            sums.append(jnp.sum(p, axis=-1, keepdims=True))
            maxes.append(mx)
            pvs.append(jnp.dot(p.astype(BF16), vv, preferred_element_type=F32))
        l = jnp.where(low, sums[0], sums[1])
        dst = _rows(r + dil * Q_BLOCK * n, Q_BLOCK, dil)
        o_ref[dst, :] = jnp.where(low, pvs[0], pvs[1]) * (1.0 / l)
        lse_ref[dst, :] = jnp.where(low, maxes[0], maxes[1]) + jnp.log(l) * LOG2E
        return carry

    lax.fori_loop(0, dil * per, block, 0, unroll=ATTN_UNROLL)


def _attention(q, k, v, bias, b, s):
    tile = ATTN_TILE
    nt = s // tile
    nb = len(DILATED_BRANCHES)
    blk = pl.BlockSpec((ATTN_PAIRS, tile, LANES), lambda bb, g, i: (g, bb * nt + i, 0))
    kv_scr = []
    for _, dil in DILATED_BRANCHES:
        keys = tile + Q_BLOCK * dil
        kv_scr += [pltpu.VMEM((ATTN_PAIRS, keys, LANES), BF16)] * 2
    return pl.pallas_call(
        _attn_kernel,
        grid=(b, HEAD_PAIRS // ATTN_PAIRS, nt),
        in_specs=[blk, blk, blk,
                  pl.BlockSpec((nb, 2, 2 * ATTN_PAIRS, Q_BLOCK, 2 * Q_BLOCK), lambda bb, g, i: (0, 0, g, 0, 0))],
        out_specs=blk,
        out_shape=jax.ShapeDtypeStruct(q.shape, F32),
        scratch_shapes=[pltpu.VMEM((tile, LANES), BF16), pltpu.VMEM((tile, LANES), F32),
                        pltpu.VMEM((nb - 1, ATTN_PAIRS, tile, LANES), F32),
                        pltpu.VMEM((nb, ATTN_PAIRS, tile, LANES), F32)] + kv_scr,
        compiler_params=pltpu.CompilerParams(dimension_semantics=("parallel", "parallel", "arbitrary"),
                                             vmem_limit_bytes=ATTN_VMEM_LIMIT),
    )(q, k, v, bias)


def _bias_step_tables():
    exact = REL_BUCKETS // 2
    i = np.arange(Q_BLOCK)[:, None]
    j = np.arange(2 * Q_BLOCK)[None, :]
    steps = i + Q_BLOCK - j
    onehot, band = [], []
    for window, dil in DILATED_BRANCHES:
        dist = (np.arange(2 * Q_BLOCK) * dil).astype(np.int32)
        d = np.maximum(dist, 1).astype(np.float32)
        log_b = exact + (np.log(d / np.float32(exact)) / np.float32(math.log(REL_MAX_DIST / exact))
                         * np.float32(REL_BUCKETS - exact)).astype(np.int32)
        bucket = np.where(dist < exact, dist, np.minimum(log_b, REL_BUCKETS - 1))
        onehot.append(np.eye(REL_BUCKETS, dtype=np.float32)[bucket])
        band.append((steps >= 0) & (steps <= window // dil))
    first = np.broadcast_to(j >= Q_BLOCK, (Q_BLOCK, 2 * Q_BLOCK))
    return np.stack(onehot), np.stack(band), first


def _bias_tables(rel_table):
    onehot, band, first = _bias_step_tables()
    nb = len(DILATED_BRANCHES)
    vec = LOG2E * jnp.einsum('nsb,bh->nhs', onehot, rel_table.astype(F32), precision=lax.Precision.HIGHEST)
    width = 3 * Q_BLOCK
    ext = jnp.concatenate([jnp.zeros((nb, ATTN_HEADS, Q_BLOCK), F32), vec[..., ::-1]], axis=-1)
    hank = jnp.tile(ext, (1, 1, Q_BLOCK + 1))[..., :Q_BLOCK * (width + 1)].reshape(nb, ATTN_HEADS, Q_BLOCK, width + 1)
    bias = hank[:, :, ::-1, Q_BLOCK:width]
    regular = jnp.where(band[:, None], bias, NEG_INF)
    start = jnp.where((band & first[None])[:, None], bias, NEG_INF)
    return jnp.stack([regular, start], axis=1)


def _block_diag(w):
    nb, n, _ = w.shape
    eye = jnp.eye(nb, dtype=w.dtype)
    return (eye[:, None, :, None] * w[:, :, None, :]).reshape(nb * n, nb * n)


def _split_bf16(x):
    hi = x.astype(BF16)
    return hi, (x - hi.astype(F32)).astype(BF16)


def _mix_kernel(x_ref, attn_ref, rnn_ref, ga_ref, gn_ref, wo_ref, gf_ref,
                wr_ref, br_ref, tri_ref, upper_ref,
                x1_ref, h2_ref, pos_ref, gate_ref, cnt_ref):
    slabs = [attn_ref[hp] for hp in range(HEAD_PAIRS)]
    sumsq = functools.reduce(jnp.add, [jnp.sum(a * a, axis=-1, keepdims=True) for a in slabs])
    scale = lax.rsqrt(sumsq * (1.0 / D_ATTN) + EPS)
    na = (jnp.concatenate([a * scale for a in slabs], axis=1) * ga_ref[...]).astype(BF16)
    nr = _rms(rnn_ref[...], gn_ref[...]).astype(BF16)
    x1 = (x_ref[...] + jnp.dot(na, wo_ref[0:D_ATTN, :], preferred_element_type=F32)
          + jnp.dot(nr, wo_ref[D_ATTN:, :], preferred_element_type=F32))
    x1_ref[...] = x1
    h2 = _rms(x1, gf_ref[...])
    h2_ref[...] = h2.astype(BF16)

    hh, hl = _split_bf16(h2)
    by_hi = jnp.dot(hh, wr_ref[...], preferred_element_type=F32)
    by_lo = jnp.dot(hl, wr_ref[...], preferred_element_type=F32)
    logits = by_hi[:, :LANES] + by_hi[:, LANES:] + by_lo[:, :LANES] + br_ref[...]
    tm = logits.shape[0]
    lane = lax.broadcasted_iota(I32, (tm, LANES), 1)
    lanef = lane.astype(F32)
    big = float(LANES)

    def top(vals):
        m = jnp.max(vals, axis=-1, keepdims=True)
        return m, jnp.min(jnp.where(vals == m, lanef, big), axis=-1, keepdims=True)

    is_group = lane < N_GROUPS
    gmax, gsel = top(jnp.where(is_group, logits, NEG_INF))
    g_w = 1.0 / jnp.sum(jnp.where(is_group, jnp.exp(logits - gmax), 0.0), axis=-1, keepdims=True)
    lo_lane = ROUTER_LANE0 + EXPERTS_PER_GROUP * gsel
    in_group = jnp.logical_and(lanef >= lo_lane, lanef < lo_lane + EXPERTS_PER_GROUP)
    el = jnp.where(in_group, logits, NEG_INF)
    v1, i1 = top(el)
    v2, i2 = top(jnp.where(lanef == i1, NEG_INF, el))
    t = jnp.exp(v2 - v1)
    p1 = 1.0 / (1.0 + t)
    gate1 = g_w * p1
    gate2 = g_w * (t * p1)

    oh1 = (lanef == i1).astype(F32)
    oh2 = (lanef == i2).astype(F32)
    cnt = oh1 + oh2
    prefix = jnp.dot(tri_ref[...], cnt.astype(BF16), preferred_element_type=F32)
    total = jnp.sum(cnt, axis=0, keepdims=True)
    chunks = jnp.floor((total + (ROW_CHUNK - 1)) * (1.0 / ROW_CHUNK))
    seg_start = ROW_CHUNK * jnp.dot(jnp.broadcast_to(chunks, (8, LANES)).astype(BF16), upper_ref[...],
                                    preferred_element_type=F32)[0:1, :]
    base = seg_start + prefix
    pos1 = jnp.sum(oh1 * base, axis=-1, keepdims=True)
    pos2 = jnp.sum(oh2 * base, axis=-1, keepdims=True)
    cnt_ref[...] = jnp.broadcast_to(total, cnt_ref.shape).astype(I32)
    pos_ref[...] = jnp.where(lane == 0, pos1, jnp.where(lane == 1, pos2, 0.0))
    gate_ref[...] = jnp.where(lane == 0, gate1, jnp.where(lane == 1, gate2, 0.0))


def _mix(x2, attn, rnn, ga, gn, wo_bf, gf, wr_pair, br, tri, upper):
    m = x2.shape[0]
    tm = TOKEN_TILE
    row = lambda i: (i, 0)
    fix = lambda i: (0, 0)
    t512 = pl.BlockSpec((tm, D_ATTN), row)
    t128 = pl.BlockSpec((tm, LANES), row)
    t1024 = pl.BlockSpec((tm, D_MODEL), row)
    slab = pl.BlockSpec((HEAD_PAIRS, tm, LANES), lambda i: (0, i, 0))
    return pl.pallas_call(
        _mix_kernel,
        grid=(m // tm,),
        in_specs=[t1024, slab, t512,
                  pl.BlockSpec((1, D_ATTN), fix), pl.BlockSpec((1, D_RNN), fix),
                  pl.BlockSpec((D_MIX, D_MODEL), fix), pl.BlockSpec((1, D_MODEL), fix),
                  pl.BlockSpec((D_MODEL, 2 * LANES), fix), pl.BlockSpec((1, LANES), fix),
                  pl.BlockSpec((tm, tm), fix), pl.BlockSpec((LANES, LANES), fix)],
        out_specs=[t1024, t1024, t128, t128, pl.BlockSpec((8, LANES), row)],
        out_shape=[jax.ShapeDtypeStruct((m, D_MODEL), F32), jax.ShapeDtypeStruct((m, D_MODEL), BF16),
                   jax.ShapeDtypeStruct((m, LANES), F32), jax.ShapeDtypeStruct((m, LANES), F32),
                   jax.ShapeDtypeStruct((m // tm * 8, LANES), I32)],
        compiler_params=_cparams("parallel"),
    )(x2, attn, rnn, ga, gn, wo_bf, gf, wr_pair, br, tri, upper)


def _rows_copy(src, src_row, dst, dst_row, sem, rows=ROW_CHUNK):
    return pltpu.make_async_copy(src.at[pl.ds(src_row, rows)], dst.at[pl.ds(dst_row, rows)], sem)


def _for_each_chunk(i, ntot_ref, dst_ref, copy):
    n = ntot_ref[i]

    def body(j, carry):
        copy(pl.multiple_of(j * ROW_CHUNK, ROW_CHUNK), pl.multiple_of(dst_ref[i * TILE_CHUNKS + j], ROW_CHUNK))
        return carry

    lax.fori_loop(0, n, body, 0)
    return n


def _wait_chunks(count, wait_rows):
    def batch(c, carry):
        wait_rows(WAIT_BATCH * ROW_CHUNK)
        return carry

    def single(c, carry):
        wait_rows(ROW_CHUNK)
        return carry

    lax.fori_loop(0, count // WAIT_BATCH, batch, 0)
    lax.fori_loop(0, count % WAIT_BATCH, single, 0)


def _pack_bf16_pairs(x):
    c = x.shape[1] // 2
    lo = lax.bitcast_convert_type(x[:, :c], U32) >> 16
    hi = lax.bitcast_convert_type(x[:, c:], U32) & jnp.uint32(0xFFFF0000)
    return hi | lo


def _unpack_bf16_pairs(w):
    lo = lax.bitcast_convert_type(w << 16, F32).astype(BF16)
    hi = lax.bitcast_convert_type(w & jnp.uint32(0xFFFF0000), F32).astype(BF16)
    return lo, hi


def _dispatch_kernel(ntot_ref, dst_ref, tailn_ref, taildst_ref,
                     h_ref, pos_ref, gate_ref, rows_ref, sorted_buf, zero_buf, pending, sems, *, tm):
    i = pl.program_id(0)
    slot = i % 2
    pos_t = pos_ref[...].T
    gate_t = gate_ref[...].T
    row = lax.broadcasted_iota(I32, (TILE_ROWS, tm), 0).astype(F32)
    hit1 = row == pos_t[0:1, :]
    hit2 = row == pos_t[1:2, :]
    onehot = jnp.where(jnp.logical_or(hit1, hit2), 1.0, 0.0).astype(BF16)
    feat = jnp.dot(onehot, h_ref[...], preferred_element_type=F32)
    sorted_buf[slot, :, 0:ROW_WORDS - LANES] = _pack_bf16_pairs(feat)
    gate = jnp.sum(jnp.where(hit1, gate_t[0:1, :], 0.0) + jnp.where(hit2, gate_t[1:2, :], 0.0),
                   axis=-1, keepdims=True)
    lane = lax.broadcasted_iota(I32, (TILE_ROWS, LANES), 1)
    sorted_buf[slot, :, ROW_WORDS - LANES:] = jnp.where(lane == 0, lax.bitcast_convert_type(gate, U32), jnp.uint32(0))

    def wait_rows(sem):
        return lambda r: _rows_copy(sorted_buf.at[0], 0, rows_ref, 0, sem, r).wait()

    @pl.when(i > 0)
    def _():
        _wait_chunks(pending[0], wait_rows(sems.at[1 - slot]))

    def send(loc, dst):
        _rows_copy(sorted_buf.at[slot], loc, rows_ref, dst, sems.at[slot]).start()

    n_sent = _for_each_chunk(i, ntot_ref, dst_ref, send)

    @pl.when(i == 0)
    def _():
        zero_buf[...] = jnp.zeros_like(zero_buf)

    def tails():
        def per_expert(e, total):
            n = tailn_ref[e]
            dst0 = taildst_ref[e]
            big = n // WAIT_BATCH

            def batch(c, carry):
                _rows_copy(zero_buf, 0, rows_ref, pl.multiple_of(dst0 + c * (WAIT_BATCH * ROW_CHUNK), ROW_CHUNK),
                           sems.at[slot], WAIT_BATCH * ROW_CHUNK).start()
                return carry

            def single(c, carry):
                _rows_copy(zero_buf, 0, rows_ref, pl.multiple_of(dst0 + c * ROW_CHUNK, ROW_CHUNK),
                           sems.at[slot]).start()
                return carry

            lax.fori_loop(0, big, batch, 0)
            lax.fori_loop(big * WAIT_BATCH, n, single, 0)
            return total + n

        return lax.fori_loop(0, N_EXPERTS + 1, per_expert, jnp.int32(0))

    n_sent = n_sent + lax.cond(i == 0, tails, lambda: jnp.int32(0))
    pending[0] = n_sent

    @pl.when(i == pl.num_programs(0) - 1)
    def _():
        _wait_chunks(n_sent, wait_rows(sems.at[slot]))


def _dispatch(meta, h2, pos, gates, n_rows):
    m = h2.shape[0]
    tm = TOKEN_TILE
    tile = lambda i, *_: (i, 0)
    return pl.pallas_call(
        functools.partial(_dispatch_kernel, tm=tm),
        grid_spec=pltpu.PrefetchScalarGridSpec(
            num_scalar_prefetch=4,
            grid=(m // tm,),
            in_specs=[pl.BlockSpec((tm, D_MODEL), tile), pl.BlockSpec((tm, LANES), tile),
                      pl.BlockSpec((tm, LANES), tile)],
            out_specs=pl.BlockSpec(memory_space=pl.ANY),
            scratch_shapes=[pltpu.VMEM((2, TILE_ROWS, ROW_WORDS), U32),
                            pltpu.VMEM((WAIT_BATCH * ROW_CHUNK, ROW_WORDS), U32),
                            pltpu.SMEM((1,), I32), pltpu.SemaphoreType.DMA((2,))],
        ),
        out_shape=jax.ShapeDtypeStruct((n_rows, ROW_WORDS), U32),
        compiler_params=_cparams("arbitrary"),
    )(*meta, h2, pos, gates)


def _expert_kernel(be_ref, nused_ref, next_ref, valid_ref, rows_ref, wg_hbm, wu_hbm, wd_hbm, out_ref,
                   wgf, wuf, wdf, wgb, wub, wdb, holder, sems, *, layer):
    i = pl.program_id(0)
    used = i < nused_ref[0]
    expert = be_ref[i]
    first = jnp.logical_or(i == 0, expert != be_ref[jnp.maximum(i - 1, 0)])

    def weight_copies(e, slot):
        return (pltpu.make_async_copy(wg_hbm.at[layer, e], wgf.at[slot], sems.at[slot, 0]),
                pltpu.make_async_copy(wu_hbm.at[layer, e], wuf.at[slot], sems.at[slot, 1]),
                pltpu.make_async_copy(wd_hbm.at[layer, e], wdf.at[slot], sems.at[slot, 2]))

    @pl.when(jnp.logical_and(used, first))
    def _():
        @pl.when(i == 0)
        def _():
            holder[0] = 0
            for c in weight_copies(expert, 0):
                c.start()

        slot = holder[0]
        for c in weight_copies(expert, slot):
            c.wait()
        wgb[...] = wgf[slot].astype(BF16)
        wub[...] = wuf[slot].astype(BF16)
        wdb[...] = wdf[slot].astype(BF16)
        nxt = next_ref[i]

        @pl.when(nxt >= 0)
        def _():
            for c in weight_copies(nxt, 1 - slot):
                c.start()

        holder[0] = 1 - slot

    def mlp(r0, nrows):
        rows = rows_ref[r0:r0 + nrows, :]
        xb = jnp.concatenate(_unpack_bf16_pairs(rows[:, 0:ROW_WORDS - LANES]), axis=1)
        gate = lax.bitcast_convert_type(rows[:, ROW_WORDS - LANES:][:, 0:1], F32)
        g = jnp.dot(xb, wgb[...], preferred_element_type=F32)
        u = jnp.dot(xb, wub[...], preferred_element_type=F32)
        act = (g * jax.nn.sigmoid(g) * u).astype(BF16)
        y = jnp.dot(act, wdb[...], preferred_element_type=F32) * gate
        out_ref[r0:r0 + nrows, :] = _pack_bf16_pairs(y.astype(BF16).astype(F32))

    blk = rows_ref.shape[0]
    valid = valid_ref[i]

    fused = valid > blk // 2

    @pl.when(fused)
    def _():
        mlp(0, blk)

    @pl.when(jnp.logical_not(fused))
    def _():
        for r0 in range(0, blk, EXPERT_SUBBLOCK):
            @pl.when(r0 < valid)
            def _():
                mlp(r0, EXPERT_SUBBLOCK)

            @pl.when(r0 >= valid)
            def _():
                out_ref[r0:r0 + EXPERT_SUBBLOCK, :] = jnp.zeros((EXPERT_SUBBLOCK, out_ref.shape[1]), out_ref.dtype)


def _experts(block_expert, n_used, next_expert, valid, rows, w_gate, w_up, w_down, layer):
    n_rows = rows.shape[0]
    blk = EXPERT_BLOCK
    rmap = lambda i, be, nu, nx, va: (jnp.minimum(i, nu[0] - 1), 0)
    hbm = pl.BlockSpec(memory_space=pl.ANY)
    return pl.pallas_call(
        functools.partial(_expert_kernel, layer=layer),
        grid_spec=pltpu.PrefetchScalarGridSpec(
            num_scalar_prefetch=4,
            grid=(n_rows // blk,),
            in_specs=[pl.BlockSpec((blk, ROW_WORDS), rmap), hbm, hbm, hbm],
            out_specs=pl.BlockSpec((blk, D_MODEL // 2), lambda i, be, nu, nx, va: (i, 0)),
            scratch_shapes=[pltpu.VMEM((2, D_MODEL, D_EXPERT), F32), pltpu.VMEM((2, D_MODEL, D_EXPERT), F32),
                            pltpu.VMEM((2, D_EXPERT, D_MODEL), F32),
                            pltpu.VMEM((D_MODEL, D_EXPERT), BF16), pltpu.VMEM((D_MODEL, D_EXPERT), BF16),
                            pltpu.VMEM((D_EXPERT, D_MODEL), BF16),
                            pltpu.SMEM((1,), I32), pltpu.SemaphoreType.DMA((2, 3))],
        ),
        out_shape=jax.ShapeDtypeStruct((n_rows, D_MODEL // 2), U32),
        compiler_params=_cparams("arbitrary"),
    )(block_expert, n_used, next_expert, valid, rows, w_gate, w_up, w_down)


def _proj_section(hb, w_ref, n):
    return jnp.dot(hb, w_ref[:, n * D_ATTN:(n + 1) * D_ATTN], preferred_element_type=F32)


def _proj_qkv(hb, w_ref, qg_ref, kg_ref, gm_ref, q_ref, k_ref, v_ref):
    sec = functools.partial(_proj_section, hb, w_ref)

    def head_norm(z, gain):
        zz = (z * z).astype(BF16)
        half = gm_ref.shape[0]
        ms = jnp.concatenate([jnp.dot(zz[:, c:c + half], gm_ref[...], preferred_element_type=F32)
                              for c in range(0, D_ATTN, half)], axis=1)
        return z * lax.rsqrt(ms + EPS) * gain

    def put_slabs(ref, z):
        for hp in range(HEAD_PAIRS):
            ref[hp] = z[:, hp * LANES:(hp + 1) * LANES]

    put_slabs(q_ref, head_norm(sec(0), qg_ref[...]))
    put_slabs(k_ref, head_norm(sec(1), kg_ref[...]))
    put_slabs(v_ref, sec(2))


def _rglru_gates(xr, gr, cw_ref, cb_ref, wa_ref, ba_ref, wx_ref, bx_ref, lam_ref, xbuf, a_scr, b_scr, g_scr):
    tt = xr.shape[0]
    xbuf[8:8 + tt, :] = xr
    g_scr[...] = jax.nn.gelu(gr, approximate=True)
    u = cb_ref[...] + cw_ref[3:4, :] * xr
    for back in range(1, CONV_WIDTH):
        u = u + cw_ref[3 - back:4 - back, :] * xbuf[8 - back:8 - back + tt, :]
    xbuf[0:8, :] = xbuf[tt:tt + 8, :]

    ub = u.astype(BF16)
    r = jax.nn.sigmoid(jnp.dot(ub, wa_ref[...], preferred_element_type=F32) + ba_ref[...])
    gi = jax.nn.sigmoid(jnp.dot(ub, wx_ref[...], preferred_element_type=F32) + bx_ref[...])
    nl = -lam_ref[...]
    softplus = jnp.maximum(nl, 0.0) + jnp.log1p(jnp.exp(-jnp.abs(nl)))
    log_a = (-RG_C) * r * softplus
    a = jnp.exp(log_a)
    a_scr[...] = a
    b_scr[...] = jnp.sqrt(-jnp.tanh(log_a) * (a * a + 1.0)) * (gi * u)


def _rglru_scan(out_ref, a_scr, b_scr, h_scr, g_scr):
    tt = out_ref.shape[0]
    rc = RNN_CHUNK
    sub = lax.broadcasted_iota(I32, (SUBLANES, LANES), 0)

    def chunk(c, carry):
        r0 = pl.multiple_of(c * rc, rc)
        for g in range(D_RNN // LANES):
            ls = slice(g * LANES, (g + 1) * LANES)
            h = h_scr[0:1, ls]
            for v in range(rc // SUBLANES):
                rows = pl.ds(r0 + v * SUBLANES, SUBLANES)
                aa = a_scr[rows, ls]
                bb = b_scr[rows, ls]
                k = 1
                while k < SUBLANES:
                    keep = sub >= k
                    a_sh = pltpu.roll(aa, k, 0)
                    b_sh = pltpu.roll(bb, k, 0)
                    bb = jnp.where(keep, aa * b_sh + bb, bb)
                    aa = jnp.where(keep, aa * a_sh, aa)
                    k *= 2
                hv = aa * h + bb
                h = hv[SUBLANES - 1:SUBLANES, :]
                out_ref[rows, ls] = hv * g_scr[rows, ls]
            h_scr[0:1, ls] = h
        return carry

    lax.fori_loop(0, tt // rc, chunk, 0)


def _combine_body(ntot_ref, dst_ref, x1_ref, pos_ref, rows_ref, local, pending, sems):
    i = pl.program_id(0)
    slot = i % 2
    tm = x1_ref.shape[0]

    def fetch_tile(t, s):
        def fetch(loc, src):
            _rows_copy(rows_ref, src, local.at[s], loc, sems.at[s]).start()

        return _for_each_chunk(t, ntot_ref, dst_ref, fetch)

    @pl.when(i == 0)
    def _():
        local[...] = jnp.zeros_like(local)
        pending[0] = fetch_tile(0, 0)

    @pl.when(i + 1 < pl.num_programs(0))
    def _():
        pending[1 - slot] = fetch_tile(i + 1, 1 - slot)

    _wait_chunks(pending[slot], lambda r: _rows_copy(rows_ref, 0, local.at[0], 0, sems.at[slot], r).wait())

    pos = pos_ref[...]
    col = lax.broadcasted_iota(I32, (tm, TILE_ROWS), 1).astype(F32)
    pick = jnp.where(jnp.logical_or(col == pos[:, 0:1], col == pos[:, 1:2]), 1.0, 0.0).astype(BF16)
    lo, hi = _unpack_bf16_pairs(local[slot])
    moe = jnp.concatenate([jnp.dot(pick, lo, preferred_element_type=F32),
                           jnp.dot(pick, hi, preferred_element_type=F32)], axis=1)
    return x1_ref[...] + moe


N_PROJ_IN = 12


def _edge_kernel(*refs, with_combine, with_proj, tiles_per_seq):
    refs = list(refs)
    take = lambda n: [refs.pop(0) for _ in range(n)]
    if with_combine:
        ntot_ref, dst_ref = take(2)
    (x_ref,) = take(1)
    if with_combine:
        pos_ref, rows_ref = take(2)
    if with_proj:
        g_ref, w_ref, qg_ref, kg_ref, gm_ref, cw_ref, cb_ref, wa_ref, ba_ref, wx_ref, bx_ref, lam_ref = take(N_PROJ_IN)
    if with_combine:
        (x2_ref,) = take(1)
    if with_proj:
        q_ref, k_ref, v_ref, rnn_ref = take(4)
    if with_combine:
        local, pending, sems = take(3)
    if with_proj:
        xbuf, a_scr, b_scr, h_scr, g_scr = take(5)

    if with_proj:
        @pl.when(pl.program_id(0) % tiles_per_seq == 0)
        def _():
            xbuf[0:8, :] = jnp.zeros((8, D_RNN), F32)
            h_scr[...] = jnp.zeros_like(h_scr)

    if with_combine:
        x = _combine_body(ntot_ref, dst_ref, x_ref, pos_ref, rows_ref, local, pending, sems)
        x2_ref[...] = x
    else:
        x = x_ref[...]
    if with_proj:
        hb = _rms(x, g_ref[...]).astype(BF16)
        _rglru_gates(_proj_section(hb, w_ref, 3), _proj_section(hb, w_ref, 4), cw_ref, cb_ref, wa_ref, ba_ref,
                     wx_ref, bx_ref, lam_ref, xbuf, a_scr, b_scr, g_scr)
        _proj_qkv(hb, w_ref, qg_ref, kg_ref, gm_ref, q_ref, k_ref, v_ref)
        _rglru_scan(rnn_ref, a_scr, b_scr, h_scr, g_scr)


def _edge(x_in, seq_len, combine_in=None, proj_in=None):
    m = x_in.shape[0]
    tm = TOKEN_TILE
    with_combine, with_proj = combine_in is not None, proj_in is not None
    tile = lambda i, *_: (i, 0)
    fix = lambda i, *_: (0, 0)
    operands, in_specs, out_specs, out_shape, scratch = [], [], [], [], []
    n_prefetch = 0
    if with_combine:
        chunk_meta, pos, out_rows = combine_in
        operands += list(chunk_meta)
        n_prefetch = len(chunk_meta)
    operands.append(x_in)
    in_specs.append(pl.BlockSpec((tm, D_MODEL), tile))
    if with_combine:
        operands += [pos, out_rows]
        in_specs += [pl.BlockSpec((tm, LANES), tile), pl.BlockSpec(memory_space=pl.ANY)]
        out_specs.append(pl.BlockSpec((tm, D_MODEL), tile))
        out_shape.append(jax.ShapeDtypeStruct((m, D_MODEL), F32))
        scratch += [pltpu.VMEM((2, TILE_ROWS, D_MODEL // 2), U32), pltpu.SMEM((2,), I32),
                    pltpu.SemaphoreType.DMA((2,))]
    if with_proj:
        assert len(proj_in) == N_PROJ_IN
        operands += list(proj_in)
        in_specs += [pl.BlockSpec(a.shape, fix) for a in proj_in]
        slab = pl.BlockSpec((HEAD_PAIRS, tm, LANES), lambda i, *_: (0, i, 0))
        out_specs += [slab, slab, slab, pl.BlockSpec((tm, D_RNN), tile)]
        out_shape += [jax.ShapeDtypeStruct((HEAD_PAIRS, m, LANES), F32)] * 3 + [jax.ShapeDtypeStruct((m, D_RNN), F32)]
        scratch += [pltpu.VMEM((tm + 8, D_RNN), F32), pltpu.VMEM((tm, D_RNN), F32), pltpu.VMEM((tm, D_RNN), F32),
                    pltpu.VMEM((8, D_RNN), F32), pltpu.VMEM((tm, D_RNN), F32)]
    return pl.pallas_call(
        functools.partial(_edge_kernel, with_combine=with_combine, with_proj=with_proj,
                          tiles_per_seq=seq_len // tm),
        grid_spec=pltpu.PrefetchScalarGridSpec(
            num_scalar_prefetch=n_prefetch, grid=(m // tm,),
            in_specs=in_specs, out_specs=out_specs, scratch_shapes=scratch),
        out_shape=out_shape,
        compiler_params=_cparams("arbitrary"),
    )(*operands)


def _dispatch_plan(cnt_tiles, n_tiles, n_blocks):
    cnt = cnt_tiles.reshape(n_tiles, 8, LANES)[:, 0, ROUTER_LANE0:ROUTER_LANE0 + N_EXPERTS]
    seg = (cnt + ROW_CHUNK - 1) // ROW_CHUNK * ROW_CHUNK
    used = jnp.sum(seg, axis=0)
    padded = (used + EXPERT_BLOCK - 1) // EXPERT_BLOCK * EXPERT_BLOCK
    p_ends = jnp.cumsum(padded)
    p_starts = p_ends - padded
    seg_dst = p_starts[None, :] + jnp.cumsum(seg, axis=0) - seg
    loc_start = jnp.cumsum(seg, axis=1) - seg
    flat = lambda a: a.reshape(-1).astype(I32)
    block_row = jnp.arange(n_blocks, dtype=I32) * EXPERT_BLOCK
    block_expert = jnp.minimum(jnp.sum((p_ends[None, :] <= block_row[:, None]).astype(I32), axis=1), N_EXPERTS - 1)
    n_used = (p_ends[-1:] // EXPERT_BLOCK).astype(I32)
    after = (p_ends // EXPERT_BLOCK).astype(I32)[block_expert]
    next_expert = jnp.where(after < n_used[0], block_expert[jnp.minimum(after, n_blocks - 1)], -1).astype(I32)
    ends = jnp.cumsum(seg // ROW_CHUNK, axis=1)
    j = jnp.arange(TILE_CHUNKS, dtype=I32)
    owner = jnp.minimum(jnp.sum((ends[:, None, :] <= j[None, :, None]).astype(I32), axis=2), N_EXPERTS - 1)
    is_owner = owner[:, :, None] == jnp.arange(N_EXPERTS, dtype=I32)[None, None, :]
    chunk_dst = jnp.sum(jnp.where(is_owner, (seg_dst - loc_start)[:, None, :], 0), axis=2) + ROW_CHUNK * j[None, :]
    chunk_meta = (flat(ends[:, -1]), flat(chunk_dst))
    rest = n_blocks * EXPERT_BLOCK - p_ends[-1:]
    tail_meta = (flat(jnp.concatenate([padded - used, rest]) // ROW_CHUNK),
                 flat(jnp.concatenate([p_starts + used, p_ends[-1:]])))
    occupied_end = jnp.sum(jnp.where(block_expert[:, None] == jnp.arange(N_EXPERTS, dtype=I32)[None, :],
                                     (p_starts + used)[None, :], 0), axis=1)
    valid = jnp.clip(occupied_end - block_row, 0, EXPERT_BLOCK).astype(I32)
    return chunk_meta, tail_meta, (block_expert.astype(I32), n_used, next_expert, valid)


def _constants():
    gm = np.kron(np.eye(MXU_DEPTH // HEAD_DIM), np.full((HEAD_DIM, HEAD_DIM), 1.0 / HEAD_DIM))
    tri = np.tril(np.ones((TOKEN_TILE, TOKEN_TILE), np.float32), -1)
    upper = np.triu(np.ones((LANES, LANES), np.float32), 1)
    return jnp.asarray(gm, BF16), jnp.asarray(tri, BF16), jnp.asarray(upper, BF16)


def kernel(x, rel_bias_table, norm_mix, w_in, q_norm, k_norm, conv_w, conv_b, rg_w_a, rg_b_a, rg_w_x, rg_b_x,
           rg_lambda, norm_attn_out, norm_rnn_out, w_out, norm_ffn, router_group_w, router_group_b,
           router_expert_w, router_expert_b, expert_w_gate, expert_w_up, expert_w_down):
    b, s, d = x.shape
    depth = w_in.shape[0]
    m = b * s
    assert d == D_MODEL and s % ATTN_TILE == 0 and s % TOKEN_TILE == 0

    gm, tri, upper = _constants()
    bias = _bias_tables(rel_bias_table)
    scale = HEAD_DIM ** -0.5
    n_tiles = m // TOKEN_TILE
    n_blocks = -(-(m * TOP_K + n_tiles * N_EXPERTS * (ROW_CHUNK - 1)) // EXPERT_BLOCK) + N_EXPERTS
    n_rows = n_blocks * EXPERT_BLOCK
    row1 = lambda v: v.reshape(1, -1).astype(F32)

    def proj_params(l):
        return (row1(norm_mix[l]), w_in[l].astype(BF16),
                row1(jnp.tile(q_norm[l], ATTN_HEADS) * (scale * LOG2E)), row1(jnp.tile(k_norm[l], ATTN_HEADS)), gm,
                conv_w[l].astype(F32), row1(conv_b[l]),
                _block_diag(rg_w_a[l]).astype(BF16), row1(rg_b_a[l]),
                _block_diag(rg_w_x[l]).astype(BF16), row1(rg_b_x[l]), row1(rg_lambda[l]))

    x2 = x.reshape(m, d).astype(F32)
    q, k, v, rnn = _edge(x2, s, proj_in=proj_params(0))
    for l in range(depth):
        attn = _attention(q, k, v, bias, b, s)

        wr = jnp.zeros((D_MODEL, LANES), F32)
        wr = wr.at[:, :N_GROUPS].set(router_group_w[l]).at[:, N_GROUPS:N_GROUPS + N_EXPERTS].set(router_expert_w[l])
        wr_pair = jnp.concatenate(_split_bf16(wr), axis=1)
        br = jnp.zeros((1, LANES), F32)
        br = br.at[0, :N_GROUPS].set(router_group_b[l]).at[0, N_GROUPS:N_GROUPS + N_EXPERTS].set(router_expert_b[l])
        x1, h2, pos, gates, cnt = _mix(x2, attn, rnn, row1(norm_attn_out[l]), row1(norm_rnn_out[l]),
                                       w_out[l].astype(BF16), row1(norm_ffn[l]), wr_pair, br, tri, upper)

        chunk_meta, tail_meta, block_meta = _dispatch_plan(cnt, n_tiles, n_blocks)
        rows = _dispatch(chunk_meta + tail_meta, h2, pos, gates, n_rows)
        out_rows = _experts(*block_meta, rows, expert_w_gate, expert_w_up, expert_w_down, l)
        nxt = proj_params(l + 1) if l + 1 < depth else None
        x2, *started = _edge(x1, s, combine_in=(chunk_meta, pos, out_rows), proj_in=nxt)
        if started:
            q, k, v, rnn = started
    return x2.reshape(b, s, d).astype(x.dtype)
```

```python
import functools
import math

import numpy as np
import jax
import jax.numpy as jnp
from jax import lax
from jax.experimental import pallas as pl
from jax.experimental.pallas import tpu as pltpu

F32 = jnp.float32
BF16 = jnp.bfloat16
I32 = jnp.int32
U32 = jnp.uint32

D_MODEL = 1024
ATTN_HEADS = 8
HEAD_DIM = 64
D_ATTN = ATTN_HEADS * HEAD_DIM
RNN_BLOCKS = 8
D_RNN = 512
D_MIX = D_ATTN + D_RNN
D_IN = 3 * D_ATTN + 2 * D_RNN
DILATED_BRANCHES = ((128, 1), (512, 4), (2048, 16))
Q_BLOCK = 128
REL_BUCKETS = 32
REL_MAX_DIST = 2048
CONV_WIDTH = 4
RG_C = 8.0
N_GROUPS = 4
EXPERTS_PER_GROUP = 8
N_EXPERTS = N_GROUPS * EXPERTS_PER_GROUP
TOP_K = 2
D_EXPERT = 512
EPS = 1e-6
NEG_INF = -1e30
LOG2E = math.log2(math.e)

LANES = 128
SUBLANES = 8
MXU_DEPTH = 256
HEAD_PAIRS = D_ATTN // LANES
ROUTER_LANE0 = N_GROUPS
TOKEN_TILE = 512
ATTN_TILE = 2048
ATTN_PAIRS = 2
ATTN_UNROLL = 16
RNN_CHUNK = 64
EXPERT_SUBBLOCK = 128
EXPERT_BLOCK = 512
ROW_CHUNK = 8
TILE_ROWS = -(-(TOKEN_TILE * TOP_K + N_EXPERTS * (ROW_CHUNK - 1)) // 256) * 256
TILE_CHUNKS = TILE_ROWS // ROW_CHUNK
WAIT_BATCH = 16
ROW_WORDS = D_MODEL // 2 + LANES
MERGE_ROWS = 256
VMEM_LIMIT = 48 * 1024 * 1024
ATTN_VMEM_LIMIT = 56 * 1024 * 1024


def _cparams(*sem):
    return pltpu.CompilerParams(dimension_semantics=sem, vmem_limit_bytes=VMEM_LIMIT)


def _rms(x, gain):
    return x * lax.rsqrt(jnp.mean(x * x, axis=-1, keepdims=True) + EPS) * gain


def _rows(start, size, stride):
    return pl.ds(start, size) if stride == 1 else pl.ds(start, size, stride=stride)


def _attn_kernel(q_ref, k_ref, v_ref, bias_ref, out_ref, qd, stage, o_scr, l_scr, *kv_scr):
    last = len(DILATED_BRANCHES) - 1
    for p in range(ATTN_PAIRS):
        for n, (_, dil) in enumerate(DILATED_BRANCHES):
            o_dst = out_ref.at[p] if n == last else o_scr.at[n, p]
            _attn_pair(q_ref.at[p], k_ref.at[p], v_ref.at[p], bias_ref.at[n], 2 * p, o_dst, l_scr.at[n, p],
                       qd, kv_scr[2 * n].at[p], kv_scr[2 * n + 1].at[p], stage, dil)

        def merge(c, carry):
            rows = pl.ds(pl.multiple_of(c * MERGE_ROWS, MERGE_ROWS), MERGE_ROWS)
            lse = [l_scr[n, p, rows, :] for n in range(last + 1)]
            mx = functools.reduce(jnp.maximum, lse)
            w = [jnp.exp2(l - mx) for l in lse]
            o = [o_scr[n, p, rows, :] for n in range(last)] + [out_ref[p, rows, :]]
            num = functools.reduce(jnp.add, [wi * oi for wi, oi in zip(w, o)])
            out_ref[p, rows, :] = num * (1.0 / functools.reduce(jnp.add, w))
            return carry

        lax.fori_loop(0, out_ref.shape[1] // MERGE_ROWS, merge, 0)


def _attn_pair(q_ref, k_ref, v_ref, bias_ref, head0, o_ref, lse_ref, qd, kd, vd, stage, dil):
    i = pl.program_id(2)
    tile = q_ref.shape[0]
    rows = tile // dil
    seg = Q_BLOCK + rows

    @pl.when(i == 0)
    def _():
        for r in range(dil):
            kd[r * seg:r * seg + Q_BLOCK, :] = jnp.zeros((Q_BLOCK, LANES), BF16)
            for hh in range(2):
                vd[hh, r * seg:r * seg + Q_BLOCK, :] = jnp.zeros((Q_BLOCK, LANES), BF16)

    @pl.when(i > 0)
    def _():
        for r in range(dil):
            kd[r * seg:r * seg + Q_BLOCK, :] = kd[r * seg + rows:(r + 1) * seg, :]
            for hh in range(2):
                vd[hh, r * seg:r * seg + Q_BLOCK, :] = vd[hh, r * seg + rows:(r + 1) * seg, :]

    def regroup(src_ref, put):
        if dil % 16 == 0:
            quarter = tile // 4
            for c in range(4):
                stage[c * quarter:(c + 1) * quarter, :] = src_ref[pl.ds(c, quarter, stride=4), :]
            for r in range(dil):
                put(r, stage[_rows((r % 4) * quarter + r // 4, rows, dil // 4), :].astype(BF16))
        else:
            for r in range(dil):
                put(r, src_ref[_rows(r, rows, dil), :].astype(BF16))

    def put_q(r, x):
        qd[r * rows:(r + 1) * rows, :] = x

    def put_k(r, x):
        kd[r * seg + Q_BLOCK:(r + 1) * seg, :] = x

    first_head = lax.broadcasted_iota(I32, (rows, LANES), 1) < HEAD_DIM

    def put_v(r, x):
        vd[0, r * seg + Q_BLOCK:(r + 1) * seg, :] = jnp.where(first_head, x, jnp.zeros_like(x))
        vd[1, r * seg + Q_BLOCK:(r + 1) * seg, :] = jnp.where(first_head, jnp.zeros_like(x), x)

    regroup(q_ref, put_q)
    regroup(k_ref, put_k)
    regroup(v_ref, put_v)

    per = rows // Q_BLOCK
    low = lax.broadcasted_iota(I32, (Q_BLOCK, LANES), 1) < HEAD_DIM
    key_lane = lax.broadcasted_iota(I32, (2 * Q_BLOCK, LANES), 1)
    ones = [jnp.where(key_lane < HEAD_DIM, 1.0, 0.0).astype(BF16), jnp.where(key_lane < HEAD_DIM, 0.0, 1.0).astype(BF16)]

    def block(blk, carry):
        r = blk // per
        n = blk % per
        q0 = pl.multiple_of(r * rows + n * Q_BLOCK, Q_BLOCK)
        k0 = pl.multiple_of(r * seg + n * Q_BLOCK, Q_BLOCK)
        sel = jnp.where(jnp.logical_and(i == 0, n == 0), 1, 0)
        q = qd[pl.ds(q0, Q_BLOCK), :]
        kk = kd[pl.ds(k0, 2 * Q_BLOCK), :]
        probs, maxes = [], []
        for hh in range(2):
            qm = jnp.where(low if hh == 0 else jnp.logical_not(low), q, jnp.zeros_like(q))
            s = lax.dot_general(qm, kk, (((1,), (1,)), ((), ())), preferred_element_type=F32)
            s = s + bias_ref[sel, head0 + hh]
            mx = jnp.max(s, axis=-1, keepdims=True)
            probs.append(jnp.exp2(s - mx).astype(BF16))
            maxes.append(mx)
        rhs = jnp.concatenate([jnp.concatenate([vd[hh, pl.ds(k0, 2 * Q_BLOCK), :], ones[hh]], axis=1)
                               for hh in range(2)], axis=0)
        both = jnp.dot(jnp.concatenate(probs, axis=1), rhs, preferred_element_type=F32)
        l = both[:, LANES:]
        dst = _rows(r + dil * Q_BLOCK * n, Q_BLOCK, dil)
        o_ref[dst, :] = both[:, :LANES] * (1.0 / l)
        lse_ref[dst, :] = jnp.where(low, maxes[0], maxes[1]) + jnp.log(l) * LOG2E
        return carry

    lax.fori_loop(0, dil * per, block, 0, unroll=ATTN_UNROLL)


def _attention(q, k, v, bias, b, s):
    tile = ATTN_TILE
    nt = s // tile
    nb = len(DILATED_BRANCHES)
    blk = pl.BlockSpec((ATTN_PAIRS, tile, LANES), lambda bb, g, i: (g, bb * nt + i, 0))
    kv_scr = []
    for _, dil in DILATED_BRANCHES:
        keys = tile + Q_BLOCK * dil
        kv_scr += [pltpu.VMEM((ATTN_PAIRS, keys, LANES), BF16), pltpu.VMEM((ATTN_PAIRS, 2, keys, LANES), BF16)]
    return pl.pallas_call(
        _attn_kernel,
        grid=(b, HEAD_PAIRS // ATTN_PAIRS, nt),
        in_specs=[blk, blk, blk,
                  pl.BlockSpec((nb, 2, 2 * ATTN_PAIRS, Q_BLOCK, 2 * Q_BLOCK), lambda bb, g, i: (0, 0, g, 0, 0))],
        out_specs=blk,
        out_shape=jax.ShapeDtypeStruct(q.shape, F32),
        scratch_shapes=[pltpu.VMEM((tile, LANES), BF16), pltpu.VMEM((tile, LANES), F32),
                        pltpu.VMEM((nb - 1, ATTN_PAIRS, tile, LANES), F32),
                        pltpu.VMEM((nb, ATTN_PAIRS, tile, LANES), F32)] + kv_scr,
        compiler_params=pltpu.CompilerParams(dimension_semantics=("parallel", "parallel", "arbitrary"),
                                             vmem_limit_bytes=ATTN_VMEM_LIMIT),
    )(q, k, v, bias)


def _bias_step_tables():
    exact = REL_BUCKETS // 2
    i = np.arange(Q_BLOCK)[:, None]
    j = np.arange(2 * Q_BLOCK)[None, :]
    steps = i + Q_BLOCK - j
    onehot, band = [], []
    for window, dil in DILATED_BRANCHES:
        dist = (np.arange(2 * Q_BLOCK) * dil).astype(np.int32)
        d = np.maximum(dist, 1).astype(np.float32)
        log_b = exact + (np.log(d / np.float32(exact)) / np.float32(math.log(REL_MAX_DIST / exact))
                         * np.float32(REL_BUCKETS - exact)).astype(np.int32)
        bucket = np.where(dist < exact, dist, np.minimum(log_b, REL_BUCKETS - 1))
        onehot.append(np.eye(REL_BUCKETS, dtype=np.float32)[bucket])
        band.append((steps >= 0) & (steps <= window // dil))
    first = np.broadcast_to(j >= Q_BLOCK, (Q_BLOCK, 2 * Q_BLOCK))
    return np.stack(onehot), np.stack(band), first


def _bias_tables(rel_table):
    onehot, band, first = _bias_step_tables()
    nb = len(DILATED_BRANCHES)
    vec = LOG2E * jnp.einsum('nsb,bh->nhs', onehot, rel_table.astype(F32), precision=lax.Precision.HIGHEST)
    period = 3 * Q_BLOCK
    padded = jnp.concatenate([vec, jnp.zeros((nb, ATTN_HEADS, Q_BLOCK), F32)], axis=-1)
    wrapped = jnp.roll(padded[..., ::-1], Q_BLOCK + 1, axis=-1)
    bias = jnp.tile(wrapped, (1, 1, Q_BLOCK))[..., :Q_BLOCK * (period - 1)].reshape(
        nb, ATTN_HEADS, Q_BLOCK, period - 1)[..., :2 * Q_BLOCK]
    regular = jnp.where(band[:, None], bias, NEG_INF)
    start = jnp.where((band & first[None])[:, None], bias, NEG_INF)
    return jnp.stack([regular, start], axis=1)


def _block_diag(w):
    nb, n, _ = w.shape
    eye = jnp.eye(nb, dtype=w.dtype)
    return (eye[:, None, :, None] * w[:, :, None, :]).reshape(nb * n, nb * n)


def _split_bf16(x):
    hi = x.astype(BF16)
    return hi, (x - hi.astype(F32)).astype(BF16)


def _mix_kernel(x_ref, attn_ref, rnn_ref, ga_ref, gn_ref, wo_ref, gf_ref,
                wr_ref, br_ref, tri_ref, upper_ref,
                x1_ref, h2_ref, pos_ref, gate_ref, cnt_ref):
    slabs = [attn_ref[hp] for hp in range(HEAD_PAIRS)]
    sumsq = functools.reduce(jnp.add, [jnp.sum(a * a, axis=-1, keepdims=True) for a in slabs])
    scale = lax.rsqrt(sumsq * (1.0 / D_ATTN) + EPS)
    na = (jnp.concatenate([a * scale for a in slabs], axis=1) * ga_ref[...]).astype(BF16)
    nr = _rms(rnn_ref[...], gn_ref[...]).astype(BF16)
    x1 = (x_ref[...] + jnp.dot(na, wo_ref[0:D_ATTN, :], preferred_element_type=F32)
          + jnp.dot(nr, wo_ref[D_ATTN:, :], preferred_element_type=F32))
    x1_ref[...] = x1
    h2 = _rms(x1, gf_ref[...])
    h2_ref[...] = h2.astype(BF16)

    hh, hl = _split_bf16(h2)
    by_hi = jnp.dot(hh, wr_ref[...], preferred_element_type=F32)
    by_lo = jnp.dot(hl, wr_ref[...], preferred_element_type=F32)
    logits = by_hi[:, :LANES] + by_hi[:, LANES:] + by_lo[:, :LANES] + br_ref[...]
    tm = logits.shape[0]
    lane = lax.broadcasted_iota(I32, (tm, LANES), 1)
    lanef = lane.astype(F32)
    big = float(LANES)

    def top(vals):
        m = jnp.max(vals, axis=-1, keepdims=True)
        return m, jnp.min(jnp.where(vals == m, lanef, big), axis=-1, keepdims=True)

    is_group = lane < N_GROUPS
    gmax, gsel = top(jnp.where(is_group, logits, NEG_INF))
    g_w = 1.0 / jnp.sum(jnp.where(is_group, jnp.exp(logits - gmax), 0.0), axis=-1, keepdims=True)
    lo_lane = ROUTER_LANE0 + EXPERTS_PER_GROUP * gsel
    in_group = jnp.logical_and(lanef >= lo_lane, lanef < lo_lane + EXPERTS_PER_GROUP)
    el = jnp.where(in_group, logits, NEG_INF)
    v1, i1 = top(el)
    v2, i2 = top(jnp.where(lanef == i1, NEG_INF, el))
    t = jnp.exp(v2 - v1)
    p1 = 1.0 / (1.0 + t)
    gate1 = g_w * p1
    gate2 = g_w * (t * p1)

    oh1 = (lanef == i1).astype(F32)
    oh2 = (lanef == i2).astype(F32)
    cnt = oh1 + oh2
    prefix = jnp.dot(tri_ref[...], cnt.astype(BF16), preferred_element_type=F32)
    total = jnp.sum(cnt, axis=0, keepdims=True)
    chunks = jnp.floor((total + (ROW_CHUNK - 1)) * (1.0 / ROW_CHUNK))
    seg_start = ROW_CHUNK * jnp.dot(jnp.broadcast_to(chunks, (8, LANES)).astype(BF16), upper_ref[...],
                                    preferred_element_type=F32)[0:1, :]
    base = seg_start + prefix
    pos1 = jnp.sum(oh1 * base, axis=-1, keepdims=True)
    pos2 = jnp.sum(oh2 * base, axis=-1, keepdims=True)
    cnt_ref[...] = jnp.broadcast_to(total, cnt_ref.shape).astype(I32)
    pos_ref[...] = jnp.where(lane == 0, pos1, jnp.where(lane == 1, pos2, 0.0))
    gate_ref[...] = jnp.where(lane == 0, gate1, jnp.where(lane == 1, gate2, 0.0))


def _mix(x2, attn, rnn, ga, gn, wo_bf, gf, wr_pair, br, tri, upper):
    m = x2.shape[0]
    tm = TOKEN_TILE
    row = lambda i: (i, 0)
    fix = lambda i: (0, 0)
    t512 = pl.BlockSpec((tm, D_ATTN), row)
    t128 = pl.BlockSpec((tm, LANES), row)
    t1024 = pl.BlockSpec((tm, D_MODEL), row)
    slab = pl.BlockSpec((HEAD_PAIRS, tm, LANES), lambda i: (0, i, 0))
    return pl.pallas_call(
        _mix_kernel,
        grid=(m // tm,),
        in_specs=[t1024, slab, t512,
                  pl.BlockSpec((1, D_ATTN), fix), pl.BlockSpec((1, D_RNN), fix),
                  pl.BlockSpec((D_MIX, D_MODEL), fix), pl.BlockSpec((1, D_MODEL), fix),
                  pl.BlockSpec((D_MODEL, 2 * LANES), fix), pl.BlockSpec((1, LANES), fix),
                  pl.BlockSpec((tm, tm), fix), pl.BlockSpec((LANES, LANES), fix)],
        out_specs=[t1024, t1024, t128, t128, pl.BlockSpec((8, LANES), row)],
        out_shape=[jax.ShapeDtypeStruct((m, D_MODEL), F32), jax.ShapeDtypeStruct((m, D_MODEL), BF16),
                   jax.ShapeDtypeStruct((m, LANES), F32), jax.ShapeDtypeStruct((m, LANES), F32),
                   jax.ShapeDtypeStruct((m // tm * 8, LANES), I32)],
        compiler_params=_cparams("parallel"),
    )(x2, attn, rnn, ga, gn, wo_bf, gf, wr_pair, br, tri, upper)


def _rows_copy(src, src_row, dst, dst_row, sem, rows=ROW_CHUNK):
    return pltpu.make_async_copy(src.at[pl.ds(src_row, rows)], dst.at[pl.ds(dst_row, rows)], sem)


def _for_each_chunk(i, ntot_ref, dst_ref, copy):
    n = ntot_ref[i]

    def body(j, carry):
        copy(pl.multiple_of(j * ROW_CHUNK, ROW_CHUNK), pl.multiple_of(dst_ref[i * TILE_CHUNKS + j], ROW_CHUNK))
        return carry

    lax.fori_loop(0, n, body, 0)
    return n


def _wait_chunks(count, wait_rows):
    def batch(c, carry):
        wait_rows(WAIT_BATCH * ROW_CHUNK)
        return carry

    def single(c, carry):
        wait_rows(ROW_CHUNK)
        return carry

    lax.fori_loop(0, count // WAIT_BATCH, batch, 0)
    lax.fori_loop(0, count % WAIT_BATCH, single, 0)


def _pack_bf16_pairs(x):
    c = x.shape[1] // 2
    lo = lax.bitcast_convert_type(x[:, :c], U32) >> 16
    hi = lax.bitcast_convert_type(x[:, c:], U32) & jnp.uint32(0xFFFF0000)
    return hi | lo


def _unpack_bf16_pairs(w):
    lo = lax.bitcast_convert_type(w << 16, F32).astype(BF16)
    hi = lax.bitcast_convert_type(w & jnp.uint32(0xFFFF0000), F32).astype(BF16)
    return lo, hi


def _dispatch_kernel(ntot_ref, dst_ref, tailn_ref, taildst_ref,
                     h_ref, pos_ref, gate_ref, rows_ref, sorted_buf, zero_buf, pending, sems, *, tm):
    i = pl.program_id(0)
    slot = i % 2
    pos_t = pos_ref[...].T
    gate_t = gate_ref[...].T
    row = lax.broadcasted_iota(I32, (TILE_ROWS, tm), 0).astype(F32)
    hit1 = row == pos_t[0:1, :]
    hit2 = row == pos_t[1:2, :]
    onehot = jnp.where(jnp.logical_or(hit1, hit2), 1.0, 0.0).astype(BF16)
    feat = jnp.dot(onehot, h_ref[...], preferred_element_type=F32)
    sorted_buf[slot, :, 0:ROW_WORDS - LANES] = _pack_bf16_pairs(feat)
    gate = jnp.sum(jnp.where(hit1, gate_t[0:1, :], 0.0) + jnp.where(hit2, gate_t[1:2, :], 0.0),
                   axis=-1, keepdims=True)
    lane = lax.broadcasted_iota(I32, (TILE_ROWS, LANES), 1)
    sorted_buf[slot, :, ROW_WORDS - LANES:] = jnp.where(lane == 0, lax.bitcast_convert_type(gate, U32), jnp.uint32(0))

    def wait_rows(sem):
        return lambda r: _rows_copy(sorted_buf.at[0], 0, rows_ref, 0, sem, r).wait()

    @pl.when(i > 0)
    def _():
        _wait_chunks(pending[0], wait_rows(sems.at[1 - slot]))

    def send(loc, dst):
        _rows_copy(sorted_buf.at[slot], loc, rows_ref, dst, sems.at[slot]).start()

    n_sent = _for_each_chunk(i, ntot_ref, dst_ref, send)

    @pl.when(i == 0)
    def _():
        zero_buf[...] = jnp.zeros_like(zero_buf)

    def tails():
        def per_expert(e, total):
            n = tailn_ref[e]
            dst0 = taildst_ref[e]
            big = n // WAIT_BATCH

            def batch(c, carry):
                _rows_copy(zero_buf, 0, rows_ref, pl.multiple_of(dst0 + c * (WAIT_BATCH * ROW_CHUNK), ROW_CHUNK),
                           sems.at[slot], WAIT_BATCH * ROW_CHUNK).start()
                return carry

            def single(c, carry):
                _rows_copy(zero_buf, 0, rows_ref, pl.multiple_of(dst0 + c * ROW_CHUNK, ROW_CHUNK),
                           sems.at[slot]).start()
                return carry

            lax.fori_loop(0, big, batch, 0)
            lax.fori_loop(big * WAIT_BATCH, n, single, 0)
            return total + n

        return lax.fori_loop(0, N_EXPERTS + 1, per_expert, jnp.int32(0))

    n_sent = n_sent + lax.cond(i == 0, tails, lambda: jnp.int32(0))
    pending[0] = n_sent

    @pl.when(i == pl.num_programs(0) - 1)
    def _():
        _wait_chunks(n_sent, wait_rows(sems.at[slot]))


def _dispatch(meta, h2, pos, gates, n_rows):
    m = h2.shape[0]
    tm = TOKEN_TILE
    tile = lambda i, *_: (i, 0)
    return pl.pallas_call(
        functools.partial(_dispatch_kernel, tm=tm),
        grid_spec=pltpu.PrefetchScalarGridSpec(
            num_scalar_prefetch=4,
            grid=(m // tm,),
            in_specs=[pl.BlockSpec((tm, D_MODEL), tile), pl.BlockSpec((tm, LANES), tile),
                      pl.BlockSpec((tm, LANES), tile)],
            out_specs=pl.BlockSpec(memory_space=pl.ANY),
            scratch_shapes=[pltpu.VMEM((2, TILE_ROWS, ROW_WORDS), U32),
                            pltpu.VMEM((WAIT_BATCH * ROW_CHUNK, ROW_WORDS), U32),
                            pltpu.SMEM((1,), I32), pltpu.SemaphoreType.DMA((2,))],
        ),
        out_shape=jax.ShapeDtypeStruct((n_rows, ROW_WORDS), U32),
        compiler_params=_cparams("arbitrary"),
    )(*meta, h2, pos, gates)


def _expert_kernel(be_ref, nused_ref, next_ref, valid_ref, rows_ref, wg_hbm, wu_hbm, wd_hbm, out_ref,
                   wgf, wuf, wdf, wgb, wub, wdb, holder, sems, *, layer):
    i = pl.program_id(0)
    used = i < nused_ref[0]
    expert = be_ref[i]
    first = jnp.logical_or(i == 0, expert != be_ref[jnp.maximum(i - 1, 0)])

    def weight_copies(e, slot):
        return (pltpu.make_async_copy(wg_hbm.at[layer, e], wgf.at[slot], sems.at[slot, 0]),
                pltpu.make_async_copy(wu_hbm.at[layer, e], wuf.at[slot], sems.at[slot, 1]),
                pltpu.make_async_copy(wd_hbm.at[layer, e], wdf.at[slot], sems.at[slot, 2]))

    @pl.when(jnp.logical_and(used, first))
    def _():
        @pl.when(i == 0)
        def _():
            holder[0] = 0
            for c in weight_copies(expert, 0):
                c.start()

        slot = holder[0]
        for c in weight_copies(expert, slot):
            c.wait()
        wgb[...] = wgf[slot].astype(BF16)
        wub[...] = wuf[slot].astype(BF16)
        wdb[...] = wdf[slot].astype(BF16)
        nxt = next_ref[i]

        @pl.when(nxt >= 0)
        def _():
            for c in weight_copies(nxt, 1 - slot):
                c.start()

        holder[0] = 1 - slot

    def mlp(r0, nrows):
        rows = rows_ref[r0:r0 + nrows, :]
        xb = jnp.concatenate(_unpack_bf16_pairs(rows[:, 0:ROW_WORDS - LANES]), axis=1)
        gate = lax.bitcast_convert_type(rows[:, ROW_WORDS - LANES:][:, 0:1], F32)
        g = jnp.dot(xb, wgb[...], preferred_element_type=F32)
        u = jnp.dot(xb, wub[...], preferred_element_type=F32)
        act = (g * jax.nn.sigmoid(g) * u).astype(BF16)
        y = jnp.dot(act, wdb[...], preferred_element_type=F32) * gate
        out_ref[r0:r0 + nrows, :] = _pack_bf16_pairs(y.astype(BF16).astype(F32))

    blk = rows_ref.shape[0]
    valid = valid_ref[i]

    fused = valid > blk // 2

    @pl.when(fused)
    def _():
        mlp(0, blk)

    @pl.when(jnp.logical_not(fused))
    def _():
        for r0 in range(0, blk, EXPERT_SUBBLOCK):
            @pl.when(r0 < valid)
            def _():
                mlp(r0, EXPERT_SUBBLOCK)

            @pl.when(r0 >= valid)
            def _():
                out_ref[r0:r0 + EXPERT_SUBBLOCK, :] = jnp.zeros((EXPERT_SUBBLOCK, out_ref.shape[1]), out_ref.dtype)


def _experts(block_expert, n_used, next_expert, valid, rows, w_gate, w_up, w_down, layer):
    n_rows = rows.shape[0]
    blk = EXPERT_BLOCK
    rmap = lambda i, be, nu, nx, va: (jnp.minimum(i, nu[0] - 1), 0)
    hbm = pl.BlockSpec(memory_space=pl.ANY)
    return pl.pallas_call(
        functools.partial(_expert_kernel, layer=layer),
        grid_spec=pltpu.PrefetchScalarGridSpec(
            num_scalar_prefetch=4,
            grid=(n_rows // blk,),
            in_specs=[pl.BlockSpec((blk, ROW_WORDS), rmap), hbm, hbm, hbm],
            out_specs=pl.BlockSpec((blk, D_MODEL // 2), lambda i, be, nu, nx, va: (i, 0)),
            scratch_shapes=[pltpu.VMEM((2, D_MODEL, D_EXPERT), F32), pltpu.VMEM((2, D_MODEL, D_EXPERT), F32),
                            pltpu.VMEM((2, D_EXPERT, D_MODEL), F32),
                            pltpu.VMEM((D_MODEL, D_EXPERT), BF16), pltpu.VMEM((D_MODEL, D_EXPERT), BF16),
                            pltpu.VMEM((D_EXPERT, D_MODEL), BF16),
                            pltpu.SMEM((1,), I32), pltpu.SemaphoreType.DMA((2, 3))],
        ),
        out_shape=jax.ShapeDtypeStruct((n_rows, D_MODEL // 2), U32),
        compiler_params=_cparams("arbitrary"),
    )(block_expert, n_used, next_expert, valid, rows, w_gate, w_up, w_down)


def _proj_section(hb, w_ref, n):
    return jnp.dot(hb, w_ref[:, n * D_ATTN:(n + 1) * D_ATTN], preferred_element_type=F32)


def _proj_qkv(hb, w_ref, qg_ref, kg_ref, gm_ref, q_ref, k_ref, v_ref):
    sec = functools.partial(_proj_section, hb, w_ref)

    def head_norm(z, gain):
        zz = (z * z).astype(BF16)
        half = gm_ref.shape[0]
        ms = jnp.concatenate([jnp.dot(zz[:, c:c + half], gm_ref[...], preferred_element_type=F32)
                              for c in range(0, D_ATTN, half)], axis=1)
        return z * lax.rsqrt(ms + EPS) * gain

    def put_slabs(ref, z):
        for hp in range(HEAD_PAIRS):
            ref[hp] = z[:, hp * LANES:(hp + 1) * LANES]

    put_slabs(q_ref, head_norm(sec(0), qg_ref[...]))
    put_slabs(k_ref, head_norm(sec(1), kg_ref[...]))
    put_slabs(v_ref, sec(2))


def _rglru_gates(xr, gr, cw_ref, cb_ref, wa_ref, ba_ref, wx_ref, bx_ref, lam_ref, xbuf, a_scr, b_scr, g_scr):
    tt = xr.shape[0]
    xbuf[8:8 + tt, :] = xr
    g_scr[...] = jax.nn.gelu(gr, approximate=True)
    u = cb_ref[...] + cw_ref[3:4, :] * xr
    for back in range(1, CONV_WIDTH):
        u = u + cw_ref[3 - back:4 - back, :] * xbuf[8 - back:8 - back + tt, :]
    xbuf[0:8, :] = xbuf[tt:tt + 8, :]

    ub = u.astype(BF16)
    r = jax.nn.sigmoid(jnp.dot(ub, wa_ref[...], preferred_element_type=F32) + ba_ref[...])
    gi = jax.nn.sigmoid(jnp.dot(ub, wx_ref[...], preferred_element_type=F32) + bx_ref[...])
    nl = -lam_ref[...]
    softplus = jnp.maximum(nl, 0.0) + jnp.log1p(jnp.exp(-jnp.abs(nl)))
    log_a = (-RG_C) * r * softplus
    a = jnp.exp(log_a)
    a_scr[...] = a
    b_scr[...] = jnp.sqrt(-jnp.tanh(log_a) * (a * a + 1.0)) * (gi * u)


def _rglru_scan(out_ref, a_scr, b_scr, h_scr, g_scr):
    tt = out_ref.shape[0]
    rc = RNN_CHUNK
    sub = lax.broadcasted_iota(I32, (SUBLANES, LANES), 0)

    def chunk(c, carry):
        r0 = pl.multiple_of(c * rc, rc)
        for g in range(D_RNN // LANES):
            ls = slice(g * LANES, (g + 1) * LANES)
            h = h_scr[0:1, ls]
            for v in range(rc // SUBLANES):
                rows = pl.ds(r0 + v * SUBLANES, SUBLANES)
                aa = a_scr[rows, ls]
                bb = b_scr[rows, ls]
                k = 1
                while k < SUBLANES:
                    keep = sub >= k
                    a_sh = pltpu.roll(aa, k, 0)
                    b_sh = pltpu.roll(bb, k, 0)
                    bb = jnp.where(keep, aa * b_sh + bb, bb)
                    aa = jnp.where(keep, aa * a_sh, aa)
                    k *= 2
                hv = aa * h + bb
                h = hv[SUBLANES - 1:SUBLANES, :]
                out_ref[rows, ls] = hv * g_scr[rows, ls]
            h_scr[0:1, ls] = h
        return carry

    lax.fori_loop(0, tt // rc, chunk, 0)


def _combine_body(ntot_ref, dst_ref, x1_ref, pos_ref, rows_ref, local, pending, sems):
    i = pl.program_id(0)
    slot = i % 2
    tm = x1_ref.shape[0]

    def fetch_tile(t, s):
        def fetch(loc, src):
            _rows_copy(rows_ref, src, local.at[s], loc, sems.at[s]).start()

        return _for_each_chunk(t, ntot_ref, dst_ref, fetch)

    @pl.when(i == 0)
    def _():
        local[...] = jnp.zeros_like(local)
        pending[0] = fetch_tile(0, 0)

    @pl.when(i + 1 < pl.num_programs(0))
    def _():
        pending[1 - slot] = fetch_tile(i + 1, 1 - slot)

    _wait_chunks(pending[slot], lambda r: _rows_copy(rows_ref, 0, local.at[0], 0, sems.at[slot], r).wait())

    pos = pos_ref[...]
    col = lax.broadcasted_iota(I32, (tm, TILE_ROWS), 1).astype(F32)
    pick = jnp.where(jnp.logical_or(col == pos[:, 0:1], col == pos[:, 1:2]), 1.0, 0.0).astype(BF16)
    lo, hi = _unpack_bf16_pairs(local[slot])
    moe = jnp.concatenate([jnp.dot(pick, lo, preferred_element_type=F32),
                           jnp.dot(pick, hi, preferred_element_type=F32)], axis=1)
    return x1_ref[...] + moe


N_PROJ_IN = 12


def _edge_kernel(*refs, with_combine, with_proj, tiles_per_seq):
    refs = list(refs)
    take = lambda n: [refs.pop(0) for _ in range(n)]
    if with_combine:
        ntot_ref, dst_ref = take(2)
    (x_ref,) = take(1)
    if with_combine:
        pos_ref, rows_ref = take(2)
    if with_proj:
        g_ref, w_ref, qg_ref, kg_ref, gm_ref, cw_ref, cb_ref, wa_ref, ba_ref, wx_ref, bx_ref, lam_ref = take(N_PROJ_IN)
    if with_combine:
        (x2_ref,) = take(1)
    if with_proj:
        q_ref, k_ref, v_ref, rnn_ref = take(4)
    if with_combine:
        local, pending, sems = take(3)
    if with_proj:
        xbuf, a_scr, b_scr, h_scr, g_scr = take(5)

    if with_proj:
        @pl.when(pl.program_id(0) % tiles_per_seq == 0)
        def _():
            xbuf[0:8, :] = jnp.zeros((8, D_RNN), F32)
            h_scr[...] = jnp.zeros_like(h_scr)

    if with_combine:
        x = _combine_body(ntot_ref, dst_ref, x_ref, pos_ref, rows_ref, local, pending, sems)
        x2_ref[...] = x
    else:
        x = x_ref[...]
    if with_proj:
        hb = _rms(x, g_ref[...]).astype(BF16)
        _rglru_gates(_proj_section(hb, w_ref, 3), _proj_section(hb, w_ref, 4), cw_ref, cb_ref, wa_ref, ba_ref,
                     wx_ref, bx_ref, lam_ref, xbuf, a_scr, b_scr, g_scr)
        _proj_qkv(hb, w_ref, qg_ref, kg_ref, gm_ref, q_ref, k_ref, v_ref)
        _rglru_scan(rnn_ref, a_scr, b_scr, h_scr, g_scr)


def _edge(x_in, seq_len, combine_in=None, proj_in=None):
    m = x_in.shape[0]
    tm = TOKEN_TILE
    with_combine, with_proj = combine_in is not None, proj_in is not None
    tile = lambda i, *_: (i, 0)
    fix = lambda i, *_: (0, 0)
    operands, in_specs, out_specs, out_shape, scratch = [], [], [], [], []
    n_prefetch = 0
    if with_combine:
        chunk_meta, pos, out_rows = combine_in
        operands += list(chunk_meta)
        n_prefetch = len(chunk_meta)
    operands.append(x_in)
    in_specs.append(pl.BlockSpec((tm, D_MODEL), tile))
    if with_combine:
        operands += [pos, out_rows]
        in_specs += [pl.BlockSpec((tm, LANES), tile), pl.BlockSpec(memory_space=pl.ANY)]
        out_specs.append(pl.BlockSpec((tm, D_MODEL), tile))
        out_shape.append(jax.ShapeDtypeStruct((m, D_MODEL), F32))
        scratch += [pltpu.VMEM((2, TILE_ROWS, D_MODEL // 2), U32), pltpu.SMEM((2,), I32),
                    pltpu.SemaphoreType.DMA((2,))]
    if with_proj:
        assert len(proj_in) == N_PROJ_IN
        operands += list(proj_in)
        in_specs += [pl.BlockSpec(a.shape, fix) for a in proj_in]
        slab = pl.BlockSpec((HEAD_PAIRS, tm, LANES), lambda i, *_: (0, i, 0))
        out_specs += [slab, slab, slab, pl.BlockSpec((tm, D_RNN), tile)]
        out_shape += [jax.ShapeDtypeStruct((HEAD_PAIRS, m, LANES), F32)] * 3 + [jax.ShapeDtypeStruct((m, D_RNN), F32)]
        scratch += [pltpu.VMEM((tm + 8, D_RNN), F32), pltpu.VMEM((tm, D_RNN), F32), pltpu.VMEM((tm, D_RNN), F32),
                    pltpu.VMEM((8, D_RNN), F32), pltpu.VMEM((tm, D_RNN), F32)]
    return pl.pallas_call(
        functools.partial(_edge_kernel, with_combine=with_combine, with_proj=with_proj,
                          tiles_per_seq=seq_len // tm),
        grid_spec=pltpu.PrefetchScalarGridSpec(
            num_scalar_prefetch=n_prefetch, grid=(m // tm,),
            in_specs=in_specs, out_specs=out_specs, scratch_shapes=scratch),
        out_shape=out_shape,
        compiler_params=_cparams("arbitrary"),
    )(*operands)


def _dispatch_plan(cnt_tiles, n_tiles, n_blocks):
    cnt = cnt_tiles.reshape(n_tiles, 8, LANES)[:, 0, ROUTER_LANE0:ROUTER_LANE0 + N_EXPERTS]
    seg = (cnt + ROW_CHUNK - 1) // ROW_CHUNK * ROW_CHUNK
    used = jnp.sum(seg, axis=0)
    padded = (used + EXPERT_BLOCK - 1) // EXPERT_BLOCK * EXPERT_BLOCK
    p_ends = jnp.cumsum(padded)
    p_starts = p_ends - padded
    seg_dst = p_starts[None, :] + jnp.cumsum(seg, axis=0) - seg
    loc_start = jnp.cumsum(seg, axis=1) - seg
    flat = lambda a: a.reshape(-1).astype(I32)
    block_row = jnp.arange(n_blocks, dtype=I32) * EXPERT_BLOCK
    block_expert = jnp.minimum(jnp.sum((p_ends[None, :] <= block_row[:, None]).astype(I32), axis=1), N_EXPERTS - 1)
    n_used = (p_ends[-1:] // EXPERT_BLOCK).astype(I32)
    after = (p_ends // EXPERT_BLOCK).astype(I32)[block_expert]
    next_expert = jnp.where(after < n_used[0], block_expert[jnp.minimum(after, n_blocks - 1)], -1).astype(I32)
    ends = jnp.cumsum(seg // ROW_CHUNK, axis=1)
    j = jnp.arange(TILE_CHUNKS, dtype=I32)
    owner = jnp.minimum(jnp.sum((ends[:, None, :] <= j[None, :, None]).astype(I32), axis=2), N_EXPERTS - 1)
    is_owner = owner[:, :, None] == jnp.arange(N_EXPERTS, dtype=I32)[None, None, :]
    chunk_dst = jnp.sum(jnp.where(is_owner, (seg_dst - loc_start)[:, None, :], 0), axis=2) + ROW_CHUNK * j[None, :]
    chunk_meta = (flat(ends[:, -1]), flat(chunk_dst))
    rest = n_blocks * EXPERT_BLOCK - p_ends[-1:]
    tail_meta = (flat(jnp.concatenate([padded - used, rest]) // ROW_CHUNK),
                 flat(jnp.concatenate([p_starts + used, p_ends[-1:]])))
    occupied_end = jnp.sum(jnp.where(block_expert[:, None] == jnp.arange(N_EXPERTS, dtype=I32)[None, :],
                                     (p_starts + used)[None, :], 0), axis=1)
    valid = jnp.clip(occupied_end - block_row, 0, EXPERT_BLOCK).astype(I32)
    return chunk_meta, tail_meta, (block_expert.astype(I32), n_used, next_expert, valid)


def _constants():
    gm = np.kron(np.eye(MXU_DEPTH // HEAD_DIM), np.full((HEAD_DIM, HEAD_DIM), 1.0 / HEAD_DIM))
    tri = np.tril(np.ones((TOKEN_TILE, TOKEN_TILE), np.float32), -1)
    upper = np.triu(np.ones((LANES, LANES), np.float32), 1)
    return jnp.asarray(gm, BF16), jnp.asarray(tri, BF16), jnp.asarray(upper, BF16)


def kernel(x, rel_bias_table, norm_mix, w_in, q_norm, k_norm, conv_w, conv_b, rg_w_a, rg_b_a, rg_w_x, rg_b_x,
           rg_lambda, norm_attn_out, norm_rnn_out, w_out, norm_ffn, router_group_w, router_group_b,
           router_expert_w, router_expert_b, expert_w_gate, expert_w_up, expert_w_down):
    b, s, d = x.shape
    depth = w_in.shape[0]
    m = b * s
    assert d == D_MODEL and s % ATTN_TILE == 0 and s % TOKEN_TILE == 0

    gm, tri, upper = _constants()
    bias = _bias_tables(rel_bias_table)
    scale = HEAD_DIM ** -0.5
    n_tiles = m // TOKEN_TILE
    n_blocks = -(-(m * TOP_K + n_tiles * N_EXPERTS * (ROW_CHUNK - 1)) // EXPERT_BLOCK) + N_EXPERTS
    n_rows = n_blocks * EXPERT_BLOCK
    row1 = lambda v: v.reshape(1, -1).astype(F32)

    def proj_params(l):
        return (row1(norm_mix[l]), w_in[l].astype(BF16),
                row1(jnp.tile(q_norm[l], ATTN_HEADS) * (scale * LOG2E)), row1(jnp.tile(k_norm[l], ATTN_HEADS)), gm,
                conv_w[l].astype(F32), row1(conv_b[l]),
                _block_diag(rg_w_a[l]).astype(BF16), row1(rg_b_a[l]),
                _block_diag(rg_w_x[l]).astype(BF16), row1(rg_b_x[l]), row1(rg_lambda[l]))

    x2 = x.reshape(m, d).astype(F32)
    q, k, v, rnn = _edge(x2, s, proj_in=proj_params(0))
    for l in range(depth):
        attn = _attention(q, k, v, bias, b, s)

        wr = jnp.zeros((D_MODEL, LANES), F32)
        wr = wr.at[:, :N_GROUPS].set(router_group_w[l]).at[:, N_GROUPS:N_GROUPS + N_EXPERTS].set(router_expert_w[l])
        wr_pair = jnp.concatenate(_split_bf16(wr), axis=1)
        br = jnp.zeros((1, LANES), F32)
        br = br.at[0, :N_GROUPS].set(router_group_b[l]).at[0, N_GROUPS:N_GROUPS + N_EXPERTS].set(router_expert_b[l])
        x1, h2, pos, gates, cnt = _mix(x2, attn, rnn, row1(norm_attn_out[l]), row1(norm_rnn_out[l]),
                                       w_out[l].astype(BF16), row1(norm_ffn[l]), wr_pair, br, tri, upper)

        chunk_meta, tail_meta, block_meta = _dispatch_plan(cnt, n_tiles, n_blocks)
        rows = _dispatch(chunk_meta + tail_meta, h2, pos, gates, n_rows)
        out_rows = _experts(*block_meta, rows, expert_w_gate, expert_w_up, expert_w_down, l)
        nxt = proj_params(l + 1) if l + 1 < depth else None
        x2, *started = _edge(x1, s, combine_in=(chunk_meta, pos, out_rows), proj_in=nxt)
        if started:
            q, k, v, rnn = started
    return x2.reshape(b, s, d).astype(x.dtype)
```

```python
import functools
import math

import numpy as np
import jax
import jax.numpy as jnp
from jax import lax
from jax.experimental import pallas as pl
from jax.experimental.pallas import tpu as pltpu

F32 = jnp.float32
BF16 = jnp.bfloat16
I32 = jnp.int32
U32 = jnp.uint32

D_MODEL = 1024
ATTN_HEADS = 8
HEAD_DIM = 64
D_ATTN = ATTN_HEADS * HEAD_DIM
RNN_BLOCKS = 8
D_RNN = 512
D_MIX = D_ATTN + D_RNN
D_IN = 3 * D_ATTN + 2 * D_RNN
DILATED_BRANCHES = ((128, 1), (512, 4), (2048, 16))
Q_BLOCK = 128
REL_BUCKETS = 32
REL_MAX_DIST = 2048
CONV_WIDTH = 4
RG_C = 8.0
N_GROUPS = 4
EXPERTS_PER_GROUP = 8
N_EXPERTS = N_GROUPS * EXPERTS_PER_GROUP
TOP_K = 2
D_EXPERT = 512
EPS = 1e-6
NEG_INF = -1e30
LOG2E = math.log2(math.e)

LANES = 128
SUBLANES = 8
MXU_DEPTH = 256
HEAD_PAIRS = D_ATTN // LANES
ROUTER_LANE0 = N_GROUPS
TOKEN_TILE = 512
ATTN_TILE = 2048
ATTN_PAIRS = 2
ATTN_UNROLL = 16
RNN_CHUNK = 64
EXPERT_SUBBLOCK = 128
EXPERT_BLOCK = 512
ROW_CHUNK = 8
TILE_ROWS = -(-(TOKEN_TILE * TOP_K + N_EXPERTS * (ROW_CHUNK - 1)) // 256) * 256
TILE_CHUNKS = TILE_ROWS // ROW_CHUNK
WAIT_BATCH = 16
ROW_WORDS = D_MODEL // 2 + LANES
MERGE_ROWS = 256
VMEM_LIMIT = 48 * 1024 * 1024
ATTN_VMEM_LIMIT = 56 * 1024 * 1024


def _cparams(*sem):
    return pltpu.CompilerParams(dimension_semantics=sem, vmem_limit_bytes=VMEM_LIMIT)


def _rms(x, gain):
    return x * lax.rsqrt(jnp.mean(x * x, axis=-1, keepdims=True) + EPS) * gain


def _rows(start, size, stride):
    return pl.ds(start, size) if stride == 1 else pl.ds(start, size, stride=stride)


def _attn_kernel(q_ref, k_ref, v_ref, bias_ref, out_ref, qd, stage, o_scr, l_scr, *kv_scr):
    last = len(DILATED_BRANCHES) - 1
    for p in range(ATTN_PAIRS):
        for n, (_, dil) in enumerate(DILATED_BRANCHES):
            o_dst = out_ref.at[p] if n == last else o_scr.at[n, p]
            _attn_pair(q_ref.at[p], k_ref.at[p], v_ref.at[p], bias_ref.at[n], 2 * p, o_dst, l_scr.at[n, p],
                       qd, kv_scr[2 * n].at[p], kv_scr[2 * n + 1].at[p], stage, dil)

        def merge(c, carry):
            rows = pl.ds(pl.multiple_of(c * MERGE_ROWS, MERGE_ROWS), MERGE_ROWS)
            lse = [l_scr[n, p, rows, :] for n in range(last + 1)]
            mx = functools.reduce(jnp.maximum, lse)
            w = [jnp.exp2(l - mx) for l in lse]
            o = [o_scr[n, p, rows, :] for n in range(last)] + [out_ref[p, rows, :]]
            num = functools.reduce(jnp.add, [wi * oi for wi, oi in zip(w, o)])
            out_ref[p, rows, :] = num * (1.0 / functools.reduce(jnp.add, w))
            return carry

        lax.fori_loop(0, out_ref.shape[1] // MERGE_ROWS, merge, 0)


def _attn_pair(q_ref, k_ref, v_ref, bias_ref, head0, o_ref, lse_ref, qd, kd, vd, stage, dil):
    i = pl.program_id(2)
    tile = q_ref.shape[0]
    rows = tile // dil
    seg = Q_BLOCK + rows

    @pl.when(i == 0)
    def _():
        for r in range(dil):
            kd[r * seg:r * seg + Q_BLOCK, :] = jnp.zeros((Q_BLOCK, LANES), BF16)
            for hh in range(2):
                vd[hh, r * seg:r * seg + Q_BLOCK, :] = jnp.zeros((Q_BLOCK, LANES), BF16)

    @pl.when(i > 0)
    def _():
        for r in range(dil):
            kd[r * seg:r * seg + Q_BLOCK, :] = kd[r * seg + rows:(r + 1) * seg, :]
            for hh in range(2):
                vd[hh, r * seg:r * seg + Q_BLOCK, :] = vd[hh, r * seg + rows:(r + 1) * seg, :]

    def regroup(src_ref, put):
        if dil % 16 == 0:
            quarter = tile // 4
            for c in range(4):
                stage[0, c * quarter:(c + 1) * quarter, :] = src_ref[pl.ds(c, quarter, stride=4), :]
            for r in range(dil):
                put(r, stage[0, _rows((r % 4) * quarter + r // 4, rows, dil // 4), :].astype(BF16))
        else:
            for r in range(dil):
                put(r, src_ref[_rows(r, rows, dil), :].astype(BF16))

    def put_q(r, x):
        qd[r * rows:(r + 1) * rows, :] = x

    def put_k(r, x):
        kd[r * seg + Q_BLOCK:(r + 1) * seg, :] = x

    first_head = lax.broadcasted_iota(I32, (rows, LANES), 1) < HEAD_DIM

    def put_v(r, x):
        vd[0, r * seg + Q_BLOCK:(r + 1) * seg, :] = jnp.where(first_head, x, jnp.zeros_like(x))
        vd[1, r * seg + Q_BLOCK:(r + 1) * seg, :] = jnp.where(first_head, jnp.zeros_like(x), x)

    regroup(q_ref, put_q)
    regroup(k_ref, put_k)
    regroup(v_ref, put_v)

    per = rows // Q_BLOCK
    low = lax.broadcasted_iota(I32, (Q_BLOCK, LANES), 1) < HEAD_DIM
    key_lane = lax.broadcasted_iota(I32, (2 * Q_BLOCK, LANES), 1)
    ones = [jnp.where(key_lane < HEAD_DIM, 1.0, 0.0).astype(BF16), jnp.where(key_lane < HEAD_DIM, 0.0, 1.0).astype(BF16)]

    def block(blk, carry):
        r = blk // per
        n = blk % per
        q0 = pl.multiple_of(r * rows + n * Q_BLOCK, Q_BLOCK)
        k0 = pl.multiple_of(r * seg + n * Q_BLOCK, Q_BLOCK)
        sel = jnp.where(jnp.logical_and(i == 0, n == 0), 1, 0)
        q = qd[pl.ds(q0, Q_BLOCK), :]
        kk = kd[pl.ds(k0, 2 * Q_BLOCK), :]
        probs, maxes = [], []
        for hh in range(2):
            qm = jnp.where(low if hh == 0 else jnp.logical_not(low), q, jnp.zeros_like(q))
            s = lax.dot_general(qm, kk, (((1,), (1,)), ((), ())), preferred_element_type=F32)
            s = s + bias_ref[sel, head0 + hh]
            mx = jnp.max(s, axis=-1, keepdims=True)
            probs.append(jnp.exp2(s - mx).astype(BF16))
            maxes.append(mx)
        rhs = jnp.concatenate([jnp.concatenate([vd[hh, pl.ds(k0, 2 * Q_BLOCK), :], ones[hh]], axis=1)
                               for hh in range(2)], axis=0)
        both = jnp.dot(jnp.concatenate(probs, axis=1), rhs, preferred_element_type=F32)
        l = both[:, LANES:]
        o_tile = both[:, :LANES] * (1.0 / l)
        lse_tile = jnp.where(low, maxes[0], maxes[1]) + jnp.log(l) * LOG2E
        if dil % 16 == 0:
            dst = _rows((r % 4) * (tile // 4) + r // 4 + (dil // 4) * Q_BLOCK * n, Q_BLOCK, dil // 4)
            stage[1, dst, :] = o_tile
            stage[2, dst, :] = lse_tile
        else:
            dst = _rows(r + dil * Q_BLOCK * n, Q_BLOCK, dil)
            o_ref[dst, :] = o_tile
            lse_ref[dst, :] = lse_tile
        return carry

    lax.fori_loop(0, dil * per, block, 0, unroll=ATTN_UNROLL)
    if dil % 16 == 0:
        quarter = tile // 4
        for c in range(4):
            o_ref[pl.ds(c, quarter, stride=4), :] = stage[1, c * quarter:(c + 1) * quarter, :]
            lse_ref[pl.ds(c, quarter, stride=4), :] = stage[2, c * quarter:(c + 1) * quarter, :]


def _attention(q, k, v, bias, b, s):
    tile = ATTN_TILE
    nt = s // tile
    nb = len(DILATED_BRANCHES)
    blk = pl.BlockSpec((ATTN_PAIRS, tile, LANES), lambda bb, g, i: (g, bb * nt + i, 0))
    kv_scr = []
    for _, dil in DILATED_BRANCHES:
        keys = tile + Q_BLOCK * dil
        kv_scr += [pltpu.VMEM((ATTN_PAIRS, keys, LANES), BF16), pltpu.VMEM((ATTN_PAIRS, 2, keys, LANES), BF16)]
    return pl.pallas_call(
        _attn_kernel,
        grid=(b, HEAD_PAIRS // ATTN_PAIRS, nt),
        in_specs=[blk, blk, blk,
                  pl.BlockSpec((nb, 2, 2 * ATTN_PAIRS, Q_BLOCK, 2 * Q_BLOCK), lambda bb, g, i: (0, 0, g, 0, 0))],
        out_specs=blk,
        out_shape=jax.ShapeDtypeStruct(q.shape, F32),
        scratch_shapes=[pltpu.VMEM((tile, LANES), BF16), pltpu.VMEM((3, tile, LANES), F32),
                        pltpu.VMEM((nb - 1, ATTN_PAIRS, tile, LANES), F32),
                        pltpu.VMEM((nb, ATTN_PAIRS, tile, LANES), F32)] + kv_scr,
        compiler_params=pltpu.CompilerParams(dimension_semantics=("parallel", "parallel", "arbitrary"),
                                             vmem_limit_bytes=ATTN_VMEM_LIMIT),
    )(q, k, v, bias)


def _bias_step_tables():
    exact = REL_BUCKETS // 2
    i = np.arange(Q_BLOCK)[:, None]
    j = np.arange(2 * Q_BLOCK)[None, :]
    steps = i + Q_BLOCK - j
    onehot, band = [], []
    for window, dil in DILATED_BRANCHES:
        dist = (np.arange(2 * Q_BLOCK) * dil).astype(np.int32)
        d = np.maximum(dist, 1).astype(np.float32)
        log_b = exact + (np.log(d / np.float32(exact)) / np.float32(math.log(REL_MAX_DIST / exact))
                         * np.float32(REL_BUCKETS - exact)).astype(np.int32)
        bucket = np.where(dist < exact, dist, np.minimum(log_b, REL_BUCKETS - 1))
        onehot.append(np.eye(REL_BUCKETS, dtype=np.float32)[bucket])
        band.append((steps >= 0) & (steps <= window // dil))
    first = np.broadcast_to(j >= Q_BLOCK, (Q_BLOCK, 2 * Q_BLOCK))
    return np.stack(onehot), np.stack(band), first


def _bias_tables(rel_table):
    onehot, band, first = _bias_step_tables()
    nb = len(DILATED_BRANCHES)
    vec = LOG2E * jnp.einsum('nsb,bh->nhs', onehot, rel_table.astype(F32), precision=lax.Precision.HIGHEST)
    period = 3 * Q_BLOCK
    padded = jnp.concatenate([vec, jnp.zeros((nb, ATTN_HEADS, Q_BLOCK), F32)], axis=-1)
    wrapped = jnp.roll(padded[..., ::-1], Q_BLOCK + 1, axis=-1)
    bias = jnp.tile(wrapped, (1, 1, Q_BLOCK))[..., :Q_BLOCK * (period - 1)].reshape(
        nb, ATTN_HEADS, Q_BLOCK, period - 1)[..., :2 * Q_BLOCK]
    regular = jnp.where(band[:, None], bias, NEG_INF)
    start = jnp.where((band & first[None])[:, None], bias, NEG_INF)
    return jnp.stack([regular, start], axis=1)


def _block_diag(w, group):
    *lead, nb, n, _ = w.shape
    w = w.reshape(*lead, nb // group, group, n, n)
    eye = jnp.eye(group, dtype=w.dtype)
    return (eye[:, None, :, None] * w[..., :, :, None, :]).reshape(*lead, nb // group, group * n, group * n)


def _split_bf16(x):
    hi = x.astype(BF16)
    return hi, (x - hi.astype(F32)).astype(BF16)


def _mix_kernel(x_ref, attn_ref, rnn_ref, ga_ref, gn_ref, wo_ref, gf_ref,
                wr_ref, br_ref, tri_ref, upper_ref,
                x1_ref, h2_ref, pos_ref, gate_ref, cnt_ref):
    slabs = [attn_ref[hp] for hp in range(HEAD_PAIRS)]
    sumsq = functools.reduce(jnp.add, [jnp.sum(a * a, axis=-1, keepdims=True) for a in slabs])
    scale = lax.rsqrt(sumsq * (1.0 / D_ATTN) + EPS)
    na = (jnp.concatenate([a * scale for a in slabs], axis=1) * ga_ref[...]).astype(BF16)
    nr = _rms(rnn_ref[...], gn_ref[...]).astype(BF16)
    x1 = (x_ref[...] + jnp.dot(na, wo_ref[0:D_ATTN, :], preferred_element_type=F32)
          + jnp.dot(nr, wo_ref[D_ATTN:, :], preferred_element_type=F32))
    x1_ref[...] = x1
    h2 = _rms(x1, gf_ref[...])
    h2_ref[...] = h2.astype(BF16)

    hh, hl = _split_bf16(h2)
    by_hi = jnp.dot(hh, wr_ref[...], preferred_element_type=F32)
    by_lo = jnp.dot(hl, wr_ref[...], preferred_element_type=F32)
    logits = by_hi[:, :LANES] + by_hi[:, LANES:] + by_lo[:, :LANES] + br_ref[...]
    tm = logits.shape[0]
    lane = lax.broadcasted_iota(I32, (tm, LANES), 1)
    lanef = lane.astype(F32)
    big = float(LANES)

    def top(vals):
        m = jnp.max(vals, axis=-1, keepdims=True)
        return m, jnp.min(jnp.where(vals == m, lanef, big), axis=-1, keepdims=True)

    is_group = lane < N_GROUPS
    gmax, gsel = top(jnp.where(is_group, logits, NEG_INF))
    g_w = 1.0 / jnp.sum(jnp.where(is_group, jnp.exp(logits - gmax), 0.0), axis=-1, keepdims=True)
    lo_lane = ROUTER_LANE0 + EXPERTS_PER_GROUP * gsel
    in_group = jnp.logical_and(lanef >= lo_lane, lanef < lo_lane + EXPERTS_PER_GROUP)
    el = jnp.where(in_group, logits, NEG_INF)
    v1, i1 = top(el)
    v2, i2 = top(jnp.where(lanef == i1, NEG_INF, el))
    t = jnp.exp(v2 - v1)
    p1 = 1.0 / (1.0 + t)
    gate1 = g_w * p1
    gate2 = g_w * (t * p1)

    oh1 = (lanef == i1).astype(F32)
    oh2 = (lanef == i2).astype(F32)
    cnt = oh1 + oh2
    prefix = jnp.dot(tri_ref[...], cnt.astype(BF16), preferred_element_type=F32)
    total = jnp.sum(cnt, axis=0, keepdims=True)
    chunks = jnp.floor((total + (ROW_CHUNK - 1)) * (1.0 / ROW_CHUNK))
    seg_start = ROW_CHUNK * jnp.dot(jnp.broadcast_to(chunks, (8, LANES)).astype(BF16), upper_ref[...],
                                    preferred_element_type=F32)[0:1, :]
    base = seg_start + prefix
    pos1 = jnp.sum(oh1 * base, axis=-1, keepdims=True)
    pos2 = jnp.sum(oh2 * base, axis=-1, keepdims=True)
    cnt_ref[...] = jnp.broadcast_to(total, cnt_ref.shape).astype(I32)
    pos_ref[...] = jnp.where(lane == 0, pos1, jnp.where(lane == 1, pos2, 0.0))
    gate_ref[...] = jnp.where(lane == 0, gate1, jnp.where(lane == 1, gate2, 0.0))


def _layer_spec(a, layer):
    return pl.BlockSpec((None,) + a.shape[1:], lambda i, *_: (layer,) + (0,) * (a.ndim - 1))


def _mix(x2, attn, rnn, layer_params, layer, tri, upper):
    m = x2.shape[0]
    tm = TOKEN_TILE
    row = lambda i: (i, 0)
    fix = lambda i: (0, 0)
    t512 = pl.BlockSpec((tm, D_ATTN), row)
    t128 = pl.BlockSpec((tm, LANES), row)
    t1024 = pl.BlockSpec((tm, D_MODEL), row)
    slab = pl.BlockSpec((HEAD_PAIRS, tm, LANES), lambda i: (0, i, 0))
    return pl.pallas_call(
        _mix_kernel,
        grid=(m // tm,),
        in_specs=[t1024, slab, t512] + [_layer_spec(a, layer) for a in layer_params]
        + [pl.BlockSpec((tm, tm), fix), pl.BlockSpec((LANES, LANES), fix)],
        out_specs=[t1024, t1024, t128, t128, pl.BlockSpec((8, LANES), row)],
        out_shape=[jax.ShapeDtypeStruct((m, D_MODEL), F32), jax.ShapeDtypeStruct((m, D_MODEL), BF16),
                   jax.ShapeDtypeStruct((m, LANES), F32), jax.ShapeDtypeStruct((m, LANES), F32),
                   jax.ShapeDtypeStruct((m // tm * 8, LANES), I32)],
        compiler_params=_cparams("parallel"),
    )(x2, attn, rnn, *layer_params, tri, upper)


def _rows_copy(src, src_row, dst, dst_row, sem, rows=ROW_CHUNK):
    return pltpu.make_async_copy(src.at[pl.ds(src_row, rows)], dst.at[pl.ds(dst_row, rows)], sem)


def _for_each_chunk(i, ntot_ref, dst_ref, copy):
    n = ntot_ref[i]

    def body(j, carry):
        copy(pl.multiple_of(j * ROW_CHUNK, ROW_CHUNK), pl.multiple_of(dst_ref[i * TILE_CHUNKS + j], ROW_CHUNK))
        return carry

    lax.fori_loop(0, n, body, 0)
    return n


def _wait_chunks(count, wait_rows):
    def batch(c, carry):
        wait_rows(WAIT_BATCH * ROW_CHUNK)
        return carry

    def single(c, carry):
        wait_rows(ROW_CHUNK)
        return carry

    lax.fori_loop(0, count // WAIT_BATCH, batch, 0)
    lax.fori_loop(0, count % WAIT_BATCH, single, 0)


def _pack_bf16_pairs(x):
    c = x.shape[1] // 2
    lo = lax.bitcast_convert_type(x[:, :c], U32) >> 16
    hi = lax.bitcast_convert_type(x[:, c:], U32) & jnp.uint32(0xFFFF0000)
    return hi | lo


def _unpack_bf16_pairs(w):
    lo = lax.bitcast_convert_type(w << 16, F32).astype(BF16)
    hi = lax.bitcast_convert_type(w & jnp.uint32(0xFFFF0000), F32).astype(BF16)
    return lo, hi


def _dispatch_kernel(ntot_ref, dst_ref, tailn_ref, taildst_ref,
                     h_ref, pos_ref, gate_ref, rows_ref, sorted_buf, zero_buf, pending, sems, *, tm):
    i = pl.program_id(0)
    slot = i % 2
    pos_t = pos_ref[...].T
    gate_t = gate_ref[...].T
    row = lax.broadcasted_iota(I32, (TILE_ROWS, tm), 0).astype(F32)
    hit1 = row == pos_t[0:1, :]
    hit2 = row == pos_t[1:2, :]
    onehot = jnp.where(jnp.logical_or(hit1, hit2), 1.0, 0.0).astype(BF16)
    feat = jnp.dot(onehot, h_ref[...], preferred_element_type=F32)
    sorted_buf[slot, :, 0:ROW_WORDS - LANES] = _pack_bf16_pairs(feat)
    gate = jnp.sum(jnp.where(hit1, gate_t[0:1, :], 0.0) + jnp.where(hit2, gate_t[1:2, :], 0.0),
                   axis=-1, keepdims=True)
    lane = lax.broadcasted_iota(I32, (TILE_ROWS, LANES), 1)
    sorted_buf[slot, :, ROW_WORDS - LANES:] = jnp.where(lane == 0, lax.bitcast_convert_type(gate, U32), jnp.uint32(0))

    def wait_rows(sem):
        return lambda r: _rows_copy(sorted_buf.at[0], 0, rows_ref, 0, sem, r).wait()

    @pl.when(i > 0)
    def _():
        _wait_chunks(pending[0], wait_rows(sems.at[1 - slot]))

    def send(loc, dst):
        _rows_copy(sorted_buf.at[slot], loc, rows_ref, dst, sems.at[slot]).start()

    n_sent = _for_each_chunk(i, ntot_ref, dst_ref, send)

    @pl.when(i == 0)
    def _():
        zero_buf[...] = jnp.zeros_like(zero_buf)

    def tails():
        def per_expert(e, total):
            n = tailn_ref[e]
            dst0 = taildst_ref[e]
            big = n // WAIT_BATCH

            def batch(c, carry):
                _rows_copy(zero_buf, 0, rows_ref, pl.multiple_of(dst0 + c * (WAIT_BATCH * ROW_CHUNK), ROW_CHUNK),
                           sems.at[slot], WAIT_BATCH * ROW_CHUNK).start()
                return carry

            def single(c, carry):
                _rows_copy(zero_buf, 0, rows_ref, pl.multiple_of(dst0 + c * ROW_CHUNK, ROW_CHUNK),
                           sems.at[slot]).start()
                return carry

            lax.fori_loop(0, big, batch, 0)
            lax.fori_loop(big * WAIT_BATCH, n, single, 0)
            return total + n

        return lax.fori_loop(0, N_EXPERTS + 1, per_expert, jnp.int32(0))

    n_sent = n_sent + lax.cond(i == 0, tails, lambda: jnp.int32(0))
    pending[0] = n_sent

    @pl.when(i == pl.num_programs(0) - 1)
    def _():
        _wait_chunks(n_sent, wait_rows(sems.at[slot]))


def _dispatch(meta, h2, pos, gates, n_rows):
    m = h2.shape[0]
    tm = TOKEN_TILE
    tile = lambda i, *_: (i, 0)
    return pl.pallas_call(
        functools.partial(_dispatch_kernel, tm=tm),
        grid_spec=pltpu.PrefetchScalarGridSpec(
            num_scalar_prefetch=4,
            grid=(m // tm,),
            in_specs=[pl.BlockSpec((tm, D_MODEL), tile), pl.BlockSpec((tm, LANES), tile),
                      pl.BlockSpec((tm, LANES), tile)],
            out_specs=pl.BlockSpec(memory_space=pl.ANY),
            scratch_shapes=[pltpu.VMEM((2, TILE_ROWS, ROW_WORDS), U32),
                            pltpu.VMEM((WAIT_BATCH * ROW_CHUNK, ROW_WORDS), U32),
                            pltpu.SMEM((1,), I32), pltpu.SemaphoreType.DMA((2,))],
        ),
        out_shape=jax.ShapeDtypeStruct((n_rows, ROW_WORDS), U32),
        compiler_params=_cparams("arbitrary"),
    )(*meta, h2, pos, gates)


def _expert_kernel(be_ref, nused_ref, next_ref, valid_ref, rows_ref, wg_hbm, wu_hbm, wd_hbm, out_ref,
                   wgf, wuf, wdf, wgb, wub, wdb, holder, sems, *, layer):
    i = pl.program_id(0)
    used = i < nused_ref[0]
    expert = be_ref[i]
    first = jnp.logical_or(i == 0, expert != be_ref[jnp.maximum(i - 1, 0)])

    def weight_copies(e, slot):
        return (pltpu.make_async_copy(wg_hbm.at[layer, e], wgf.at[slot], sems.at[slot, 0]),
                pltpu.make_async_copy(wu_hbm.at[layer, e], wuf.at[slot], sems.at[slot, 1]),
                pltpu.make_async_copy(wd_hbm.at[layer, e], wdf.at[slot], sems.at[slot, 2]))

    @pl.when(jnp.logical_and(used, first))
    def _():
        @pl.when(i == 0)
        def _():
            holder[0] = 0
            for c in weight_copies(expert, 0):
                c.start()

        slot = holder[0]
        for c in weight_copies(expert, slot):
            c.wait()
        wgb[...] = wgf[slot].astype(BF16)
        wub[...] = wuf[slot].astype(BF16)
        wdb[...] = wdf[slot].astype(BF16)
        nxt = next_ref[i]

        @pl.when(nxt >= 0)
        def _():
            for c in weight_copies(nxt, 1 - slot):
                c.start()

        holder[0] = 1 - slot

    def mlp(r0, nrows):
        rows = rows_ref[r0:r0 + nrows, :]
        xb = jnp.concatenate(_unpack_bf16_pairs(rows[:, 0:ROW_WORDS - LANES]), axis=1)
        gate = lax.bitcast_convert_type(rows[:, ROW_WORDS - LANES:][:, 0:1], F32)
        g = jnp.dot(xb, wgb[...], preferred_element_type=F32)
        u = jnp.dot(xb, wub[...], preferred_element_type=F32)
        act = (g * jax.nn.sigmoid(g) * u).astype(BF16)
        y = jnp.dot(act, wdb[...], preferred_element_type=F32) * gate
        out_ref[r0:r0 + nrows, :] = _pack_bf16_pairs(y.astype(BF16).astype(F32))

    blk = rows_ref.shape[0]
    valid = valid_ref[i]

    fused = valid > blk // 2

    @pl.when(fused)
    def _():
        mlp(0, blk)

    @pl.when(jnp.logical_not(fused))
    def _():
        for r0 in range(0, blk, EXPERT_SUBBLOCK):
            @pl.when(r0 < valid)
            def _():
                mlp(r0, EXPERT_SUBBLOCK)

            @pl.when(r0 >= valid)
            def _():
                out_ref[r0:r0 + EXPERT_SUBBLOCK, :] = jnp.zeros((EXPERT_SUBBLOCK, out_ref.shape[1]), out_ref.dtype)


def _experts(block_expert, n_used, next_expert, valid, rows, w_gate, w_up, w_down, layer):
    n_rows = rows.shape[0]
    blk = EXPERT_BLOCK
    rmap = lambda i, be, nu, nx, va: (jnp.minimum(i, nu[0] - 1), 0)
    hbm = pl.BlockSpec(memory_space=pl.ANY)
    return pl.pallas_call(
        functools.partial(_expert_kernel, layer=layer),
        grid_spec=pltpu.PrefetchScalarGridSpec(
            num_scalar_prefetch=4,
            grid=(n_rows // blk,),
            in_specs=[pl.BlockSpec((blk, ROW_WORDS), rmap), hbm, hbm, hbm],
            out_specs=pl.BlockSpec((blk, D_MODEL // 2), lambda i, be, nu, nx, va: (i, 0)),
            scratch_shapes=[pltpu.VMEM((2, D_MODEL, D_EXPERT), F32), pltpu.VMEM((2, D_MODEL, D_EXPERT), F32),
                            pltpu.VMEM((2, D_EXPERT, D_MODEL), F32),
                            pltpu.VMEM((D_MODEL, D_EXPERT), BF16), pltpu.VMEM((D_MODEL, D_EXPERT), BF16),
                            pltpu.VMEM((D_EXPERT, D_MODEL), BF16),
                            pltpu.SMEM((1,), I32), pltpu.SemaphoreType.DMA((2, 3))],
        ),
        out_shape=jax.ShapeDtypeStruct((n_rows, D_MODEL // 2), U32),
        compiler_params=_cparams("arbitrary"),
    )(block_expert, n_used, next_expert, valid, rows, w_gate, w_up, w_down)


def _proj_section(hb, w_ref, n):
    return jnp.dot(hb, w_ref[:, n * D_ATTN:(n + 1) * D_ATTN], preferred_element_type=F32)


def _proj_qkv(hb, w_ref, qg_ref, kg_ref, gm_ref, q_ref, k_ref, v_ref):
    sec = functools.partial(_proj_section, hb, w_ref)

    def head_norm(z, gain):
        zz = (z * z).astype(BF16)
        half = gm_ref.shape[0]
        ms = jnp.concatenate([jnp.dot(zz[:, c:c + half], gm_ref[...], preferred_element_type=F32)
                              for c in range(0, D_ATTN, half)], axis=1)
        return z * lax.rsqrt(ms + EPS) * gain

    def put_slabs(ref, z):
        for hp in range(HEAD_PAIRS):
            ref[hp] = z[:, hp * LANES:(hp + 1) * LANES]

    put_slabs(q_ref, head_norm(sec(0), qg_ref[...]))
    put_slabs(k_ref, head_norm(sec(1), kg_ref[...]))
    put_slabs(v_ref, sec(2))


def _rglru_gates(xr, gr, cw_ref, cb_ref, wa_ref, ba_ref, wx_ref, bx_ref, lam_ref, xbuf, a_scr, b_scr, g_scr):
    tt = xr.shape[0]
    xbuf[8:8 + tt, :] = xr
    g_scr[...] = jax.nn.gelu(gr, approximate=True)
    u = cb_ref[...] + cw_ref[3:4, :] * xr
    for back in range(1, CONV_WIDTH):
        u = u + cw_ref[3 - back:4 - back, :] * xbuf[8 - back:8 - back + tt, :]
    xbuf[0:8, :] = xbuf[tt:tt + 8, :]

    ub = u.astype(BF16)

    def gate(w_ref, b_ref):
        deep = w_ref.shape[1]
        z = jnp.concatenate([jnp.dot(ub[:, c * deep:(c + 1) * deep], w_ref[c], preferred_element_type=F32)
                             for c in range(w_ref.shape[0])], axis=1)
        return jax.nn.sigmoid(z + b_ref[...])

    r = gate(wa_ref, ba_ref)
    gi = gate(wx_ref, bx_ref)
    nl = -lam_ref[...]
    softplus = jnp.maximum(nl, 0.0) + jnp.log1p(jnp.exp(-jnp.abs(nl)))
    log_a = (-RG_C) * r * softplus
    a = jnp.exp(log_a)
    a_scr[...] = a
    b_scr[...] = jnp.sqrt(-jnp.tanh(log_a) * (a * a + 1.0)) * (gi * u)


def _rglru_scan(out_ref, a_scr, b_scr, h_scr, g_scr):
    tt = out_ref.shape[0]
    rc = RNN_CHUNK
    sub = lax.broadcasted_iota(I32, (SUBLANES, LANES), 0)

    def chunk(c, carry):
        r0 = pl.multiple_of(c * rc, rc)
        for g in range(D_RNN // LANES):
            ls = slice(g * LANES, (g + 1) * LANES)
            h = h_scr[0:1, ls]
            for v in range(rc // SUBLANES):
                rows = pl.ds(r0 + v * SUBLANES, SUBLANES)
                aa = a_scr[rows, ls]
                bb = b_scr[rows, ls]
                k = 1
                while k < SUBLANES:
                    keep = sub >= k
                    a_sh = pltpu.roll(aa, k, 0)
                    b_sh = pltpu.roll(bb, k, 0)
                    bb = jnp.where(keep, aa * b_sh + bb, bb)
                    aa = jnp.where(keep, aa * a_sh, aa)
                    k *= 2
                hv = aa * h + bb
                h = hv[SUBLANES - 1:SUBLANES, :]
                out_ref[rows, ls] = hv * g_scr[rows, ls]
            h_scr[0:1, ls] = h
        return carry

    lax.fori_loop(0, tt // rc, chunk, 0)


def _combine_body(ntot_ref, dst_ref, x1_ref, pos_ref, rows_ref, local, pending, sems):
    i = pl.program_id(0)
    slot = i % 2
    tm = x1_ref.shape[0]

    def fetch_tile(t, s):
        def fetch(loc, src):
            _rows_copy(rows_ref, src, local.at[s], loc, sems.at[s]).start()

        return _for_each_chunk(t, ntot_ref, dst_ref, fetch)

    @pl.when(i == 0)
    def _():
        local[...] = jnp.zeros_like(local)
        pending[0] = fetch_tile(0, 0)

    @pl.when(i + 1 < pl.num_programs(0))
    def _():
        pending[1 - slot] = fetch_tile(i + 1, 1 - slot)

    _wait_chunks(pending[slot], lambda r: _rows_copy(rows_ref, 0, local.at[0], 0, sems.at[slot], r).wait())

    pos = pos_ref[...]
    col = lax.broadcasted_iota(I32, (tm, TILE_ROWS), 1).astype(F32)
    pick = jnp.where(jnp.logical_or(col == pos[:, 0:1], col == pos[:, 1:2]), 1.0, 0.0).astype(BF16)
    lo, hi = _unpack_bf16_pairs(local[slot])
    moe = jnp.concatenate([jnp.dot(pick, lo, preferred_element_type=F32),
                           jnp.dot(pick, hi, preferred_element_type=F32)], axis=1)
    return x1_ref[...] + moe


N_PROJ_IN = 12


def _edge_kernel(*refs, with_combine, with_proj, tiles_per_seq):
    refs = list(refs)
    take = lambda n: [refs.pop(0) for _ in range(n)]
    if with_combine:
        ntot_ref, dst_ref = take(2)
    (x_ref,) = take(1)
    if with_combine:
        pos_ref, rows_ref = take(2)
    if with_proj:
        g_ref, w_ref, qg_ref, kg_ref, gm_ref, cw_ref, cb_ref, wa_ref, ba_ref, wx_ref, bx_ref, lam_ref = take(N_PROJ_IN)
    if with_combine:
        (x2_ref,) = take(1)
    if with_proj:
        q_ref, k_ref, v_ref, rnn_ref = take(4)
    if with_combine:
        local, pending, sems = take(3)
    if with_proj:
        xbuf, a_scr, b_scr, h_scr, g_scr = take(5)

    if with_proj:
        @pl.when(pl.program_id(0) % tiles_per_seq == 0)
        def _():
            xbuf[0:8, :] = jnp.zeros((8, D_RNN), F32)
            h_scr[...] = jnp.zeros_like(h_scr)

    if with_combine:
        x = _combine_body(ntot_ref, dst_ref, x_ref, pos_ref, rows_ref, local, pending, sems)
        x2_ref[...] = x
    else:
        x = x_ref[...]
    if with_proj:
        hb = _rms(x, g_ref[...]).astype(BF16)
        _rglru_gates(_proj_section(hb, w_ref, 3), _proj_section(hb, w_ref, 4), cw_ref, cb_ref, wa_ref, ba_ref,
                     wx_ref, bx_ref, lam_ref, xbuf, a_scr, b_scr, g_scr)
        _proj_qkv(hb, w_ref, qg_ref, kg_ref, gm_ref, q_ref, k_ref, v_ref)
        _rglru_scan(rnn_ref, a_scr, b_scr, h_scr, g_scr)


def _edge(x_in, seq_len, combine_in=None, proj_in=None, layer=None):
    m = x_in.shape[0]
    tm = TOKEN_TILE
    with_combine, with_proj = combine_in is not None, proj_in is not None
    tile = lambda i, *_: (i, 0)
    fix = lambda i, *_: (0, 0)
    operands, in_specs, out_specs, out_shape, scratch = [], [], [], [], []
    n_prefetch = 0
    if with_combine:
        chunk_meta, pos, out_rows = combine_in
        operands += list(chunk_meta)
        n_prefetch = len(chunk_meta)
    operands.append(x_in)
    in_specs.append(pl.BlockSpec((tm, D_MODEL), tile))
    if with_combine:
        operands += [pos, out_rows]
        in_specs += [pl.BlockSpec((tm, LANES), tile), pl.BlockSpec(memory_space=pl.ANY)]
        out_specs.append(pl.BlockSpec((tm, D_MODEL), tile))
        out_shape.append(jax.ShapeDtypeStruct((m, D_MODEL), F32))
        scratch += [pltpu.VMEM((2, TILE_ROWS, D_MODEL // 2), U32), pltpu.SMEM((2,), I32),
                    pltpu.SemaphoreType.DMA((2,))]
    if with_proj:
        assert len(proj_in) == N_PROJ_IN
        operands += list(proj_in)
        in_specs += [_layer_spec(a, layer) for a in proj_in]
        slab = pl.BlockSpec((HEAD_PAIRS, tm, LANES), lambda i, *_: (0, i, 0))
        out_specs += [slab, slab, slab, pl.BlockSpec((tm, D_RNN), tile)]
        out_shape += [jax.ShapeDtypeStruct((HEAD_PAIRS, m, LANES), F32)] * 3 + [jax.ShapeDtypeStruct((m, D_RNN), F32)]
        scratch += [pltpu.VMEM((tm + 8, D_RNN), F32), pltpu.VMEM((tm, D_RNN), F32), pltpu.VMEM((tm, D_RNN), F32),
                    pltpu.VMEM((8, D_RNN), F32), pltpu.VMEM((tm, D_RNN), F32)]
    return pl.pallas_call(
        functools.partial(_edge_kernel, with_combine=with_combine, with_proj=with_proj,
                          tiles_per_seq=seq_len // tm),
        grid_spec=pltpu.PrefetchScalarGridSpec(
            num_scalar_prefetch=n_prefetch, grid=(m // tm,),
            in_specs=in_specs, out_specs=out_specs, scratch_shapes=scratch),
        out_shape=out_shape,
        compiler_params=_cparams("arbitrary"),
    )(*operands)


def _dispatch_plan(cnt_tiles, n_tiles, n_blocks):
    cnt = cnt_tiles.reshape(n_tiles, 8, LANES)[:, 0, ROUTER_LANE0:ROUTER_LANE0 + N_EXPERTS]
    seg = (cnt + ROW_CHUNK - 1) // ROW_CHUNK * ROW_CHUNK
    used = jnp.sum(seg, axis=0)
    padded = (used + EXPERT_BLOCK - 1) // EXPERT_BLOCK * EXPERT_BLOCK
    p_ends = jnp.cumsum(padded)
    p_starts = p_ends - padded
    seg_dst = p_starts[None, :] + jnp.cumsum(seg, axis=0) - seg
    loc_start = jnp.cumsum(seg, axis=1) - seg
    flat = lambda a: a.reshape(-1).astype(I32)
    block_row = jnp.arange(n_blocks, dtype=I32) * EXPERT_BLOCK
    block_expert = jnp.minimum(jnp.sum((p_ends[None, :] <= block_row[:, None]).astype(I32), axis=1), N_EXPERTS - 1)
    n_used = (p_ends[-1:] // EXPERT_BLOCK).astype(I32)
    after = (p_ends // EXPERT_BLOCK).astype(I32)[block_expert]
    next_expert = jnp.where(after < n_used[0], block_expert[jnp.minimum(after, n_blocks - 1)], -1).astype(I32)
    ends = jnp.cumsum(seg // ROW_CHUNK, axis=1)
    j = jnp.arange(TILE_CHUNKS, dtype=I32)
    owner = jnp.minimum(jnp.sum((ends[:, None, :] <= j[None, :, None]).astype(I32), axis=2), N_EXPERTS - 1)
    is_owner = owner[:, :, None] == jnp.arange(N_EXPERTS, dtype=I32)[None, None, :]
    chunk_dst = jnp.sum(jnp.where(is_owner, (seg_dst - loc_start)[:, None, :], 0), axis=2) + ROW_CHUNK * j[None, :]
    chunk_meta = (flat(ends[:, -1]), flat(chunk_dst))
    rest = n_blocks * EXPERT_BLOCK - p_ends[-1:]
    tail_meta = (flat(jnp.concatenate([padded - used, rest]) // ROW_CHUNK),
                 flat(jnp.concatenate([p_starts + used, p_ends[-1:]])))
    occupied_end = jnp.sum(jnp.where(block_expert[:, None] == jnp.arange(N_EXPERTS, dtype=I32)[None, :],
                                     (p_starts + used)[None, :], 0), axis=1)
    valid = jnp.clip(occupied_end - block_row, 0, EXPERT_BLOCK).astype(I32)
    return chunk_meta, tail_meta, (block_expert.astype(I32), n_used, next_expert, valid)


def _constants():
    gm = np.kron(np.eye(MXU_DEPTH // HEAD_DIM), np.full((HEAD_DIM, HEAD_DIM), 1.0 / HEAD_DIM))
    tri = np.tril(np.ones((TOKEN_TILE, TOKEN_TILE), np.float32), -1)
    upper = np.triu(np.ones((LANES, LANES), np.float32), 1)
    return jnp.asarray(gm, BF16), jnp.asarray(tri, BF16), jnp.asarray(upper, BF16)


def kernel(x, rel_bias_table, norm_mix, w_in, q_norm, k_norm, conv_w, conv_b, rg_w_a, rg_b_a, rg_w_x, rg_b_x,
           rg_lambda, norm_attn_out, norm_rnn_out, w_out, norm_ffn, router_group_w, router_group_b,
           router_expert_w, router_expert_b, expert_w_gate, expert_w_up, expert_w_down):
    b, s, d = x.shape
    depth = w_in.shape[0]
    m = b * s
    assert d == D_MODEL and s % ATTN_TILE == 0 and s % TOKEN_TILE == 0

    gm, tri, upper = _constants()
    bias = _bias_tables(rel_bias_table)
    scale = HEAD_DIM ** -0.5
    n_tiles = m // TOKEN_TILE
    n_blocks = -(-(m * TOP_K + n_tiles * N_EXPERTS * (ROW_CHUNK - 1)) // EXPERT_BLOCK) + N_EXPERTS
    n_rows = n_blocks * EXPERT_BLOCK

    vec = lambda v: v.reshape(depth, 1, -1).astype(F32)
    per_mxu = MXU_DEPTH // (D_RNN // RNN_BLOCKS)
    proj_params = (
        vec(norm_mix), w_in.astype(BF16),
        vec(jnp.tile(q_norm, (1, ATTN_HEADS)) * (scale * LOG2E)), vec(jnp.tile(k_norm, (1, ATTN_HEADS))),
        jnp.broadcast_to(gm, (depth,) + gm.shape), conv_w.astype(F32), vec(conv_b),
        _block_diag(rg_w_a, per_mxu).astype(BF16), vec(rg_b_a),
        _block_diag(rg_w_x, per_mxu).astype(BF16), vec(rg_b_x), vec(rg_lambda))
    wr = jnp.zeros((depth, D_MODEL, LANES), F32)
    wr = wr.at[:, :, :N_GROUPS].set(router_group_w).at[:, :, N_GROUPS:N_GROUPS + N_EXPERTS].set(router_expert_w)
    br = jnp.zeros((depth, 1, LANES), F32)
    br = br.at[:, 0, :N_GROUPS].set(router_group_b).at[:, 0, N_GROUPS:N_GROUPS + N_EXPERTS].set(router_expert_b)
    mix_params = (vec(norm_attn_out), vec(norm_rnn_out), w_out.astype(BF16), vec(norm_ffn),
                  jnp.concatenate(_split_bf16(wr), axis=2), br)

    x2 = x.reshape(m, d).astype(F32)
    q, k, v, rnn = _edge(x2, s, proj_in=proj_params, layer=0)
    for l in range(depth):
        attn = _attention(q, k, v, bias, b, s)
        x1, h2, pos, gates, cnt = _mix(x2, attn, rnn, mix_params, l, tri, upper)

        chunk_meta, tail_meta, block_meta = _dispatch_plan(cnt, n_tiles, n_blocks)
        rows = _dispatch(chunk_meta + tail_meta, h2, pos, gates, n_rows)
        out_rows = _experts(*block_meta, rows, expert_w_gate, expert_w_up, expert_w_down, l)
        nxt = proj_params if l + 1 < depth else None
        x2, *started = _edge(x1, s, combine_in=(chunk_meta, pos, out_rows), proj_in=nxt, layer=l + 1)
        if started:
            q, k, v, rnn = started
    return x2.reshape(b, s, d).astype(x.dtype)
```

```python
import functools
import math

import numpy as np
import jax
import jax.numpy as jnp
from jax import lax
from jax.experimental import pallas as pl
from jax.experimental.pallas import tpu as pltpu

F32 = jnp.float32
BF16 = jnp.bfloat16
I32 = jnp.int32
U32 = jnp.uint32

D_MODEL = 1024
ATTN_HEADS = 8
HEAD_DIM = 64
D_ATTN = ATTN_HEADS * HEAD_DIM
RNN_BLOCKS = 8
D_RNN = 512
D_MIX = D_ATTN + D_RNN
D_IN = 3 * D_ATTN + 2 * D_RNN
DILATED_BRANCHES = ((128, 1), (512, 4), (2048, 16))
Q_BLOCK = 128
REL_BUCKETS = 32
REL_MAX_DIST = 2048
CONV_WIDTH = 4
RG_C = 8.0
N_GROUPS = 4
EXPERTS_PER_GROUP = 8
N_EXPERTS = N_GROUPS * EXPERTS_PER_GROUP
TOP_K = 2
D_EXPERT = 512
EPS = 1e-6
NEG_INF = -1e30
LOG2E = math.log2(math.e)

LANES = 128
SUBLANES = 8
MXU_DEPTH = 256
HEAD_PAIRS = D_ATTN // LANES
ROUTER_LANE0 = N_GROUPS
TOKEN_TILE = 512
ATTN_TILE = 2048
ATTN_PAIRS = 2
ATTN_UNROLL = 16
RNN_CHUNK = 64
EXPERT_SUBBLOCK = 128
EXPERT_BLOCK = 512
ROW_CHUNK = 8
TILE_ROWS = -(-(TOKEN_TILE * TOP_K + N_EXPERTS * (ROW_CHUNK - 1)) // 256) * 256
TILE_CHUNKS = TILE_ROWS // ROW_CHUNK
WAIT_BATCH = 16
ROW_WORDS = D_MODEL // 2 + LANES
MERGE_ROWS = 256
VMEM_LIMIT = 48 * 1024 * 1024
ATTN_VMEM_LIMIT = 56 * 1024 * 1024


def _cparams(*sem):
    return pltpu.CompilerParams(dimension_semantics=sem, vmem_limit_bytes=VMEM_LIMIT)


def _rms(x, gain):
    return x * lax.rsqrt(jnp.mean(x * x, axis=-1, keepdims=True) + EPS) * gain


def _rows(start, size, stride):
    return pl.ds(start, size) if stride == 1 else pl.ds(start, size, stride=stride)


def _attn_kernel(q_ref, k_ref, v_ref, bias_ref, out_ref, qd, stage, o_scr, l_scr, *kv_scr):
    last = len(DILATED_BRANCHES) - 1
    for p in range(ATTN_PAIRS):
        for n, (_, dil) in enumerate(DILATED_BRANCHES):
            o_dst = out_ref.at[p] if n == last else o_scr.at[n, p]
            _attn_pair(q_ref.at[p], k_ref.at[p], v_ref.at[p], bias_ref.at[n], 2 * p, o_dst, l_scr.at[n, p],
                       qd, kv_scr[2 * n].at[p], kv_scr[2 * n + 1].at[p], stage, dil)

        def merge(c, carry):
            rows = pl.ds(pl.multiple_of(c * MERGE_ROWS, MERGE_ROWS), MERGE_ROWS)
            lse = [l_scr[n, p, rows, :] for n in range(last + 1)]
            mx = functools.reduce(jnp.maximum, lse)
            w = [jnp.exp2(l - mx) for l in lse]
            o = [o_scr[n, p, rows, :] for n in range(last)] + [out_ref[p, rows, :]]
            num = functools.reduce(jnp.add, [wi * oi for wi, oi in zip(w, o)])
            out_ref[p, rows, :] = num * (1.0 / functools.reduce(jnp.add, w))
            return carry

        lax.fori_loop(0, out_ref.shape[1] // MERGE_ROWS, merge, 0)


def _attn_pair(q_ref, k_ref, v_ref, bias_ref, head0, o_ref, lse_ref, qd, kd, vd, stage, dil):
    i = pl.program_id(2)
    tile = q_ref.shape[0]
    rows = tile // dil
    seg = Q_BLOCK + rows

    @pl.when(i == 0)
    def _():
        for r in range(dil):
            kd[r * seg:r * seg + Q_BLOCK, :] = jnp.zeros((Q_BLOCK, LANES), BF16)
            for hh in range(2):
                vd[hh, r * seg:r * seg + Q_BLOCK, :] = jnp.zeros((Q_BLOCK, LANES), BF16)

    @pl.when(i > 0)
    def _():
        for r in range(dil):
            kd[r * seg:r * seg + Q_BLOCK, :] = kd[r * seg + rows:(r + 1) * seg, :]
            for hh in range(2):
                vd[hh, r * seg:r * seg + Q_BLOCK, :] = vd[hh, r * seg + rows:(r + 1) * seg, :]

    def regroup(src_ref, put):
        if dil % 16 == 0:
            quarter = tile // 4
            for c in range(4):
                stage[0, c * quarter:(c + 1) * quarter, :] = src_ref[pl.ds(c, quarter, stride=4), :]
            for r in range(dil):
                put(r, stage[0, _rows((r % 4) * quarter + r // 4, rows, dil // 4), :].astype(BF16))
        else:
            for r in range(dil):
                put(r, src_ref[_rows(r, rows, dil), :].astype(BF16))

    def put_q(r, x):
        qd[r * rows:(r + 1) * rows, :] = x

    def put_k(r, x):
        kd[r * seg + Q_BLOCK:(r + 1) * seg, :] = x

    first_head = lax.broadcasted_iota(I32, (rows, LANES), 1) < HEAD_DIM

    def put_v(r, x):
        vd[0, r * seg + Q_BLOCK:(r + 1) * seg, :] = jnp.where(first_head, x, jnp.zeros_like(x))
        vd[1, r * seg + Q_BLOCK:(r + 1) * seg, :] = jnp.where(first_head, jnp.zeros_like(x), x)

    regroup(q_ref, put_q)
    regroup(k_ref, put_k)
    regroup(v_ref, put_v)

    per = rows // Q_BLOCK
    low = lax.broadcasted_iota(I32, (Q_BLOCK, LANES), 1) < HEAD_DIM
    key_lane = lax.broadcasted_iota(I32, (2 * Q_BLOCK, LANES), 1)
    ones = [jnp.where(key_lane < HEAD_DIM, 1.0, 0.0).astype(BF16),
            jnp.where(key_lane < HEAD_DIM, 0.0, 1.0).astype(BF16)]

    def block(blk, carry):
        r = blk // per
        n = blk % per
        q0 = pl.multiple_of(r * rows + n * Q_BLOCK, Q_BLOCK)
        k0 = pl.multiple_of(r * seg + n * Q_BLOCK, Q_BLOCK)
        sel = jnp.where(jnp.logical_and(i == 0, n == 0), 1, 0)
        q = qd[pl.ds(q0, Q_BLOCK), :]
        kk = kd[pl.ds(k0, 2 * Q_BLOCK), :]
        probs, maxes = [], []
        for hh in range(2):
            qm = jnp.where(low if hh == 0 else jnp.logical_not(low), q, jnp.zeros_like(q))
            s = lax.dot_general(qm, kk, (((1,), (1,)), ((), ())), preferred_element_type=F32)
            s = s + bias_ref[sel, head0 + hh]
            mx = jnp.max(s, axis=-1, keepdims=True)
            probs.append(jnp.exp2(s - mx).astype(BF16))
            maxes.append(mx)
        rhs = jnp.concatenate([jnp.concatenate([vd[hh, pl.ds(k0, 2 * Q_BLOCK), :], ones[hh]], axis=1)
                               for hh in range(2)], axis=0)
        both = jnp.dot(jnp.concatenate(probs, axis=1), rhs, preferred_element_type=F32)
        l = both[:, LANES:]
        o_tile = both[:, :LANES] * (1.0 / l)
        lse_tile = jnp.where(low, maxes[0], maxes[1]) + jnp.log(l) * LOG2E
        if dil % 16 == 0:
            dst = _rows((r % 4) * (tile // 4) + r // 4 + (dil // 4) * Q_BLOCK * n, Q_BLOCK, dil // 4)
            stage[1, dst, :] = o_tile
            stage[2, dst, :] = lse_tile
        else:
            dst = _rows(r + dil * Q_BLOCK * n, Q_BLOCK, dil)
            o_ref[dst, :] = o_tile
            lse_ref[dst, :] = lse_tile
        return carry

    lax.fori_loop(0, dil * per, block, 0, unroll=ATTN_UNROLL)
    if dil % 16 == 0:
        quarter = tile // 4
        for c in range(4):
            o_ref[pl.ds(c, quarter, stride=4), :] = stage[1, c * quarter:(c + 1) * quarter, :]
            lse_ref[pl.ds(c, quarter, stride=4), :] = stage[2, c * quarter:(c + 1) * quarter, :]


def _attention(q, k, v, bias, b, s):
    tile = ATTN_TILE
    nt = s // tile
    nb = len(DILATED_BRANCHES)
    blk = pl.BlockSpec((ATTN_PAIRS, tile, LANES), lambda bb, g, i: (g, bb * nt + i, 0))
    kv_scr = []
    for _, dil in DILATED_BRANCHES:
        keys = tile + Q_BLOCK * dil
        kv_scr += [pltpu.VMEM((ATTN_PAIRS, keys, LANES), BF16), pltpu.VMEM((ATTN_PAIRS, 2, keys, LANES), BF16)]
    return pl.pallas_call(
        _attn_kernel,
        grid=(b, HEAD_PAIRS // ATTN_PAIRS, nt),
        in_specs=[blk, blk, blk,
                  pl.BlockSpec((nb, 2, 2 * ATTN_PAIRS, Q_BLOCK, 2 * Q_BLOCK), lambda bb, g, i: (0, 0, g, 0, 0))],
        out_specs=blk,
        out_shape=jax.ShapeDtypeStruct(q.shape, F32),
        scratch_shapes=[pltpu.VMEM((tile, LANES), BF16), pltpu.VMEM((3, tile, LANES), F32),
                        pltpu.VMEM((nb - 1, ATTN_PAIRS, tile, LANES), F32),
                        pltpu.VMEM((nb, ATTN_PAIRS, tile, LANES), F32)] + kv_scr,
        compiler_params=pltpu.CompilerParams(dimension_semantics=("parallel", "parallel", "arbitrary"),
                                             vmem_limit_bytes=ATTN_VMEM_LIMIT),
    )(q, k, v, bias)


def _bias_step_tables():
    exact = REL_BUCKETS // 2
    i = np.arange(Q_BLOCK)[:, None]
    j = np.arange(2 * Q_BLOCK)[None, :]
    steps = i + Q_BLOCK - j
    onehot, band = [], []
    for window, dil in DILATED_BRANCHES:
        dist = (np.arange(2 * Q_BLOCK) * dil).astype(np.int32)
        d = np.maximum(dist, 1).astype(np.float32)
        log_b = exact + (np.log(d / np.float32(exact)) / np.float32(math.log(REL_MAX_DIST / exact))
                         * np.float32(REL_BUCKETS - exact)).astype(np.int32)
        bucket = np.where(dist < exact, dist, np.minimum(log_b, REL_BUCKETS - 1))
        onehot.append(np.eye(REL_BUCKETS, dtype=np.float32)[bucket])
        band.append((steps >= 0) & (steps <= window // dil))
    first = np.broadcast_to(j >= Q_BLOCK, (Q_BLOCK, 2 * Q_BLOCK))
    return np.stack(onehot), np.stack(band), first


def _bias_tables(rel_table):
    onehot, band, first = _bias_step_tables()
    nb = len(DILATED_BRANCHES)
    vec = LOG2E * jnp.einsum('nsb,bh->nhs', onehot, rel_table.astype(F32), precision=lax.Precision.HIGHEST)
    period = 3 * Q_BLOCK
    padded = jnp.concatenate([vec, jnp.zeros((nb, ATTN_HEADS, Q_BLOCK), F32)], axis=-1)
    wrapped = jnp.roll(padded[..., ::-1], Q_BLOCK + 1, axis=-1)
    bias = jnp.tile(wrapped, (1, 1, Q_BLOCK))[..., :Q_BLOCK * (period - 1)].reshape(
        nb, ATTN_HEADS, Q_BLOCK, period - 1)[..., :2 * Q_BLOCK]
    regular = jnp.where(band[:, None], bias, NEG_INF)
    start = jnp.where((band & first[None])[:, None], bias, NEG_INF)
    return jnp.stack([regular, start], axis=1)


def _block_diag(w, group):
    *lead, nb, n, _ = w.shape
    w = w.reshape(*lead, nb // group, group, n, n)
    eye = jnp.eye(group, dtype=w.dtype)
    return (eye[:, None, :, None] * w[..., :, :, None, :]).reshape(*lead, nb // group, group * n, group * n)


def _split_bf16(x):
    hi = x.astype(BF16)
    return hi, (x - hi.astype(F32)).astype(BF16)


def _mix_kernel(x_ref, attn_ref, rnn_ref, ga_ref, gn_ref, wo_ref, gf_ref,
                wr_ref, br_ref, tri_ref, upper_ref,
                x1_ref, h2_ref, pos_ref, gate_ref, cnt_ref):
    slabs = [attn_ref[hp] for hp in range(HEAD_PAIRS)]
    sumsq = functools.reduce(jnp.add, [jnp.sum(a * a, axis=-1, keepdims=True) for a in slabs])
    scale = lax.rsqrt(sumsq * (1.0 / D_ATTN) + EPS)
    na = (jnp.concatenate([a * scale for a in slabs], axis=1) * ga_ref[...]).astype(BF16)
    nr = _rms(rnn_ref[...], gn_ref[...]).astype(BF16)
    x1 = (x_ref[...] + jnp.dot(na, wo_ref[0:D_ATTN, :], preferred_element_type=F32)
          + jnp.dot(nr, wo_ref[D_ATTN:, :], preferred_element_type=F32))
    x1_ref[...] = x1
    h2 = _rms(x1, gf_ref[...])
    h2_ref[...] = h2.astype(BF16)

    hh, hl = _split_bf16(h2)
    by_hi = jnp.dot(hh, wr_ref[...], preferred_element_type=F32)
    by_lo = jnp.dot(hl, wr_ref[...], preferred_element_type=F32)
    logits = by_hi[:, :LANES] + by_hi[:, LANES:] + by_lo[:, :LANES] + br_ref[...]
    tm = logits.shape[0]
    lane = lax.broadcasted_iota(I32, (tm, LANES), 1)
    lanef = lane.astype(F32)
    big = float(LANES)

    def top(vals):
        m = jnp.max(vals, axis=-1, keepdims=True)
        return m, jnp.min(jnp.where(vals == m, lanef, big), axis=-1, keepdims=True)

    is_group = lane < N_GROUPS
    gmax, gsel = top(jnp.where(is_group, logits, NEG_INF))
    g_w = 1.0 / jnp.sum(jnp.where(is_group, jnp.exp(logits - gmax), 0.0), axis=-1, keepdims=True)
    lo_lane = ROUTER_LANE0 + EXPERTS_PER_GROUP * gsel
    in_group = jnp.logical_and(lanef >= lo_lane, lanef < lo_lane + EXPERTS_PER_GROUP)
    el = jnp.where(in_group, logits, NEG_INF)
    v1, i1 = top(el)
    v2, i2 = top(jnp.where(lanef == i1, NEG_INF, el))
    t = jnp.exp(v2 - v1)
    p1 = 1.0 / (1.0 + t)
    gate1 = g_w * p1
    gate2 = g_w * (t * p1)

    oh1 = (lanef == i1).astype(F32)
    oh2 = (lanef == i2).astype(F32)
    cnt = oh1 + oh2
    prefix = jnp.dot(tri_ref[...], cnt.astype(BF16), preferred_element_type=F32)
    total = jnp.sum(cnt, axis=0, keepdims=True)
    chunks = jnp.floor((total + (ROW_CHUNK - 1)) * (1.0 / ROW_CHUNK))
    seg_start = ROW_CHUNK * jnp.dot(jnp.broadcast_to(chunks, (8, LANES)).astype(BF16), upper_ref[...],
                                    preferred_element_type=F32)[0:1, :]
    base = seg_start + prefix
    pos1 = jnp.sum(oh1 * base, axis=-1, keepdims=True)
    pos2 = jnp.sum(oh2 * base, axis=-1, keepdims=True)
    cnt_ref[...] = jnp.broadcast_to(total, cnt_ref.shape).astype(I32)
    pos_ref[...] = jnp.where(lane == 0, pos1, jnp.where(lane == 1, pos2, 0.0))
    gate_ref[...] = jnp.where(lane == 0, gate1, jnp.where(lane == 1, gate2, 0.0))


def _layer_spec(a, layer):
    return pl.BlockSpec((None,) + a.shape[1:], lambda i, *_: (layer,) + (0,) * (a.ndim - 1))


def _mix(x2, attn, rnn, layer_params, layer, tri, upper):
    m = x2.shape[0]
    tm = TOKEN_TILE
    row = lambda i: (i, 0)
    fix = lambda i: (0, 0)
    t512 = pl.BlockSpec((tm, D_ATTN), row)
    t128 = pl.BlockSpec((tm, LANES), row)
    t1024 = pl.BlockSpec((tm, D_MODEL), row)
    slab = pl.BlockSpec((HEAD_PAIRS, tm, LANES), lambda i: (0, i, 0))
    return pl.pallas_call(
        _mix_kernel,
        grid=(m // tm,),
        in_specs=[t1024, slab, t512] + [_layer_spec(a, layer) for a in layer_params]
        + [pl.BlockSpec((tm, tm), fix), pl.BlockSpec((LANES, LANES), fix)],
        out_specs=[t1024, t1024, t128, t128, pl.BlockSpec((8, LANES), row)],
        out_shape=[jax.ShapeDtypeStruct((m, D_MODEL), F32), jax.ShapeDtypeStruct((m, D_MODEL), BF16),
                   jax.ShapeDtypeStruct((m, LANES), F32), jax.ShapeDtypeStruct((m, LANES), F32),
                   jax.ShapeDtypeStruct((m // tm * 8, LANES), I32)],
        compiler_params=_cparams("parallel"),
    )(x2, attn, rnn, *layer_params, tri, upper)


def _rows_copy(src, src_row, dst, dst_row, sem, rows=ROW_CHUNK):
    return pltpu.make_async_copy(src.at[pl.ds(src_row, rows)], dst.at[pl.ds(dst_row, rows)], sem)


def _for_each_chunk(i, ntot_ref, dst_ref, copy):
    n = ntot_ref[i]

    def body(j, carry):
        copy(pl.multiple_of(j * ROW_CHUNK, ROW_CHUNK), pl.multiple_of(dst_ref[i * TILE_CHUNKS + j], ROW_CHUNK))
        return carry

    lax.fori_loop(0, n, body, 0)
    return n


def _wait_chunks(count, wait_rows):
    def batch(c, carry):
        wait_rows(WAIT_BATCH * ROW_CHUNK)
        return carry

    def single(c, carry):
        wait_rows(ROW_CHUNK)
        return carry

    lax.fori_loop(0, count // WAIT_BATCH, batch, 0)
    lax.fori_loop(0, count % WAIT_BATCH, single, 0)


def _pack_bf16_pairs(x):
    c = x.shape[1] // 2
    lo = lax.bitcast_convert_type(x[:, :c], U32) >> 16
    hi = lax.bitcast_convert_type(x[:, c:], U32) & jnp.uint32(0xFFFF0000)
    return hi | lo


def _unpack_bf16_pairs(w):
    lo = lax.bitcast_convert_type(w << 16, F32).astype(BF16)
    hi = lax.bitcast_convert_type(w & jnp.uint32(0xFFFF0000), F32).astype(BF16)
    return lo, hi


def _dispatch_kernel(ntot_ref, dst_ref, tailn_ref, taildst_ref,
                     h_ref, pos_ref, gate_ref, rows_ref, sorted_buf, zero_buf, pending, sems, *, tm):
    i = pl.program_id(0)
    slot = i % 2
    pos_t = pos_ref[...].T
    gate_t = gate_ref[...].T
    row = lax.broadcasted_iota(I32, (TILE_ROWS, tm), 0)
    hit1 = row == pos_t[0:1, :].astype(I32)
    hit2 = row == pos_t[1:2, :].astype(I32)
    onehot = jnp.where(jnp.logical_or(hit1, hit2), 1.0, 0.0).astype(BF16)
    feat = jnp.dot(onehot, h_ref[...], preferred_element_type=F32)
    sorted_buf[slot, :, 0:ROW_WORDS - LANES] = _pack_bf16_pairs(feat)
    gate = jnp.sum(jnp.where(hit1, gate_t[0:1, :], jnp.where(hit2, gate_t[1:2, :], 0.0)), axis=-1, keepdims=True)
    lane = lax.broadcasted_iota(I32, (TILE_ROWS, LANES), 1)
    sorted_buf[slot, :, ROW_WORDS - LANES:] = jnp.where(lane == 0, lax.bitcast_convert_type(gate, U32), jnp.uint32(0))

    def wait_rows(sem):
        return lambda r: _rows_copy(sorted_buf.at[0], 0, rows_ref, 0, sem, r).wait()

    @pl.when(i > 0)
    def _():
        _wait_chunks(pending[0], wait_rows(sems.at[1 - slot]))

    def send(loc, dst):
        _rows_copy(sorted_buf.at[slot], loc, rows_ref, dst, sems.at[slot]).start()

    n_sent = _for_each_chunk(i, ntot_ref, dst_ref, send)

    @pl.when(i == 0)
    def _():
        zero_buf[...] = jnp.zeros_like(zero_buf)

    def tails():
        def per_expert(e, total):
            n = tailn_ref[e]
            dst0 = taildst_ref[e]
            big = n // WAIT_BATCH

            def batch(c, carry):
                _rows_copy(zero_buf, 0, rows_ref, pl.multiple_of(dst0 + c * (WAIT_BATCH * ROW_CHUNK), ROW_CHUNK),
                           sems.at[slot], WAIT_BATCH * ROW_CHUNK).start()
                return carry

            def single(c, carry):
                _rows_copy(zero_buf, 0, rows_ref, pl.multiple_of(dst0 + c * ROW_CHUNK, ROW_CHUNK),
                           sems.at[slot]).start()
                return carry

            lax.fori_loop(0, big, batch, 0)
            lax.fori_loop(big * WAIT_BATCH, n, single, 0)
            return total + n

        return lax.fori_loop(0, N_EXPERTS + 1, per_expert, jnp.int32(0))

    n_sent = n_sent + lax.cond(i == 0, tails, lambda: jnp.int32(0))
    pending[0] = n_sent

    @pl.when(i == pl.num_programs(0) - 1)
    def _():
        _wait_chunks(n_sent, wait_rows(sems.at[slot]))


def _dispatch(meta, h2, pos, gates, n_rows):
    m = h2.shape[0]
    tm = TOKEN_TILE
    tile = lambda i, *_: (i, 0)
    return pl.pallas_call(
        functools.partial(_dispatch_kernel, tm=tm),
        grid_spec=pltpu.PrefetchScalarGridSpec(
            num_scalar_prefetch=4,
            grid=(m // tm,),
            in_specs=[pl.BlockSpec((tm, D_MODEL), tile), pl.BlockSpec((tm, LANES), tile),
                      pl.BlockSpec((tm, LANES), tile)],
            out_specs=pl.BlockSpec(memory_space=pl.ANY),
            scratch_shapes=[pltpu.VMEM((2, TILE_ROWS, ROW_WORDS), U32),
                            pltpu.VMEM((WAIT_BATCH * ROW_CHUNK, ROW_WORDS), U32),
                            pltpu.SMEM((1,), I32), pltpu.SemaphoreType.DMA((2,))],
        ),
        out_shape=jax.ShapeDtypeStruct((n_rows, ROW_WORDS), U32),
        compiler_params=_cparams("arbitrary"),
    )(*meta, h2, pos, gates)


def _expert_kernel(be_ref, nused_ref, next_ref, valid_ref, rows_ref, wg_hbm, wu_hbm, wd_hbm, out_ref,
                   wgf, wuf, wdf, wgb, wub, wdb, holder, sems, *, layer):
    i = pl.program_id(0)
    used = i < nused_ref[0]
    expert = be_ref[i]
    first = jnp.logical_or(i == 0, expert != be_ref[jnp.maximum(i - 1, 0)])

    def weight_copies(e, slot):
        return (pltpu.make_async_copy(wg_hbm.at[layer, e], wgf.at[slot], sems.at[slot, 0]),
                pltpu.make_async_copy(wu_hbm.at[layer, e], wuf.at[slot], sems.at[slot, 1]),
                pltpu.make_async_copy(wd_hbm.at[layer, e], wdf.at[slot], sems.at[slot, 2]))

    @pl.when(jnp.logical_and(used, first))
    def _():
        @pl.when(i == 0)
        def _():
            holder[0] = 0
            for c in weight_copies(expert, 0):
                c.start()

        slot = holder[0]
        for c in weight_copies(expert, slot):
            c.wait()
        wgb[...] = wgf[slot].astype(BF16)
        wub[...] = wuf[slot].astype(BF16)
        wdb[...] = wdf[slot].astype(BF16)
        nxt = next_ref[i]

        @pl.when(nxt >= 0)
        def _():
            for c in weight_copies(nxt, 1 - slot):
                c.start()

        holder[0] = 1 - slot

    def mlp(r0, nrows):
        rows = rows_ref[r0:r0 + nrows, :]
        xb = jnp.concatenate(_unpack_bf16_pairs(rows[:, 0:ROW_WORDS - LANES]), axis=1)
        gate = lax.bitcast_convert_type(rows[:, ROW_WORDS - LANES:][:, 0:1], F32)
        g = jnp.dot(xb, wgb[...], preferred_element_type=F32)
        u = jnp.dot(xb, wub[...], preferred_element_type=F32)
        act = (g * jax.nn.sigmoid(g) * u).astype(BF16)
        y = jnp.dot(act, wdb[...], preferred_element_type=F32) * gate
        out_ref[r0:r0 + nrows, :] = _pack_bf16_pairs(y.astype(BF16).astype(F32))

    blk = rows_ref.shape[0]
    valid = valid_ref[i]

    fused = valid > blk // 2

    @pl.when(fused)
    def _():
        mlp(0, blk)

    @pl.when(jnp.logical_not(fused))
    def _():
        for r0 in range(0, blk, EXPERT_SUBBLOCK):
            @pl.when(r0 < valid)
            def _():
                mlp(r0, EXPERT_SUBBLOCK)

            @pl.when(r0 >= valid)
            def _():
                out_ref[r0:r0 + EXPERT_SUBBLOCK, :] = jnp.zeros((EXPERT_SUBBLOCK, out_ref.shape[1]), out_ref.dtype)


def _experts(block_expert, n_used, next_expert, valid, rows, w_gate, w_up, w_down, layer):
    n_rows = rows.shape[0]
    blk = EXPERT_BLOCK
    rmap = lambda i, be, nu, nx, va: (jnp.minimum(i, nu[0] - 1), 0)
    hbm = pl.BlockSpec(memory_space=pl.ANY)
    return pl.pallas_call(
        functools.partial(_expert_kernel, layer=layer),
        grid_spec=pltpu.PrefetchScalarGridSpec(
            num_scalar_prefetch=4,
            grid=(n_rows // blk,),
            in_specs=[pl.BlockSpec((blk, ROW_WORDS), rmap), hbm, hbm, hbm],
            out_specs=pl.BlockSpec((blk, D_MODEL // 2), lambda i, be, nu, nx, va: (i, 0)),
            scratch_shapes=[pltpu.VMEM((2, D_MODEL, D_EXPERT), F32), pltpu.VMEM((2, D_MODEL, D_EXPERT), F32),
                            pltpu.VMEM((2, D_EXPERT, D_MODEL), F32),
                            pltpu.VMEM((D_MODEL, D_EXPERT), BF16), pltpu.VMEM((D_MODEL, D_EXPERT), BF16),
                            pltpu.VMEM((D_EXPERT, D_MODEL), BF16),
                            pltpu.SMEM((1,), I32), pltpu.SemaphoreType.DMA((2, 3))],
        ),
        out_shape=jax.ShapeDtypeStruct((n_rows, D_MODEL // 2), U32),
        compiler_params=_cparams("arbitrary"),
    )(block_expert, n_used, next_expert, valid, rows, w_gate, w_up, w_down)


def _proj_section(hb, w_ref, n):
    return jnp.dot(hb, w_ref[:, n * D_ATTN:(n + 1) * D_ATTN], preferred_element_type=F32)


def _proj_qkv(hb, w_ref, qg_ref, kg_ref, gm_ref, q_ref, k_ref, v_ref):
    sec = functools.partial(_proj_section, hb, w_ref)

    def head_norm(z, gain):
        zz = (z * z).astype(BF16)
        half = gm_ref.shape[0]
        ms = jnp.concatenate([jnp.dot(zz[:, c:c + half], gm_ref[...], preferred_element_type=F32)
                              for c in range(0, D_ATTN, half)], axis=1)
        return z * lax.rsqrt(ms + EPS) * gain

    def put_slabs(ref, z):
        for hp in range(HEAD_PAIRS):
            ref[hp] = z[:, hp * LANES:(hp + 1) * LANES]

    put_slabs(q_ref, head_norm(sec(0), qg_ref[...]))
    put_slabs(k_ref, head_norm(sec(1), kg_ref[...]))
    put_slabs(v_ref, sec(2))


def _rglru_gates(xr, gr, cw_ref, cb_ref, wa_ref, ba_ref, wx_ref, bx_ref, lam_ref, xbuf, a_scr, b_scr, g_scr):
    tt = xr.shape[0]
    xbuf[8:8 + tt, :] = xr
    g_scr[...] = jax.nn.gelu(gr, approximate=True)
    u = cb_ref[...] + cw_ref[3:4, :] * xr
    for back in range(1, CONV_WIDTH):
        u = u + cw_ref[3 - back:4 - back, :] * xbuf[8 - back:8 - back + tt, :]
    xbuf[0:8, :] = xbuf[tt:tt + 8, :]

    ub = u.astype(BF16)

    def gate(w_ref, b_ref):
        deep = w_ref.shape[1]
        z = jnp.concatenate([jnp.dot(ub[:, c * deep:(c + 1) * deep], w_ref[c], preferred_element_type=F32)
                             for c in range(w_ref.shape[0])], axis=1)
        return jax.nn.sigmoid(z + b_ref[...])

    r = gate(wa_ref, ba_ref)
    gi = gate(wx_ref, bx_ref)
    nl = -lam_ref[...]
    softplus = jnp.maximum(nl, 0.0) + jnp.log1p(jnp.exp(-jnp.abs(nl)))
    log_a = (-RG_C) * r * softplus
    a = jnp.exp(log_a)
    a_scr[...] = a
    b_scr[...] = jnp.sqrt(-jnp.tanh(log_a) * (a * a + 1.0)) * (gi * u)


def _rglru_scan(out_ref, a_scr, b_scr, h_scr, g_scr):
    tt = out_ref.shape[0]
    rc = RNN_CHUNK
    sub = lax.broadcasted_iota(I32, (SUBLANES, LANES), 0)

    def chunk(c, carry):
        r0 = pl.multiple_of(c * rc, rc)
        for g in range(D_RNN // LANES):
            ls = slice(g * LANES, (g + 1) * LANES)
            h = h_scr[0:1, ls]
            for v in range(rc // SUBLANES):
                rows = pl.ds(r0 + v * SUBLANES, SUBLANES)
                aa = a_scr[rows, ls]
                bb = b_scr[rows, ls]
                k = 1
                while k < SUBLANES:
                    keep = sub >= k
                    a_sh = pltpu.roll(aa, k, 0)
                    b_sh = pltpu.roll(bb, k, 0)
                    bb = jnp.where(keep, aa * b_sh + bb, bb)
                    aa = jnp.where(keep, aa * a_sh, aa)
                    k *= 2
                hv = aa * h + bb
                h = hv[SUBLANES - 1:SUBLANES, :]
                out_ref[rows, ls] = hv * g_scr[rows, ls]
            h_scr[0:1, ls] = h
        return carry

    lax.fori_loop(0, tt // rc, chunk, 0)


def _combine_body(ntot_ref, dst_ref, x1_ref, pos_ref, rows_ref, local, pending, sems):
    i = pl.program_id(0)
    slot = i % 2
    tm = x1_ref.shape[0]

    def fetch_tile(t, s):
        def fetch(loc, src):
            _rows_copy(rows_ref, src, local.at[s], loc, sems.at[s]).start()

        return _for_each_chunk(t, ntot_ref, dst_ref, fetch)

    @pl.when(i == 0)
    def _():
        local[...] = jnp.zeros_like(local)
        pending[0] = fetch_tile(0, 0)

    @pl.when(i + 1 < pl.num_programs(0))
    def _():
        pending[1 - slot] = fetch_tile(i + 1, 1 - slot)

    _wait_chunks(pending[slot], lambda r: _rows_copy(rows_ref, 0, local.at[0], 0, sems.at[slot], r).wait())

    pos = pos_ref[...]
    col = lax.broadcasted_iota(I32, (tm, TILE_ROWS), 1)
    slots = pos.astype(I32)
    pick = jnp.where(jnp.logical_or(col == slots[:, 0:1], col == slots[:, 1:2]), 1.0, 0.0).astype(BF16)
    lo, hi = _unpack_bf16_pairs(local[slot])
    moe = jnp.concatenate([jnp.dot(pick, lo, preferred_element_type=F32),
                           jnp.dot(pick, hi, preferred_element_type=F32)], axis=1)
    return x1_ref[...] + moe


N_PROJ_IN = 12


def _edge_kernel(*refs, with_combine, with_proj, tiles_per_seq):
    refs = list(refs)
    take = lambda n: [refs.pop(0) for _ in range(n)]
    if with_combine:
        ntot_ref, dst_ref = take(2)
    (x_ref,) = take(1)
    if with_combine:
        pos_ref, rows_ref = take(2)
    if with_proj:
        g_ref, w_ref, qg_ref, kg_ref, gm_ref, cw_ref, cb_ref, wa_ref, ba_ref, wx_ref, bx_ref, lam_ref = take(N_PROJ_IN)
    if with_combine:
        (x2_ref,) = take(1)
    if with_proj:
        q_ref, k_ref, v_ref, rnn_ref = take(4)
    if with_combine:
        local, pending, sems = take(3)
    if with_proj:
        xbuf, a_scr, b_scr, h_scr, g_scr = take(5)

    if with_proj:
        @pl.when(pl.program_id(0) % tiles_per_seq == 0)
        def _():
            xbuf[0:8, :] = jnp.zeros((8, D_RNN), F32)
            h_scr[...] = jnp.zeros_like(h_scr)

    if with_combine:
        x = _combine_body(ntot_ref, dst_ref, x_ref, pos_ref, rows_ref, local, pending, sems)
        x2_ref[...] = x
    else:
        x = x_ref[...]
    if with_proj:
        hb = _rms(x, g_ref[...]).astype(BF16)
        _rglru_gates(_proj_section(hb, w_ref, 3), _proj_section(hb, w_ref, 4), cw_ref, cb_ref, wa_ref, ba_ref,
                     wx_ref, bx_ref, lam_ref, xbuf, a_scr, b_scr, g_scr)
        _proj_qkv(hb, w_ref, qg_ref, kg_ref, gm_ref, q_ref, k_ref, v_ref)
        _rglru_scan(rnn_ref, a_scr, b_scr, h_scr, g_scr)


def _edge(x_in, seq_len, combine_in=None, proj_in=None, layer=None):
    m = x_in.shape[0]
    tm = TOKEN_TILE
    with_combine, with_proj = combine_in is not None, proj_in is not None
    tile = lambda i, *_: (i, 0)
    operands, in_specs, out_specs, out_shape, scratch = [], [], [], [], []
    n_prefetch = 0
    if with_combine:
        chunk_meta, pos, out_rows = combine_in
        operands += list(chunk_meta)
        n_prefetch = len(chunk_meta)
    operands.append(x_in)
    in_specs.append(pl.BlockSpec((tm, D_MODEL), tile))
    if with_combine:
        operands += [pos, out_rows]
        in_specs += [pl.BlockSpec((tm, LANES), tile), pl.BlockSpec(memory_space=pl.ANY)]
        out_specs.append(pl.BlockSpec((tm, D_MODEL), tile))
        out_shape.append(jax.ShapeDtypeStruct((m, D_MODEL), F32))
        scratch += [pltpu.VMEM((2, TILE_ROWS, D_MODEL // 2), U32), pltpu.SMEM((2,), I32),
                    pltpu.SemaphoreType.DMA((2,))]
    if with_proj:
        assert len(proj_in) == N_PROJ_IN
        operands += list(proj_in)
        in_specs += [_layer_spec(a, layer) for a in proj_in]
        slab = pl.BlockSpec((HEAD_PAIRS, tm, LANES), lambda i, *_: (0, i, 0))
        out_specs += [slab, slab, slab, pl.BlockSpec((tm, D_RNN), tile)]
        out_shape += [jax.ShapeDtypeStruct((HEAD_PAIRS, m, LANES), F32)] * 3 + [jax.ShapeDtypeStruct((m, D_RNN), F32)]
        scratch += [pltpu.VMEM((tm + 8, D_RNN), F32), pltpu.VMEM((tm, D_RNN), F32), pltpu.VMEM((tm, D_RNN), F32),
                    pltpu.VMEM((8, D_RNN), F32), pltpu.VMEM((tm, D_RNN), F32)]
    return pl.pallas_call(
        functools.partial(_edge_kernel, with_combine=with_combine, with_proj=with_proj,
                          tiles_per_seq=seq_len // tm),
        grid_spec=pltpu.PrefetchScalarGridSpec(
            num_scalar_prefetch=n_prefetch, grid=(m // tm,),
            in_specs=in_specs, out_specs=out_specs, scratch_shapes=scratch),
        out_shape=out_shape,
        compiler_params=_cparams("arbitrary"),
    )(*operands)


def _dispatch_plan(cnt_tiles, n_tiles, n_blocks):
    cnt = cnt_tiles.reshape(n_tiles, 8, LANES)[:, 0, ROUTER_LANE0:ROUTER_LANE0 + N_EXPERTS]
    seg = (cnt + ROW_CHUNK - 1) // ROW_CHUNK * ROW_CHUNK
    used = jnp.sum(seg, axis=0)
    padded = (used + EXPERT_BLOCK - 1) // EXPERT_BLOCK * EXPERT_BLOCK
    p_ends = jnp.cumsum(padded)
    p_starts = p_ends - padded
    seg_dst = p_starts[None, :] + jnp.cumsum(seg, axis=0) - seg
    loc_start = jnp.cumsum(seg, axis=1) - seg
    flat = lambda a: a.reshape(-1).astype(I32)
    block_row = jnp.arange(n_blocks, dtype=I32) * EXPERT_BLOCK
    block_expert = jnp.minimum(jnp.sum((p_ends[None, :] <= block_row[:, None]).astype(I32), axis=1), N_EXPERTS - 1)
    n_used = (p_ends[-1:] // EXPERT_BLOCK).astype(I32)
    after = (p_ends // EXPERT_BLOCK).astype(I32)[block_expert]
    next_expert = jnp.where(after < n_used[0], block_expert[jnp.minimum(after, n_blocks - 1)], -1).astype(I32)
    ends = jnp.cumsum(seg // ROW_CHUNK, axis=1)
    j = jnp.arange(TILE_CHUNKS, dtype=I32)
    owner = jnp.minimum(jnp.sum((ends[:, None, :] <= j[None, :, None]).astype(I32), axis=2), N_EXPERTS - 1)
    is_owner = owner[:, :, None] == jnp.arange(N_EXPERTS, dtype=I32)[None, None, :]
    chunk_dst = jnp.sum(jnp.where(is_owner, (seg_dst - loc_start)[:, None, :], 0), axis=2) + ROW_CHUNK * j[None, :]
    chunk_meta = (flat(ends[:, -1]), flat(chunk_dst))
    rest = n_blocks * EXPERT_BLOCK - p_ends[-1:]
    tail_meta = (flat(jnp.concatenate([padded - used, rest]) // ROW_CHUNK),
                 flat(jnp.concatenate([p_starts + used, p_ends[-1:]])))
    occupied_end = jnp.sum(jnp.where(block_expert[:, None] == jnp.arange(N_EXPERTS, dtype=I32)[None, :],
                                     (p_starts + used)[None, :], 0), axis=1)
    valid = jnp.clip(occupied_end - block_row, 0, EXPERT_BLOCK).astype(I32)
    return chunk_meta, tail_meta, (block_expert.astype(I32), n_used, next_expert, valid)


def _constants():
    gm = np.kron(np.eye(MXU_DEPTH // HEAD_DIM), np.full((HEAD_DIM, HEAD_DIM), 1.0 / HEAD_DIM))
    tri = np.tril(np.ones((TOKEN_TILE, TOKEN_TILE), np.float32), -1)
    upper = np.triu(np.ones((LANES, LANES), np.float32), 1)
    return jnp.asarray(gm, BF16), jnp.asarray(tri, BF16), jnp.asarray(upper, BF16)


def kernel(x, rel_bias_table, norm_mix, w_in, q_norm, k_norm, conv_w, conv_b, rg_w_a, rg_b_a, rg_w_x, rg_b_x,
           rg_lambda, norm_attn_out, norm_rnn_out, w_out, norm_ffn, router_group_w, router_group_b,
           router_expert_w, router_expert_b, expert_w_gate, expert_w_up, expert_w_down):
    b, s, d = x.shape
    depth = w_in.shape[0]
    m = b * s
    assert d == D_MODEL and s % ATTN_TILE == 0 and s % TOKEN_TILE == 0

    gm, tri, upper = _constants()
    bias = _bias_tables(rel_bias_table)
    scale = HEAD_DIM ** -0.5
    n_tiles = m // TOKEN_TILE
    n_blocks = -(-(m * TOP_K + n_tiles * N_EXPERTS * (ROW_CHUNK - 1)) // EXPERT_BLOCK) + N_EXPERTS
    n_rows = n_blocks * EXPERT_BLOCK

    vec = lambda v: v.reshape(depth, 1, -1).astype(F32)
    per_mxu = MXU_DEPTH // (D_RNN // RNN_BLOCKS)
    proj_params = (
        vec(norm_mix), w_in.astype(BF16),
        vec(jnp.tile(q_norm, (1, ATTN_HEADS)) * (scale * LOG2E)), vec(jnp.tile(k_norm, (1, ATTN_HEADS))),
        jnp.broadcast_to(gm, (depth,) + gm.shape), conv_w.astype(F32), vec(conv_b),
        _block_diag(rg_w_a, per_mxu).astype(BF16), vec(rg_b_a),
        _block_diag(rg_w_x, per_mxu).astype(BF16), vec(rg_b_x), vec(rg_lambda))
    wr = jnp.zeros((depth, D_MODEL, LANES), F32)
    wr = wr.at[:, :, :N_GROUPS].set(router_group_w).at[:, :, N_GROUPS:N_GROUPS + N_EXPERTS].set(router_expert_w)
    br = jnp.zeros((depth, 1, LANES), F32)
    br = br.at[:, 0, :N_GROUPS].set(router_group_b).at[:, 0, N_GROUPS:N_GROUPS + N_EXPERTS].set(router_expert_b)
    mix_params = (vec(norm_attn_out), vec(norm_rnn_out), w_out.astype(BF16), vec(norm_ffn),
                  jnp.concatenate(_split_bf16(wr), axis=2), br)

    x2 = x.reshape(m, d).astype(F32)
    q, k, v, rnn = _edge(x2, s, proj_in=proj_params, layer=0)
    for l in range(depth):
        attn = _attention(q, k, v, bias, b, s)
        x1, h2, pos, gates, cnt = _mix(x2, attn, rnn, mix_params, l, tri, upper)

        chunk_meta, tail_meta, block_meta = _dispatch_plan(cnt, n_tiles, n_blocks)
        rows = _dispatch(chunk_meta + tail_meta, h2, pos, gates, n_rows)
        out_rows = _experts(*block_meta, rows, expert_w_gate, expert_w_up, expert_w_down, l)
        nxt = proj_params if l + 1 < depth else None
        x2, *started = _edge(x1, s, combine_in=(chunk_meta, pos, out_rows), proj_in=nxt, layer=l + 1)
        if started:
            q, k, v, rnn = started
    return x2.reshape(b, s, d).astype(x.dtype)
```

```python
import functools
import math

import numpy as np
import jax
import jax.numpy as jnp
from jax import lax
from jax.experimental import pallas as pl
from jax.experimental.pallas import tpu as pltpu

F32 = jnp.float32
BF16 = jnp.bfloat16
I32 = jnp.int32
U32 = jnp.uint32

D_MODEL = 1024
ATTN_HEADS = 8
HEAD_DIM = 64
D_ATTN = ATTN_HEADS * HEAD_DIM
RNN_BLOCKS = 8
D_RNN = 512
D_MIX = D_ATTN + D_RNN
D_IN = 3 * D_ATTN + 2 * D_RNN
DILATED_BRANCHES = ((128, 1), (512, 4), (2048, 16))
Q_BLOCK = 128
REL_BUCKETS = 32
REL_MAX_DIST = 2048
CONV_WIDTH = 4
RG_C = 8.0
N_GROUPS = 4
EXPERTS_PER_GROUP = 8
N_EXPERTS = N_GROUPS * EXPERTS_PER_GROUP
TOP_K = 2
D_EXPERT = 512
EPS = 1e-6
NEG_INF = -1e30
LOG2E = math.log2(math.e)

LANES = 128
SUBLANES = 8
MXU_DEPTH = 256
HEAD_PAIRS = D_ATTN // LANES
ROUTER_LANE0 = N_GROUPS
TOKEN_TILE = 512
ATTN_TILE = 2048
ATTN_PAIRS = 2
ATTN_UNROLL = 16
RNN_CHUNK = 64
EXPERT_SUBBLOCK = 128
EXPERT_BLOCK = 512
ROW_CHUNK = 8
TILE_ROWS = -(-(TOKEN_TILE * TOP_K + N_EXPERTS * (ROW_CHUNK - 1)) // MXU_DEPTH) * MXU_DEPTH
TILE_CHUNKS = TILE_ROWS // ROW_CHUNK
WAIT_BATCH = 16
ROW_WORDS = D_MODEL // 2 + LANES
MERGE_ROWS = 256
VMEM_LIMIT = 48 * 1024 * 1024
ATTN_VMEM_LIMIT = 56 * 1024 * 1024


def _cparams(*sem):
    return pltpu.CompilerParams(dimension_semantics=sem, vmem_limit_bytes=VMEM_LIMIT)


def _rms(x, gain):
    return x * lax.rsqrt(jnp.mean(x * x, axis=-1, keepdims=True) + EPS) * gain


def _rows(start, size, stride):
    return pl.ds(start, size) if stride == 1 else pl.ds(start, size, stride=stride)


def _attn_kernel(q_ref, k_ref, v_ref, bias_ref, out_ref, qd, stage, o_scr, l_scr, *kv_scr):
    last = len(DILATED_BRANCHES) - 1
    for p in range(ATTN_PAIRS):
        for n, (_, dil) in enumerate(DILATED_BRANCHES):
            o_dst = out_ref.at[p] if n == last else o_scr.at[n, p]
            _attn_pair(q_ref.at[p], k_ref.at[p], v_ref.at[p], bias_ref.at[n], 2 * p, o_dst, l_scr.at[n, p],
                       qd, kv_scr[2 * n].at[p], kv_scr[2 * n + 1].at[p], stage, dil)

        def merge(c, carry):
            rows = pl.ds(pl.multiple_of(c * MERGE_ROWS, MERGE_ROWS), MERGE_ROWS)
            lse = [l_scr[n, p, rows, :] for n in range(last + 1)]
            mx = functools.reduce(jnp.maximum, lse)
            w = [jnp.exp2(l - mx) for l in lse]
            o = [o_scr[n, p, rows, :] for n in range(last)] + [out_ref[p, rows, :]]
            num = functools.reduce(jnp.add, [wi * oi for wi, oi in zip(w, o)])
            out_ref[p, rows, :] = num * (1.0 / functools.reduce(jnp.add, w))
            return carry

        lax.fori_loop(0, out_ref.shape[1] // MERGE_ROWS, merge, 0)


def _attn_pair(q_ref, k_ref, v_ref, bias_ref, head0, o_ref, lse_ref, qd, kd, vd, stage, dil):
    i = pl.program_id(2)
    tile = q_ref.shape[0]
    rows = tile // dil
    seg = Q_BLOCK + rows

    @pl.when(i == 0)
    def _():
        for r in range(dil):
            kd[r * seg:r * seg + Q_BLOCK, :] = jnp.zeros((Q_BLOCK, LANES), BF16)
            for hh in range(2):
                vd[hh, r * seg:r * seg + Q_BLOCK, :] = jnp.zeros((Q_BLOCK, LANES), BF16)

    @pl.when(i > 0)
    def _():
        for r in range(dil):
            kd[r * seg:r * seg + Q_BLOCK, :] = kd[r * seg + rows:(r + 1) * seg, :]
            for hh in range(2):
                vd[hh, r * seg:r * seg + Q_BLOCK, :] = vd[hh, r * seg + rows:(r + 1) * seg, :]

    def regroup(src_ref, put):
        if dil % 16 == 0:
            quarter = tile // 4
            for c in range(4):
                stage[0, c * quarter:(c + 1) * quarter, :] = src_ref[pl.ds(c, quarter, stride=4), :]
            for r in range(dil):
                put(r, stage[0, _rows((r % 4) * quarter + r // 4, rows, dil // 4), :].astype(BF16))
        else:
            for r in range(dil):
                put(r, src_ref[_rows(r, rows, dil), :].astype(BF16))

    def put_q(r, x):
        qd[r * rows:(r + 1) * rows, :] = x

    def put_k(r, x):
        kd[r * seg + Q_BLOCK:(r + 1) * seg, :] = x

    first_head = lax.broadcasted_iota(I32, (rows, LANES), 1) < HEAD_DIM

    def put_v(r, x):
        vd[0, r * seg + Q_BLOCK:(r + 1) * seg, :] = jnp.where(first_head, x, jnp.zeros_like(x))
        vd[1, r * seg + Q_BLOCK:(r + 1) * seg, :] = jnp.where(first_head, jnp.zeros_like(x), x)

    regroup(q_ref, put_q)
    regroup(k_ref, put_k)
    regroup(v_ref, put_v)

    per = rows // Q_BLOCK
    low = lax.broadcasted_iota(I32, (Q_BLOCK, LANES), 1) < HEAD_DIM
    key_lane = lax.broadcasted_iota(I32, (2 * Q_BLOCK, LANES), 1)
    ones = [jnp.where(key_lane < HEAD_DIM, 1.0, 0.0).astype(BF16),
            jnp.where(key_lane < HEAD_DIM, 0.0, 1.0).astype(BF16)]

    def block(blk, carry):
        r = blk // per
        n = blk % per
        q0 = pl.multiple_of(r * rows + n * Q_BLOCK, Q_BLOCK)
        k0 = pl.multiple_of(r * seg + n * Q_BLOCK, Q_BLOCK)
        sel = jnp.where(jnp.logical_and(i == 0, n == 0), 1, 0)
        q = qd[pl.ds(q0, Q_BLOCK), :]
        kk = kd[pl.ds(k0, 2 * Q_BLOCK), :]
        probs, maxes = [], []
        for hh in range(2):
            qm = jnp.where(low if hh == 0 else jnp.logical_not(low), q, jnp.zeros_like(q))
            s = lax.dot_general(qm, kk, (((1,), (1,)), ((), ())), preferred_element_type=F32)
            s = s + bias_ref[sel, head0 + hh]
            mx = jnp.max(s, axis=-1, keepdims=True)
            probs.append(jnp.exp2(s - mx).astype(BF16))
            maxes.append(mx)
        rhs = jnp.concatenate([jnp.concatenate([vd[hh, pl.ds(k0, 2 * Q_BLOCK), :], ones[hh]], axis=1)
                               for hh in range(2)], axis=0)
        both = jnp.dot(jnp.concatenate(probs, axis=1), rhs, preferred_element_type=F32)
        l = both[:, LANES:]
        o_tile = both[:, :LANES] * (1.0 / l)
        lse_tile = jnp.where(low, maxes[0], maxes[1]) + jnp.log(l) * LOG2E
        if dil % 16 == 0:
            dst = _rows((r % 4) * (tile // 4) + r // 4 + (dil // 4) * Q_BLOCK * n, Q_BLOCK, dil // 4)
            stage[1, dst, :] = o_tile
            stage[2, dst, :] = lse_tile
        else:
            dst = _rows(r + dil * Q_BLOCK * n, Q_BLOCK, dil)
            o_ref[dst, :] = o_tile
            lse_ref[dst, :] = lse_tile
        return carry

    lax.fori_loop(0, dil * per, block, 0, unroll=ATTN_UNROLL)
    if dil % 16 == 0:
        quarter = tile // 4
        for c in range(4):
            o_ref[pl.ds(c, quarter, stride=4), :] = stage[1, c * quarter:(c + 1) * quarter, :]
            lse_ref[pl.ds(c, quarter, stride=4), :] = stage[2, c * quarter:(c + 1) * quarter, :]


def _attention(q, k, v, bias, b, s):
    tile = ATTN_TILE
    nt = s // tile
    nb = len(DILATED_BRANCHES)
    blk = pl.BlockSpec((ATTN_PAIRS, tile, LANES), lambda bb, g, i: (g, bb * nt + i, 0))
    kv_scr = []
    for _, dil in DILATED_BRANCHES:
        keys = tile + Q_BLOCK * dil
        kv_scr += [pltpu.VMEM((ATTN_PAIRS, keys, LANES), BF16), pltpu.VMEM((ATTN_PAIRS, 2, keys, LANES), BF16)]
    return pl.pallas_call(
        _attn_kernel,
        grid=(b, HEAD_PAIRS // ATTN_PAIRS, nt),
        in_specs=[blk, blk, blk,
                  pl.BlockSpec((nb, 2, 2 * ATTN_PAIRS, Q_BLOCK, 2 * Q_BLOCK), lambda bb, g, i: (0, 0, g, 0, 0))],
        out_specs=blk,
        out_shape=jax.ShapeDtypeStruct(q.shape, F32),
        scratch_shapes=[pltpu.VMEM((tile, LANES), BF16), pltpu.VMEM((3, tile, LANES), F32),
                        pltpu.VMEM((nb - 1, ATTN_PAIRS, tile, LANES), F32),
                        pltpu.VMEM((nb, ATTN_PAIRS, tile, LANES), F32)] + kv_scr,
        compiler_params=pltpu.CompilerParams(dimension_semantics=("parallel", "parallel", "arbitrary"),
                                             vmem_limit_bytes=ATTN_VMEM_LIMIT),
    )(q, k, v, bias)


def _bias_step_tables():
    exact = REL_BUCKETS // 2
    i = np.arange(Q_BLOCK)[:, None]
    j = np.arange(2 * Q_BLOCK)[None, :]
    steps = i + Q_BLOCK - j
    onehot, band = [], []
    for window, dil in DILATED_BRANCHES:
        dist = (np.arange(2 * Q_BLOCK) * dil).astype(np.int32)
        d = np.maximum(dist, 1).astype(np.float32)
        log_b = exact + (np.log(d / np.float32(exact)) / np.float32(math.log(REL_MAX_DIST / exact))
                         * np.float32(REL_BUCKETS - exact)).astype(np.int32)
        bucket = np.where(dist < exact, dist, np.minimum(log_b, REL_BUCKETS - 1))
        onehot.append(np.eye(REL_BUCKETS, dtype=np.float32)[bucket])
        band.append((steps >= 0) & (steps <= window // dil))
    first = np.broadcast_to(j >= Q_BLOCK, (Q_BLOCK, 2 * Q_BLOCK))
    return np.stack(onehot), np.stack(band), first


def _bias_tables(rel_table):
    onehot, band, first = _bias_step_tables()
    nb = len(DILATED_BRANCHES)
    vec = LOG2E * jnp.einsum('nsb,bh->nhs', onehot, rel_table.astype(F32), precision=lax.Precision.HIGHEST)
    period = 3 * Q_BLOCK
    padded = jnp.concatenate([vec, jnp.zeros((nb, ATTN_HEADS, Q_BLOCK), F32)], axis=-1)
    wrapped = jnp.roll(padded[..., ::-1], Q_BLOCK + 1, axis=-1)
    bias = jnp.tile(wrapped, (1, 1, Q_BLOCK))[..., :Q_BLOCK * (period - 1)].reshape(
        nb, ATTN_HEADS, Q_BLOCK, period - 1)[..., :2 * Q_BLOCK]
    regular = jnp.where(band[:, None], bias, NEG_INF)
    start = jnp.where((band & first[None])[:, None], bias, NEG_INF)
    return jnp.stack([regular, start], axis=1)


def _block_diag(w, group):
    *lead, nb, n, _ = w.shape
    w = w.reshape(*lead, nb // group, group, n, n)
    eye = jnp.eye(group, dtype=w.dtype)
    return (eye[:, None, :, None] * w[..., :, :, None, :]).reshape(*lead, nb // group, group * n, group * n)


def _split_bf16(x):
    hi = x.astype(BF16)
    return hi, (x - hi.astype(F32)).astype(BF16)


def _mix_kernel(x_ref, attn_ref, rnn_ref, ga_ref, gn_ref, wo_ref, gf_ref,
                wr_ref, br_ref, tri_ref, upper_ref,
                x1_ref, h2_ref, pos_ref, gate_ref, cnt_ref):
    slabs = [attn_ref[hp] for hp in range(HEAD_PAIRS)]
    sumsq = functools.reduce(jnp.add, [jnp.sum(a * a, axis=-1, keepdims=True) for a in slabs])
    scale = lax.rsqrt(sumsq * (1.0 / D_ATTN) + EPS)
    na = (jnp.concatenate([a * scale for a in slabs], axis=1) * ga_ref[...]).astype(BF16)
    nr = _rms(rnn_ref[...], gn_ref[...]).astype(BF16)
    x1 = (x_ref[...] + jnp.dot(na, wo_ref[0:D_ATTN, :], preferred_element_type=F32)
          + jnp.dot(nr, wo_ref[D_ATTN:, :], preferred_element_type=F32))
    x1_ref[...] = x1
    h2 = _rms(x1, gf_ref[...])
    h2_ref[...] = h2.astype(BF16)

    hh, hl = _split_bf16(h2)
    by_hi = jnp.dot(hh, wr_ref[...], preferred_element_type=F32)
    by_lo = jnp.dot(hl, wr_ref[...], preferred_element_type=F32)
    logits = by_hi[:, :LANES] + by_hi[:, LANES:] + by_lo[:, :LANES] + br_ref[...]
    tm = logits.shape[0]
    lane = lax.broadcasted_iota(I32, (tm, LANES), 1)
    lanef = lane.astype(F32)
    big = float(LANES)

    def top(vals):
        m = jnp.max(vals, axis=-1, keepdims=True)
        return m, jnp.min(jnp.where(vals == m, lanef, big), axis=-1, keepdims=True)

    is_group = lane < N_GROUPS
    gmax, gsel = top(jnp.where(is_group, logits, NEG_INF))
    g_w = 1.0 / jnp.sum(jnp.where(is_group, jnp.exp(logits - gmax), 0.0), axis=-1, keepdims=True)
    lo_lane = ROUTER_LANE0 + EXPERTS_PER_GROUP * gsel
    in_group = jnp.logical_and(lanef >= lo_lane, lanef < lo_lane + EXPERTS_PER_GROUP)
    el = jnp.where(in_group, logits, NEG_INF)
    v1, i1 = top(el)
    v2, i2 = top(jnp.where(lanef == i1, NEG_INF, el))
    t = jnp.exp(v2 - v1)
    p1 = 1.0 / (1.0 + t)
    gate1 = g_w * p1
    gate2 = g_w * (t * p1)

    oh1 = (lanef == i1).astype(F32)
    oh2 = (lanef == i2).astype(F32)
    cnt = oh1 + oh2
    prefix = jnp.dot(tri_ref[...], cnt.astype(BF16), preferred_element_type=F32)
    total = jnp.sum(cnt, axis=0, keepdims=True)
    chunks = jnp.floor((total + (ROW_CHUNK - 1)) * (1.0 / ROW_CHUNK))
    seg_start = ROW_CHUNK * jnp.dot(jnp.broadcast_to(chunks, (SUBLANES, LANES)).astype(BF16), upper_ref[...],
                                    preferred_element_type=F32)[0:1, :]
    base = seg_start + prefix
    pos1 = jnp.sum(oh1 * base, axis=-1, keepdims=True)
    pos2 = jnp.sum(oh2 * base, axis=-1, keepdims=True)
    cnt_ref[...] = jnp.broadcast_to(total, cnt_ref.shape).astype(I32)
    pos_ref[...] = jnp.where(lane == 0, pos1, jnp.where(lane == 1, pos2, 0.0))
    gate_ref[...] = jnp.where(lane == 0, gate1, jnp.where(lane == 1, gate2, 0.0))


def _layer_spec(a, layer):
    return pl.BlockSpec((None,) + a.shape[1:], lambda i, *_: (layer,) + (0,) * (a.ndim - 1))


def _mix(x2, attn, rnn, layer_params, layer, tri, upper):
    m = x2.shape[0]
    tm = TOKEN_TILE
    row = lambda i: (i, 0)
    fix = lambda i: (0, 0)
    t512 = pl.BlockSpec((tm, D_ATTN), row)
    t128 = pl.BlockSpec((tm, LANES), row)
    t1024 = pl.BlockSpec((tm, D_MODEL), row)
    slab = pl.BlockSpec((HEAD_PAIRS, tm, LANES), lambda i: (0, i, 0))
    return pl.pallas_call(
        _mix_kernel,
        grid=(m // tm,),
        in_specs=[t1024, slab, t512] + [_layer_spec(a, layer) for a in layer_params]
        + [pl.BlockSpec((tm, tm), fix), pl.BlockSpec((LANES, LANES), fix)],
        out_specs=[t1024, t1024, t128, t128, pl.BlockSpec((SUBLANES, LANES), row)],
        out_shape=[jax.ShapeDtypeStruct((m, D_MODEL), F32), jax.ShapeDtypeStruct((m, D_MODEL), BF16),
                   jax.ShapeDtypeStruct((m, LANES), F32), jax.ShapeDtypeStruct((m, LANES), F32),
                   jax.ShapeDtypeStruct((m // tm * SUBLANES, LANES), I32)],
        compiler_params=_cparams("parallel"),
    )(x2, attn, rnn, *layer_params, tri, upper)


def _rows_copy(src, src_row, dst, dst_row, sem, rows=ROW_CHUNK):
    return pltpu.make_async_copy(src.at[pl.ds(src_row, rows)], dst.at[pl.ds(dst_row, rows)], sem)


def _for_each_chunk(i, ntot_ref, dst_ref, copy):
    n = ntot_ref[i]

    def body(j, carry):
        copy(pl.multiple_of(j * ROW_CHUNK, ROW_CHUNK), pl.multiple_of(dst_ref[i * TILE_CHUNKS + j], ROW_CHUNK))
        return carry

    lax.fori_loop(0, n, body, 0)
    return n


def _wait_chunks(count, wait_rows):
    def batch(c, carry):
        wait_rows(WAIT_BATCH * ROW_CHUNK)
        return carry

    def single(c, carry):
        wait_rows(ROW_CHUNK)
        return carry

    lax.fori_loop(0, count // WAIT_BATCH, batch, 0)
    lax.fori_loop(0, count % WAIT_BATCH, single, 0)


def _pack_bf16_pairs(x):
    c = x.shape[1] // 2
    lo = lax.bitcast_convert_type(x[:, :c], U32) >> 16
    hi = lax.bitcast_convert_type(x[:, c:], U32) & jnp.uint32(0xFFFF0000)
    return hi | lo


def _unpack_bf16_pairs(w):
    lo = lax.bitcast_convert_type(w << 16, F32).astype(BF16)
    hi = lax.bitcast_convert_type(w & jnp.uint32(0xFFFF0000), F32).astype(BF16)
    return lo, hi


def _dispatch_kernel(ntot_ref, dst_ref, tailn_ref, taildst_ref,
                     h_ref, pos_ref, gate_ref, rows_ref, sorted_buf, zero_buf, pending, sems, *, tm):
    i = pl.program_id(0)
    slot = i % 2
    pos_t = pos_ref[...].T
    gate_t = gate_ref[...].T
    row = lax.broadcasted_iota(I32, (TILE_ROWS, tm), 0)
    hit1 = row == pos_t[0:1, :].astype(I32)
    hit2 = row == pos_t[1:2, :].astype(I32)
    onehot = jnp.where(jnp.logical_or(hit1, hit2), 1.0, 0.0).astype(BF16)
    feat = jnp.dot(onehot, h_ref[...], preferred_element_type=F32)
    sorted_buf[slot, :, 0:ROW_WORDS - LANES] = _pack_bf16_pairs(feat)
    gate = jnp.sum(jnp.where(hit1, gate_t[0:1, :], jnp.where(hit2, gate_t[1:2, :], 0.0)), axis=-1, keepdims=True)
    lane = lax.broadcasted_iota(I32, (TILE_ROWS, LANES), 1)
    sorted_buf[slot, :, ROW_WORDS - LANES:] = jnp.where(lane == 0, lax.bitcast_convert_type(gate, U32), jnp.uint32(0))

    def wait_rows(sem):
        return lambda r: _rows_copy(sorted_buf.at[0], 0, rows_ref, 0, sem, r).wait()

    @pl.when(i > 0)
    def _():
        _wait_chunks(pending[0], wait_rows(sems.at[1 - slot]))

    def send(loc, dst):
        _rows_copy(sorted_buf.at[slot], loc, rows_ref, dst, sems.at[slot]).start()

    n_sent = _for_each_chunk(i, ntot_ref, dst_ref, send)

    @pl.when(i == 0)
    def _():
        zero_buf[...] = jnp.zeros_like(zero_buf)

    def tails():
        def per_expert(e, total):
            n = tailn_ref[e]
            dst0 = taildst_ref[e]
            big = n // WAIT_BATCH

            def batch(c, carry):
                _rows_copy(zero_buf, 0, rows_ref, pl.multiple_of(dst0 + c * (WAIT_BATCH * ROW_CHUNK), ROW_CHUNK),
                           sems.at[slot], WAIT_BATCH * ROW_CHUNK).start()
                return carry

            def single(c, carry):
                _rows_copy(zero_buf, 0, rows_ref, pl.multiple_of(dst0 + c * ROW_CHUNK, ROW_CHUNK),
                           sems.at[slot]).start()
                return carry

            lax.fori_loop(0, big, batch, 0)
            lax.fori_loop(big * WAIT_BATCH, n, single, 0)
            return total + n

        return lax.fori_loop(0, N_EXPERTS + 1, per_expert, jnp.int32(0))

    n_sent = n_sent + lax.cond(i == 0, tails, lambda: jnp.int32(0))
    pending[0] = n_sent

    @pl.when(i == pl.num_programs(0) - 1)
    def _():
        _wait_chunks(n_sent, wait_rows(sems.at[slot]))


def _dispatch(meta, h2, pos, gates, n_rows):
    m = h2.shape[0]
    tm = TOKEN_TILE
    tile = lambda i, *_: (i, 0)
    return pl.pallas_call(
        functools.partial(_dispatch_kernel, tm=tm),
        grid_spec=pltpu.PrefetchScalarGridSpec(
            num_scalar_prefetch=4,
            grid=(m // tm,),
            in_specs=[pl.BlockSpec((tm, D_MODEL), tile), pl.BlockSpec((tm, LANES), tile),
                      pl.BlockSpec((tm, LANES), tile)],
            out_specs=pl.BlockSpec(memory_space=pl.ANY),
            scratch_shapes=[pltpu.VMEM((2, TILE_ROWS, ROW_WORDS), U32),
                            pltpu.VMEM((WAIT_BATCH * ROW_CHUNK, ROW_WORDS), U32),
                            pltpu.SMEM((1,), I32), pltpu.SemaphoreType.DMA((2,))],
        ),
        out_shape=jax.ShapeDtypeStruct((n_rows, ROW_WORDS), U32),
        compiler_params=_cparams("arbitrary"),
    )(*meta, h2, pos, gates)


def _expert_kernel(be_ref, nused_ref, next_ref, valid_ref, rows_ref, wg_hbm, wu_hbm, wd_hbm, out_ref,
                   wgf, wuf, wdf, wgb, wub, wdb, holder, sems, *, layer):
    i = pl.program_id(0)
    used = i < nused_ref[0]
    expert = be_ref[i]
    first = jnp.logical_or(i == 0, expert != be_ref[jnp.maximum(i - 1, 0)])

    def weight_copies(e, slot):
        return (pltpu.make_async_copy(wg_hbm.at[layer, e], wgf.at[slot], sems.at[slot, 0]),
                pltpu.make_async_copy(wu_hbm.at[layer, e], wuf.at[slot], sems.at[slot, 1]),
                pltpu.make_async_copy(wd_hbm.at[layer, e], wdf.at[slot], sems.at[slot, 2]))

    @pl.when(jnp.logical_and(used, first))
    def _():
        @pl.when(i == 0)
        def _():
            holder[0] = 0
            for c in weight_copies(expert, 0):
                c.start()

        slot = holder[0]
        for c in weight_copies(expert, slot):
            c.wait()
        wgb[...] = wgf[slot].astype(BF16)
        wub[...] = wuf[slot].astype(BF16)
        wdb[...] = wdf[slot].astype(BF16)
        nxt = next_ref[i]

        @pl.when(nxt >= 0)
        def _():
            for c in weight_copies(nxt, 1 - slot):
                c.start()

        holder[0] = 1 - slot

    def mlp(r0, nrows):
        rows = rows_ref[r0:r0 + nrows, :]
        xb = jnp.concatenate(_unpack_bf16_pairs(rows[:, 0:ROW_WORDS - LANES]), axis=1)
        gate = lax.bitcast_convert_type(rows[:, ROW_WORDS - LANES:][:, 0:1], F32)
        g = jnp.dot(xb, wgb[...], preferred_element_type=F32)
        u = jnp.dot(xb, wub[...], preferred_element_type=F32)
        act = (g * jax.nn.sigmoid(g) * u).astype(BF16)
        y = jnp.dot(act, wdb[...], preferred_element_type=F32) * gate
        out_ref[r0:r0 + nrows, :] = _pack_bf16_pairs(y.astype(BF16).astype(F32))

    blk = rows_ref.shape[0]
    valid = valid_ref[i]

    fused = valid > blk // 2

    @pl.when(fused)
    def _():
        mlp(0, blk)

    @pl.when(jnp.logical_not(fused))
    def _():
        for r0 in range(0, blk, EXPERT_SUBBLOCK):
            @pl.when(r0 < valid)
            def _():
                mlp(r0, EXPERT_SUBBLOCK)

            @pl.when(r0 >= valid)
            def _():
                out_ref[r0:r0 + EXPERT_SUBBLOCK, :] = jnp.zeros((EXPERT_SUBBLOCK, out_ref.shape[1]), out_ref.dtype)


def _experts(block_expert, n_used, next_expert, valid, rows, w_gate, w_up, w_down, layer):
    n_rows = rows.shape[0]
    blk = EXPERT_BLOCK
    rmap = lambda i, be, nu, nx, va: (jnp.minimum(i, nu[0] - 1), 0)
    hbm = pl.BlockSpec(memory_space=pl.ANY)
    return pl.pallas_call(
        functools.partial(_expert_kernel, layer=layer),
        grid_spec=pltpu.PrefetchScalarGridSpec(
            num_scalar_prefetch=4,
            grid=(n_rows // blk,),
            in_specs=[pl.BlockSpec((blk, ROW_WORDS), rmap), hbm, hbm, hbm],
            out_specs=pl.BlockSpec((blk, D_MODEL // 2), lambda i, be, nu, nx, va: (i, 0)),
            scratch_shapes=[pltpu.VMEM((2, D_MODEL, D_EXPERT), F32), pltpu.VMEM((2, D_MODEL, D_EXPERT), F32),
                            pltpu.VMEM((2, D_EXPERT, D_MODEL), F32),
                            pltpu.VMEM((D_MODEL, D_EXPERT), BF16), pltpu.VMEM((D_MODEL, D_EXPERT), BF16),
                            pltpu.VMEM((D_EXPERT, D_MODEL), BF16),
                            pltpu.SMEM((1,), I32), pltpu.SemaphoreType.DMA((2, 3))],
        ),
        out_shape=jax.ShapeDtypeStruct((n_rows, D_MODEL // 2), U32),
        compiler_params=_cparams("arbitrary"),
    )(block_expert, n_used, next_expert, valid, rows, w_gate, w_up, w_down)


def _proj_section(hb, w_ref, n):
    return jnp.dot(hb, w_ref[:, n * D_ATTN:(n + 1) * D_ATTN], preferred_element_type=F32)


def _proj_qkv(hb, w_ref, qg_ref, kg_ref, gm_ref, q_ref, k_ref, v_ref):
    sec = functools.partial(_proj_section, hb, w_ref)

    def head_norm(z, gain):
        zz = (z * z).astype(BF16)
        half = gm_ref.shape[0]
        ms = jnp.concatenate([jnp.dot(zz[:, c:c + half], gm_ref[...], preferred_element_type=F32)
                              for c in range(0, D_ATTN, half)], axis=1)
        return z * lax.rsqrt(ms + EPS) * gain

    def put_slabs(ref, z):
        for hp in range(HEAD_PAIRS):
            ref[hp] = z[:, hp * LANES:(hp + 1) * LANES]

    put_slabs(q_ref, head_norm(sec(0), qg_ref[...]))
    put_slabs(k_ref, head_norm(sec(1), kg_ref[...]))
    put_slabs(v_ref, sec(2))


def _rglru_gates(xr, gr, cw_ref, cb_ref, wa_ref, ba_ref, wx_ref, bx_ref, lam_ref, xbuf, a_scr, b_scr, g_scr):
    tt = xr.shape[0]
    head = SUBLANES
    xbuf[head:head + tt, :] = xr
    g_scr[...] = jax.nn.gelu(gr, approximate=True)
    last = CONV_WIDTH - 1
    u = cb_ref[...] + cw_ref[last:last + 1, :] * xr
    for back in range(1, CONV_WIDTH):
        u = u + cw_ref[last - back:last - back + 1, :] * xbuf[head - back:head - back + tt, :]
    xbuf[0:head, :] = xbuf[tt:tt + head, :]

    ub = u.astype(BF16)

    def gate(w_ref, b_ref):
        deep = w_ref.shape[1]
        z = jnp.concatenate([jnp.dot(ub[:, c * deep:(c + 1) * deep], w_ref[c], preferred_element_type=F32)
                             for c in range(w_ref.shape[0])], axis=1)
        return jax.nn.sigmoid(z + b_ref[...])

    r = gate(wa_ref, ba_ref)
    gi = gate(wx_ref, bx_ref)
    nl = -lam_ref[...]
    softplus = jnp.maximum(nl, 0.0) + jnp.log1p(jnp.exp(-jnp.abs(nl)))
    log_a = (-RG_C) * r * softplus
    a = jnp.exp(log_a)
    a_scr[...] = a
    b_scr[...] = jnp.sqrt(-jnp.tanh(log_a) * (a * a + 1.0)) * (gi * u)


def _rglru_scan(out_ref, a_scr, b_scr, h_scr, g_scr):
    tt = out_ref.shape[0]
    rc = RNN_CHUNK
    sub = lax.broadcasted_iota(I32, (SUBLANES, LANES), 0)

    def chunk(c, carry):
        r0 = pl.multiple_of(c * rc, rc)
        for g in range(D_RNN // LANES):
            ls = slice(g * LANES, (g + 1) * LANES)
            h = h_scr[0:1, ls]
            for v in range(rc // SUBLANES):
                rows = pl.ds(r0 + v * SUBLANES, SUBLANES)
                aa = a_scr[rows, ls]
                bb = b_scr[rows, ls]
                k = 1
                while k < SUBLANES:
                    keep = sub >= k
                    a_sh = pltpu.roll(aa, k, 0)
                    b_sh = pltpu.roll(bb, k, 0)
                    bb = jnp.where(keep, aa * b_sh + bb, bb)
                    aa = jnp.where(keep, aa * a_sh, aa)
                    k *= 2
                hv = aa * h + bb
                h = hv[SUBLANES - 1:SUBLANES, :]
                out_ref[rows, ls] = hv * g_scr[rows, ls]
            h_scr[0:1, ls] = h
        return carry

    lax.fori_loop(0, tt // rc, chunk, 0)


def _combine_body(ntot_ref, dst_ref, x1_ref, pos_ref, rows_ref, local, pending, sems):
    i = pl.program_id(0)
    slot = i % 2
    tm = x1_ref.shape[0]

    def fetch_tile(t, s):
        def fetch(loc, src):
            _rows_copy(rows_ref, src, local.at[s], loc, sems.at[s]).start()

        return _for_each_chunk(t, ntot_ref, dst_ref, fetch)

    @pl.when(i == 0)
    def _():
        local[...] = jnp.zeros_like(local)
        pending[0] = fetch_tile(0, 0)

    @pl.when(i + 1 < pl.num_programs(0))
    def _():
        pending[1 - slot] = fetch_tile(i + 1, 1 - slot)

    _wait_chunks(pending[slot], lambda r: _rows_copy(rows_ref, 0, local.at[0], 0, sems.at[slot], r).wait())

    pos = pos_ref[...]
    col = lax.broadcasted_iota(I32, (tm, TILE_ROWS), 1)
    slots = pos.astype(I32)
    pick = jnp.where(jnp.logical_or(col == slots[:, 0:1], col == slots[:, 1:2]), 1.0, 0.0).astype(BF16)
    lo, hi = _unpack_bf16_pairs(local[slot])
    moe = jnp.concatenate([jnp.dot(pick, lo, preferred_element_type=F32),
                           jnp.dot(pick, hi, preferred_element_type=F32)], axis=1)
    return x1_ref[...] + moe


N_PROJ_IN = 12


def _edge_kernel(*refs, with_combine, with_proj, tiles_per_seq):
    refs = list(refs)
    take = lambda n: [refs.pop(0) for _ in range(n)]
    if with_combine:
        ntot_ref, dst_ref = take(2)
    (x_ref,) = take(1)
    if with_combine:
        pos_ref, rows_ref = take(2)
    if with_proj:
        g_ref, w_ref, qg_ref, kg_ref, gm_ref, cw_ref, cb_ref, wa_ref, ba_ref, wx_ref, bx_ref, lam_ref = take(N_PROJ_IN)
    if with_combine:
        (x2_ref,) = take(1)
    if with_proj:
        q_ref, k_ref, v_ref, rnn_ref = take(4)
    if with_combine:
        local, pending, sems = take(3)
    if with_proj:
        xbuf, a_scr, b_scr, h_scr, g_scr = take(5)

    if with_proj:
        @pl.when(pl.program_id(0) % tiles_per_seq == 0)
        def _():
            xbuf[0:SUBLANES, :] = jnp.zeros((SUBLANES, D_RNN), F32)
            h_scr[...] = jnp.zeros_like(h_scr)

    if with_combine:
        x = _combine_body(ntot_ref, dst_ref, x_ref, pos_ref, rows_ref, local, pending, sems)
        x2_ref[...] = x
    else:
        x = x_ref[...]
    if with_proj:
        hb = _rms(x, g_ref[...]).astype(BF16)
        _rglru_gates(_proj_section(hb, w_ref, 3), _proj_section(hb, w_ref, 4), cw_ref, cb_ref, wa_ref, ba_ref,
                     wx_ref, bx_ref, lam_ref, xbuf, a_scr, b_scr, g_scr)
        _proj_qkv(hb, w_ref, qg_ref, kg_ref, gm_ref, q_ref, k_ref, v_ref)
        _rglru_scan(rnn_ref, a_scr, b_scr, h_scr, g_scr)


def _edge(x_in, seq_len, combine_in=None, proj_in=None, layer=None):
    m = x_in.shape[0]
    tm = TOKEN_TILE
    with_combine, with_proj = combine_in is not None, proj_in is not None
    tile = lambda i, *_: (i, 0)
    operands, in_specs, out_specs, out_shape, scratch = [], [], [], [], []
    n_prefetch = 0
    if with_combine:
        chunk_meta, pos, out_rows = combine_in
        operands += list(chunk_meta)
        n_prefetch = len(chunk_meta)
    operands.append(x_in)
    in_specs.append(pl.BlockSpec((tm, D_MODEL), tile))
    if with_combine:
        operands += [pos, out_rows]
        in_specs += [pl.BlockSpec((tm, LANES), tile), pl.BlockSpec(memory_space=pl.ANY)]
        out_specs.append(pl.BlockSpec((tm, D_MODEL), tile))
        out_shape.append(jax.ShapeDtypeStruct((m, D_MODEL), F32))
        scratch += [pltpu.VMEM((2, TILE_ROWS, D_MODEL // 2), U32), pltpu.SMEM((2,), I32),
                    pltpu.SemaphoreType.DMA((2,))]
    if with_proj:
        assert len(proj_in) == N_PROJ_IN
        operands += list(proj_in)
        in_specs += [_layer_spec(a, layer) for a in proj_in]
        slab = pl.BlockSpec((HEAD_PAIRS, tm, LANES), lambda i, *_: (0, i, 0))
        out_specs += [slab, slab, slab, pl.BlockSpec((tm, D_RNN), tile)]
        out_shape += [jax.ShapeDtypeStruct((HEAD_PAIRS, m, LANES), F32)] * 3 + [jax.ShapeDtypeStruct((m, D_RNN), F32)]
        scratch += [pltpu.VMEM((tm + SUBLANES, D_RNN), F32), pltpu.VMEM((tm, D_RNN), F32),
                    pltpu.VMEM((tm, D_RNN), F32), pltpu.VMEM((SUBLANES, D_RNN), F32), pltpu.VMEM((tm, D_RNN), F32)]
    return pl.pallas_call(
        functools.partial(_edge_kernel, with_combine=with_combine, with_proj=with_proj,
                          tiles_per_seq=seq_len // tm),
        grid_spec=pltpu.PrefetchScalarGridSpec(
            num_scalar_prefetch=n_prefetch, grid=(m // tm,),
            in_specs=in_specs, out_specs=out_specs, scratch_shapes=scratch),
        out_shape=out_shape,
        compiler_params=_cparams("arbitrary"),
    )(*operands)


def _dispatch_plan(cnt_tiles, n_tiles, n_blocks):
    cnt = cnt_tiles.reshape(n_tiles, SUBLANES, LANES)[:, 0, ROUTER_LANE0:ROUTER_LANE0 + N_EXPERTS]
    seg = (cnt + ROW_CHUNK - 1) // ROW_CHUNK * ROW_CHUNK
    used = jnp.sum(seg, axis=0)
    padded = (used + EXPERT_BLOCK - 1) // EXPERT_BLOCK * EXPERT_BLOCK
    p_ends = jnp.cumsum(padded)
    p_starts = p_ends - padded
    seg_dst = p_starts[None, :] + jnp.cumsum(seg, axis=0) - seg
    loc_start = jnp.cumsum(seg, axis=1) - seg
    flat = lambda a: a.reshape(-1).astype(I32)
    block_row = jnp.arange(n_blocks, dtype=I32) * EXPERT_BLOCK
    block_expert = jnp.minimum(jnp.sum((p_ends[None, :] <= block_row[:, None]).astype(I32), axis=1), N_EXPERTS - 1)
    n_used = (p_ends[-1:] // EXPERT_BLOCK).astype(I32)
    after = (p_ends // EXPERT_BLOCK).astype(I32)[block_expert]
    next_expert = jnp.where(after < n_used[0], block_expert[jnp.minimum(after, n_blocks - 1)], -1).astype(I32)
    ends = jnp.cumsum(seg // ROW_CHUNK, axis=1)
    j = jnp.arange(TILE_CHUNKS, dtype=I32)
    owner = jnp.minimum(jnp.sum((ends[:, None, :] <= j[None, :, None]).astype(I32), axis=2), N_EXPERTS - 1)
    is_owner = owner[:, :, None] == jnp.arange(N_EXPERTS, dtype=I32)[None, None, :]
    chunk_dst = jnp.sum(jnp.where(is_owner, (seg_dst - loc_start)[:, None, :], 0), axis=2) + ROW_CHUNK * j[None, :]
    chunk_meta = (flat(ends[:, -1]), flat(chunk_dst))
    rest = n_blocks * EXPERT_BLOCK - p_ends[-1:]
    tail_meta = (flat(jnp.concatenate([padded - used, rest]) // ROW_CHUNK),
                 flat(jnp.concatenate([p_starts + used, p_ends[-1:]])))
    occupied_end = jnp.sum(jnp.where(block_expert[:, None] == jnp.arange(N_EXPERTS, dtype=I32)[None, :],
                                     (p_starts + used)[None, :], 0), axis=1)
    valid = jnp.clip(occupied_end - block_row, 0, EXPERT_BLOCK).astype(I32)
    return chunk_meta, tail_meta, (block_expert.astype(I32), n_used, next_expert, valid)


def _constants():
    gm = np.kron(np.eye(MXU_DEPTH // HEAD_DIM), np.full((HEAD_DIM, HEAD_DIM), 1.0 / HEAD_DIM))
    tri = np.tril(np.ones((TOKEN_TILE, TOKEN_TILE), np.float32), -1)
    upper = np.triu(np.ones((LANES, LANES), np.float32), 1)
    return jnp.asarray(gm, BF16), jnp.asarray(tri, BF16), jnp.asarray(upper, BF16)


def kernel(x, rel_bias_table, norm_mix, w_in, q_norm, k_norm, conv_w, conv_b, rg_w_a, rg_b_a, rg_w_x, rg_b_x,
           rg_lambda, norm_attn_out, norm_rnn_out, w_out, norm_ffn, router_group_w, router_group_b,
           router_expert_w, router_expert_b, expert_w_gate, expert_w_up, expert_w_down):
    b, s, d = x.shape
    depth = w_in.shape[0]
    m = b * s
    assert d == D_MODEL and s % ATTN_TILE == 0 and s % TOKEN_TILE == 0

    gm, tri, upper = _constants()
    bias = _bias_tables(rel_bias_table)
    scale = HEAD_DIM ** -0.5
    n_tiles = m // TOKEN_TILE
    n_blocks = -(-(m * TOP_K + n_tiles * N_EXPERTS * (ROW_CHUNK - 1)) // EXPERT_BLOCK) + N_EXPERTS
    n_rows = n_blocks * EXPERT_BLOCK

    vec = lambda v: v.reshape(depth, 1, -1).astype(F32)
    per_mxu = MXU_DEPTH // (D_RNN // RNN_BLOCKS)
    proj_params = (
        vec(norm_mix), w_in.astype(BF16),
        vec(jnp.tile(q_norm, (1, ATTN_HEADS)) * (scale * LOG2E)), vec(jnp.tile(k_norm, (1, ATTN_HEADS))),
        jnp.broadcast_to(gm, (depth,) + gm.shape), conv_w.astype(F32), vec(conv_b),
        _block_diag(rg_w_a, per_mxu).astype(BF16), vec(rg_b_a),
        _block_diag(rg_w_x, per_mxu).astype(BF16), vec(rg_b_x), vec(rg_lambda))
    wr = jnp.zeros((depth, D_MODEL, LANES), F32)
    wr = wr.at[:, :, :N_GROUPS].set(router_group_w).at[:, :, N_GROUPS:N_GROUPS + N_EXPERTS].set(router_expert_w)
    br = jnp.zeros((depth, 1, LANES), F32)
    br = br.at[:, 0, :N_GROUPS].set(router_group_b).at[:, 0, N_GROUPS:N_GROUPS + N_EXPERTS].set(router_expert_b)
    mix_params = (vec(norm_attn_out), vec(norm_rnn_out), w_out.astype(BF16), vec(norm_ffn),
                  jnp.concatenate(_split_bf16(wr), axis=2), br)

    x2 = x.reshape(m, d).astype(F32)
    q, k, v, rnn = _edge(x2, s, proj_in=proj_params, layer=0)
    for l in range(depth):
        attn = _attention(q, k, v, bias, b, s)
        x1, h2, pos, gates, cnt = _mix(x2, attn, rnn, mix_params, l, tri, upper)

        chunk_meta, tail_meta, block_meta = _dispatch_plan(cnt, n_tiles, n_blocks)
        rows = _dispatch(chunk_meta + tail_meta, h2, pos, gates, n_rows)
        out_rows = _experts(*block_meta, rows, expert_w_gate, expert_w_up, expert_w_down, l)
        nxt = proj_params if l + 1 < depth else None
        x2, *started = _edge(x1, s, combine_in=(chunk_meta, pos, out_rows), proj_in=nxt, layer=l + 1)
        if started:
            q, k, v, rnn = started
    return x2.reshape(b, s, d).astype(x.dtype)
```
